```python
import jax, jax.numpy as jnp
from jax import lax
import numpy as np

D_MODEL = 1024
BATCH = 16
SEQ = 2048
DEPTH = 2

BLOCK = 128
EPS = 1e-6
ROPE_BASE = 10000.0
MLA_HEADS = 8
MLA_NOPE = 64
MLA_ROPE = 32
MLA_V = 64
MLA_Q_RANK = 384
MLA_KV_RANK = 256
RET_HEADS = 8
RET_DK = 64
RET_DV = 128
SWA_HEADS = 16
SWA_KV_HEADS = 2
SWA_DIM = 64
WINDOW = 128
SB_HEADS = 8
SB_DIM = 64
D_FF = 3584
N_EXPERTS = 8
TOP_K = 2
MOE_BLOCK = 256

AB_SPLITS = (MLA_Q_RANK, MLA_KV_RANK, MLA_ROPE, RET_HEADS * RET_DK, RET_HEADS * RET_DK, RET_HEADS * RET_DV, RET_HEADS * RET_DV)
AB_IN = sum(AB_SPLITS)
AB_MIX = MLA_HEADS * MLA_V + RET_HEADS * RET_DV
CD_SPLITS = (SWA_HEADS * SWA_DIM, SWA_KV_HEADS * SWA_DIM, SWA_KV_HEADS * SWA_DIM, SB_HEADS * SB_DIM, SB_HEADS * SB_DIM, SB_HEADS * SB_DIM)
CD_IN = sum(CD_SPLITS)
CD_MIX = SWA_HEADS * SWA_DIM + SB_HEADS * SB_DIM

kernel_name = "hybrid_mla_retention_swa_stickbreak_moe"


def _split(x, sizes):
    return jnp.split(x, [int(i) for i in np.cumsum(sizes)[:-1]], axis=-1)


def _rmsnorm(x, g):
    xf = x.astype(jnp.float32)
    y = xf * lax.rsqrt(jnp.mean(xf * xf, axis=-1, keepdims=True) + EPS)
    return (y * g.astype(jnp.float32)).astype(x.dtype)


def _rope(x, pos):
    half = x.shape[-1] // 2
    inv_freq = ROPE_BASE ** (-jnp.arange(half, dtype=jnp.float32) / half)
    ang = pos.astype(jnp.float32)[:, None] * inv_freq[None, :]
    cos = jnp.cos(ang)[:, None, :]
    sin = jnp.sin(ang)[:, None, :]
    xf = x.astype(jnp.float32)
    x1, x2 = xf[..., :half], xf[..., half:]
    return jnp.concatenate([x1 * cos - x2 * sin, x1 * sin + x2 * cos], axis=-1).astype(x.dtype)


def _to_blocks(t):
    B, S = t.shape[:2]
    return jnp.moveaxis(t.reshape((B, S // BLOCK, BLOCK) + t.shape[2:]), 1, 0)


def _from_blocks(t):
    t = jnp.moveaxis(t, 0, 1)
    return t.reshape((t.shape[0], t.shape[1] * t.shape[2]) + t.shape[3:])


def _causal_softmax_attention(q, k, v):
    S = q.shape[1]
    scale = q.shape[-1] ** -0.5
    kpos = jnp.arange(S)

    def block(args):
        qi, i = args
        s = jnp.einsum('bqhd,bkhd->bhqk', qi, k, preferred_element_type=jnp.float32) * scale
        qpos = i * BLOCK + jnp.arange(BLOCK)
        s = jnp.where(kpos[None, :] <= qpos[:, None], s, -jnp.inf)
        p = jax.nn.softmax(s, axis=-1)
        return jnp.einsum('bhqk,bkhd->bqhd', p.astype(v.dtype), v)

    return _from_blocks(lax.map(block, (_to_blocks(q), jnp.arange(S // BLOCK))))


def _retention(q, k, v):
    B, S, H, dk = q.shape
    dv = v.shape[-1]
    log_gamma = jnp.log1p(-(2.0 ** (-5.0 - jnp.arange(H, dtype=jnp.float32))))
    idx = jnp.arange(BLOCK, dtype=jnp.float32)
    rel = idx[:, None] - idx[None, :]
    decay_in = jnp.where(rel >= 0, jnp.exp(log_gamma[:, None, None] * jnp.maximum(rel, 0.0)), 0.0)
    q_decay = jnp.exp(log_gamma[:, None] * (idx + 1.0))
    k_decay = jnp.exp(log_gamma[:, None] * (BLOCK - 1.0 - idx))
    chunk_decay = jnp.exp(log_gamma * BLOCK)
    qc = _to_blocks(q.astype(jnp.float32))
    kc = _to_blocks(k.astype(jnp.float32)) * (dk ** -0.5)
    vc = _to_blocks(v.astype(jnp.float32))
    inner = jnp.einsum('nbqhd,nbkhd->nbhqk', qc, kc) * decay_in[None, None]
    out_inner = jnp.einsum('nbhqk,nbkhv->nbqhv', inner, vc)
    kv_chunk = jnp.einsum('nbkhd,hk,nbkhv->nbhdv', kc, k_decay, vc)

    def step(state, kv):
        return state * chunk_decay[None, :, None, None] + kv, state

    _, prev = lax.scan(step, jnp.zeros((B, H, dk, dv), jnp.float32), kv_chunk)
    out_cross = jnp.einsum('nbqhd,hq,nbhdv->nbqhv', qc, q_decay, prev)
    return _from_blocks(out_inner + out_cross)


def _head_groupnorm(y, g):
    B, S, H, dv = y.shape
    mu = jnp.mean(y, axis=-1, keepdims=True)
    var = jnp.mean(jnp.square(y - mu), axis=-1, keepdims=True)
    yn = (y - mu) * lax.rsqrt(var + EPS)
    return yn.reshape(B, S, H * dv) * g.astype(jnp.float32)


def _mixer_mla_retention(h, w_in, q_norm, w_uq, kv_norm, w_ukv, ret_gn, w_out):
    B, S, _ = h.shape
    pos = jnp.arange(S)
    q_lat, kv_lat, k_rope, rq, rk, rv, rg = _split(h @ w_in, AB_SPLITS)
    qh = (_rmsnorm(q_lat, q_norm) @ w_uq).reshape(B, S, MLA_HEADS, MLA_NOPE + MLA_ROPE)
    kvh = (_rmsnorm(kv_lat, kv_norm) @ w_ukv).reshape(B, S, MLA_HEADS, MLA_NOPE + MLA_V)
    q = jnp.concatenate([qh[..., :MLA_NOPE], _rope(qh[..., MLA_NOPE:], pos)], axis=-1)
    k_pe = jnp.broadcast_to(_rope(k_rope[:, :, None, :], pos), (B, S, MLA_HEADS, MLA_ROPE))
    k = jnp.concatenate([kvh[..., :MLA_NOPE], k_pe], axis=-1)
    mla = _causal_softmax_attention(q, k, kvh[..., MLA_NOPE:]).reshape(B, S, MLA_HEADS * MLA_V)
    ret = _retention(_rope(rq.reshape(B, S, RET_HEADS, RET_DK), pos),
                     _rope(rk.reshape(B, S, RET_HEADS, RET_DK), pos),
                     rv.reshape(B, S, RET_HEADS, RET_DV))
    ret = jax.nn.silu(rg.astype(jnp.float32)) * _head_groupnorm(ret, ret_gn)
    mixed = jnp.concatenate([mla, ret.astype(h.dtype)], axis=-1)
    return mixed @ w_out


def _alibi_slopes(n):
    return 2.0 ** (-8.0 * jnp.arange(1, n + 1, dtype=jnp.float32) / n)


def _sliding_window_sink_attention(q, k, v, sinks):
    B, S, HQ, d = q.shape
    HKV = k.shape[2]
    G = HQ // HKV
    nb = S // WINDOW
    scale = d ** -0.5

    def band(t):
        tb = t.reshape(B, nb, WINDOW, HKV, d)
        prev = jnp.pad(tb, ((0, 0), (1, 0), (0, 0), (0, 0), (0, 0)))[:, :-1]
        return jnp.moveaxis(jnp.concatenate([prev, tb], axis=2), 1, 0)

    qb = jnp.moveaxis(q.reshape(B, nb, WINDOW, HKV, G, d), 1, 0)
    slopes = _alibi_slopes(HQ).reshape(HKV, G)[:, :, None, None]
    sink = sinks.astype(jnp.float32).reshape(HKV, G)[None, :, :, None, None]
    dist = jnp.arange(WINDOW)[:, None] + WINDOW - jnp.arange(2 * WINDOW)[None, :]
    in_window = (dist >= 0) & (dist < WINDOW)
    dist_f = dist.astype(jnp.float32)

    def block(args):
        qi, ki, vi, i = args
        s = jnp.einsum('bqhgd,bkhd->bhgqk', qi, ki, preferred_element_type=jnp.float32) * scale - slopes * dist_f
        valid = in_window & (((i - 1) * WINDOW + jnp.arange(2 * WINDOW)) >= 0)[None, :]
        s = jnp.where(valid, s, -jnp.inf)
        m = jnp.maximum(jnp.max(s, axis=-1, keepdims=True), sink)
        p = jnp.exp(s - m)
        p = p / (jnp.sum(p, axis=-1, keepdims=True) + jnp.exp(sink - m))
        return jnp.einsum('bhgqk,bkhd->bqhgd', p.astype(vi.dtype), vi)

    out = lax.map(block, (qb, band(k), band(v), jnp.arange(nb)))
    return jnp.moveaxis(out, 0, 1).reshape(B, S, HQ * d)


def _stick_breaking_attention(q, k, v):
    S = q.shape[1]
    scale = q.shape[-1] ** -0.5
    kpos = jnp.arange(S)

    def block(args):
        qi, i = args
        z = jnp.einsum('bqhd,bkhd->bhqk', qi, k, preferred_element_type=jnp.float32) * scale
        qpos = i * BLOCK + jnp.arange(BLOCK)
        before = kpos[None, :] < qpos[:, None]
        log_keep = jnp.where(before, jax.nn.log_sigmoid(-z), 0.0)
        log_after = lax.cumsum(log_keep, axis=3, reverse=True) - log_keep
        a = jnp.where(before, jnp.exp(jax.nn.log_sigmoid(z) + log_after), 0.0)
        return jnp.einsum('bhqk,bkhd->bqhd', a.astype(v.dtype), v)

    return _from_blocks(lax.map(block, (_to_blocks(q), jnp.arange(S // BLOCK))))


def _mixer_swa_stickbreak(h, w_in, sinks, w_out):
    B, S, _ = h.shape
    sq, sk, sv, bq, bk, bv = _split(h @ w_in, CD_SPLITS)
    swa = _sliding_window_sink_attention(sq.reshape(B, S, SWA_HEADS, SWA_DIM),
                                         sk.reshape(B, S, SWA_KV_HEADS, SWA_DIM),
                                         sv.reshape(B, S, SWA_KV_HEADS, SWA_DIM), sinks)
    sb = _stick_breaking_attention(bq.reshape(B, S, SB_HEADS, SB_DIM),
                                   bk.reshape(B, S, SB_HEADS, SB_DIM),
                                   bv.reshape(B, S, SB_HEADS, SB_DIM)).reshape(B, S, SB_HEADS * SB_DIM)
    return jnp.concatenate([swa, sb], axis=-1) @ w_out


def _swiglu(h, w_gate, w_up, w_down):
    return (jax.nn.silu(h @ w_gate) * (h @ w_up)) @ w_down


def _moe(h, router, w_gate, w_up, w_down):
    B, S, D = h.shape
    T = B * S
    TK = T * TOP_K
    xf = h.reshape(T, D)
    logits = (xf @ router).astype(jnp.float32)
    top_logits, top_idx = lax.top_k(logits, TOP_K)
    gates = jax.nn.softmax(top_logits, axis=-1)
    expert = top_idx.reshape(-1).astype(jnp.int32)
    token = jnp.repeat(jnp.arange(T, dtype=jnp.int32), TOP_K)
    gate = gates.reshape(-1)
    order = jnp.argsort(expert)
    se, st, sg = expert[order], token[order], gate[order]
    counts = jnp.zeros((N_EXPERTS,), jnp.int32).at[expert].add(1)
    padded = ((counts + MOE_BLOCK - 1) // MOE_BLOCK) * MOE_BLOCK
    pad_end = jnp.cumsum(padded)
    pad_start = pad_end - padded
    grp_start = jnp.cumsum(counts) - counts
    dest = pad_start[se] + (jnp.arange(TK, dtype=jnp.int32) - grp_start[se])
    cap = -(-TK // MOE_BLOCK) * MOE_BLOCK + N_EXPERTS * MOE_BLOCK
    nblk = cap // MOE_BLOCK
    buf_tok = jnp.zeros((cap,), jnp.int32).at[dest].set(st)
    buf_gate = jnp.zeros((cap,), jnp.float32).at[dest].set(sg)
    blk_start = jnp.arange(nblk, dtype=jnp.int32) * MOE_BLOCK
    blk_exp = jnp.minimum(jnp.searchsorted(pad_end, blk_start, side='right'), N_EXPERTS - 1)
    xg = xf[buf_tok].reshape(nblk, MOE_BLOCK, D)

    def expert_block(args):
        xb, e = args
        return (jax.nn.silu(xb @ w_gate[e]) * (xb @ w_up[e])) @ w_down[e]

    yg = lax.map(expert_block, (xg, blk_exp)).reshape(cap, D)
    out = jnp.zeros((T, D), h.dtype).at[buf_tok].add((yg * buf_gate[:, None]).astype(h.dtype))
    return out.reshape(B, S, D)


def setup_inputs(seed: int = 0) -> dict:
    key = jax.random.key(seed)
    ks = iter(jax.random.split(key, 32))
    NE = (DEPTH + 1) // 2
    NO = DEPTH // 2

    def w(shape, fan_in):
        return jax.random.normal(next(ks), shape, jnp.float32) * (fan_in ** -0.5)

    def gain(shape):
        return 1.0 + 0.02 * jax.random.normal(next(ks), shape, jnp.float32)

    return {
        'x': jax.random.normal(next(ks), (BATCH, SEQ, D_MODEL), jnp.float32),
        'ab_norm': gain((NE, D_MODEL)),
        'ab_w_in': w((NE, D_MODEL, AB_IN), D_MODEL),
        'mla_q_norm': gain((NE, MLA_Q_RANK)),
        'mla_w_uq': w((NE, MLA_Q_RANK, MLA_HEADS * (MLA_NOPE + MLA_ROPE)), MLA_Q_RANK),
        'mla_kv_norm': gain((NE, MLA_KV_RANK)),
        'mla_w_ukv': w((NE, MLA_KV_RANK, MLA_HEADS * (MLA_NOPE + MLA_V)), MLA_KV_RANK),
        'ret_gn': gain((NE, RET_HEADS * RET_DV)),
        'ab_w_out': w((NE, AB_MIX, D_MODEL), AB_MIX),
        'ffn_norm': gain((NE, D_MODEL)),
        'ffn_w_gate': w((NE, D_MODEL, D_FF), D_MODEL),
        'ffn_w_up': w((NE, D_MODEL, D_FF), D_MODEL),
        'ffn_w_down': w((NE, D_FF, D_MODEL), D_FF),
        'cd_norm': gain((NO, D_MODEL)),
        'cd_w_in': w((NO, D_MODEL, CD_IN), D_MODEL),
        'swa_sinks': 0.5 * jax.random.normal(next(ks), (NO, SWA_HEADS), jnp.float32),
        'cd_w_out': w((NO, CD_MIX, D_MODEL), CD_MIX),
        'moe_norm': gain((NO, D_MODEL)),
        'moe_router': w((NO, D_MODEL, N_EXPERTS), D_MODEL),
        'moe_w_gate': w((NO, N_EXPERTS, D_MODEL, D_FF), D_MODEL),
        'moe_w_up': w((NO, N_EXPERTS, D_MODEL, D_FF), D_MODEL),
        'moe_w_down': w((NO, N_EXPERTS, D_FF, D_MODEL), D_FF),
        'final_norm': gain((D_MODEL,)),
    }


def reference(x, ab_norm, ab_w_in, mla_q_norm, mla_w_uq, mla_kv_norm, mla_w_ukv, ret_gn, ab_w_out,
              ffn_norm, ffn_w_gate, ffn_w_up, ffn_w_down, cd_norm, cd_w_in, swa_sinks, cd_w_out,
              moe_norm, moe_router, moe_w_gate, moe_w_up, moe_w_down, final_norm):
    h = x
    for layer in range(DEPTH):
        i = layer // 2
        if layer % 2 == 0:
            h = h + _mixer_mla_retention(_rmsnorm(h, ab_norm[i]), ab_w_in[i], mla_q_norm[i], mla_w_uq[i],
                                         mla_kv_norm[i], mla_w_ukv[i], ret_gn[i], ab_w_out[i])
            h = h + _swiglu(_rmsnorm(h, ffn_norm[i]), ffn_w_gate[i], ffn_w_up[i], ffn_w_down[i])
        else:
            h = h + _mixer_swa_stickbreak(_rmsnorm(h, cd_norm[i]), cd_w_in[i], swa_sinks[i], cd_w_out[i])
            h = h + _moe(_rmsnorm(h, moe_norm[i]), moe_router[i], moe_w_gate[i], moe_w_up[i], moe_w_down[i])
    return _rmsnorm(h, final_norm)
```

```python
import functools

import jax
import jax.numpy as jnp
import numpy as np
from jax import lax
from jax.experimental import pallas as pl
from jax.experimental.pallas import tpu as pltpu

F32 = jnp.float32
BF16 = jnp.bfloat16

LANES = 128
EPS = 1e-6
ROPE_BASE = 10000.0
CHUNK = 128
MLA_HEADS, MLA_NOPE, MLA_ROPE, MLA_V = 8, 64, 32, 64
MLA_Q_RANK, MLA_KV_RANK = 384, 256
RET_HEADS, RET_DK, RET_DV = 8, 64, 128
SWA_HEADS, SWA_KV_HEADS, SWA_DIM, WINDOW = 16, 2, 64, 128
SB_HEADS, SB_DIM = 8, 64
N_EXPERTS, TOP_K = 8, 2

VMEM_LIMIT = 56 * 1024 * 1024
ROW_TILE = 512
FF_TILE = 512
MOE_ROWS = 512
MLA_TQ = 512
SB_T = 256


def _cparams(sem):
    return pltpu.CompilerParams(dimension_semantics=sem, vmem_limit_bytes=VMEM_LIMIT)


def _rms(x, g):
    return x * lax.rsqrt(jnp.mean(x * x, axis=-1, keepdims=True) + EPS) * g


def _rope_slab(x, cos, sin_signed, half):
    lane = lax.broadcasted_iota(jnp.int32, x.shape, 1)
    first = (lane % (2 * half)) < half
    partner = jnp.where(first, pltpu.roll(x, LANES - half, 1), pltpu.roll(x, half, 1))
    return x * cos + partner * sin_signed


def _nt_dot(a, b):
    return lax.dot_general(a, b, (((1,), (1,)), ((), ())), preferred_element_type=F32)


def _dot(a, b):
    return jnp.dot(a, b, preferred_element_type=F32)


def _proj0_kernel(h_ref, g_ref, win_ref, qn_ref, wuq_ref, kvn_ref, wukv_ref, cm_ref, sm_ref, cr_ref, sr_ref,
                  qnope_ref, qrope_ref, knope_ref, krope_ref, v_ref, rq_ref, rk_ref, rv_ref, rg_ref):
    xn = _rms(h_ref[...], g_ref[...]).astype(BF16)

    def mm(lo, hi):
        return _dot(xn, win_ref[:, lo:hi])

    cm, sm, cr, sr = cm_ref[...], sm_ref[...], cr_ref[...], sr_ref[...]
    q_scale = (MLA_NOPE + MLA_ROPE) ** -0.5
    qh = _dot(_rms(mm(0, 384), qn_ref[...]).astype(BF16), wuq_ref[...])
    qnope_ref[...] = (qh[:, :512] * q_scale).astype(BF16)
    for s in range(2):
        slab = qh[:, 512 + LANES * s:512 + LANES * (s + 1)]
        qrope_ref[:, LANES * s:LANES * (s + 1)] = (_rope_slab(slab, cm, sm, MLA_ROPE // 2) * q_scale).astype(BF16)
    kvh = _dot(_rms(mm(384, 640), kvn_ref[...]).astype(BF16), wukv_ref[...])
    knope_ref[...] = kvh[:, :512].astype(BF16)
    v_ref[...] = kvh[:, 512:].astype(BF16)
    krope_ref[...] = _rope_slab(mm(640, 768), cm, sm, MLA_ROPE // 2).astype(BF16)
    rq = mm(768, 1280)
    rk = mm(1280, 1792)
    for s in range(4):
        sl = slice(LANES * s, LANES * (s + 1))
        rq_ref[:, sl] = _rope_slab(rq[:, sl], cr, sr, RET_DK // 2).astype(BF16)
        rk_ref[:, sl] = (_rope_slab(rk[:, sl], cr, sr, RET_DK // 2) * (RET_DK ** -0.5)).astype(BF16)
    rv_ref[...] = mm(1792, 2816).astype(BF16)
    rg = mm(2816, 3840)
    rg_ref[...] = (rg * (1.0 / (1.0 + jnp.exp(-rg)))).astype(BF16)


def _proj0(h, g, win, qn, wuq, kvn, wukv, cm, sm, cr, sr, seq):
    T, D = h.shape
    tm = ROW_TILE
    nseq = seq // tm
    row = lambda w: pl.BlockSpec((tm, w), lambda i: (i, 0))
    full = lambda a: pl.BlockSpec(a.shape, lambda i: (0, 0))
    pos = pl.BlockSpec((tm, LANES), lambda i: (i % nseq, 0))
    widths = (512, 256, 512, 128, 512, 512, 512, 1024, 1024)
    return pl.pallas_call(
        _proj0_kernel,
        grid=(T // tm,),
        in_specs=[row(D), full(g), full(win), full(qn), full(wuq), full(kvn), full(wukv), pos, pos, pos, pos],
        out_specs=[row(w) for w in widths],
        out_shape=[jax.ShapeDtypeStruct((T, w), BF16) for w in widths],
        compiler_params=_cparams(("parallel",)),
        name="proj0",
    )(h, g, win, qn, wuq, kvn, wukv, cm, sm, cr, sr)


def _mla_kernel(qn_ref, qr_ref, kn_ref, kr_ref, v_ref, o_ref, kk_ref, *, tq):
    S = qn_ref.shape[0]
    p = pl.program_id(1)
    kk_ref[:, :LANES] = kn_ref[...]
    kk_ref[:, LANES:] = kr_ref[...]
    lane = lax.broadcasted_iota(jnp.int32, (1, LANES), 1)
    quarter = (p % 2) * 2
    row = lax.broadcasted_iota(jnp.int32, (tq, tq), 0)
    col = lax.broadcasted_iota(jnp.int32, (tq, tq), 1)
    causal = col <= row
    for i in range(S // tq):
        lo, hi = i * tq, (i + 1) * tq
        qn = qn_ref[lo:hi, :]
        qr = qr_ref[lo:hi, :]
        outs = []
        for hh in range(2):
            mn = (lane // 64 == hh).astype(BF16)
            mr = (lane // 32 == quarter + hh).astype(BF16)
            qa = jnp.concatenate([qn * mn, qr * mr], axis=1)
            sd = jnp.where(causal, _nt_dot(qa, kk_ref[lo:hi, :]), -jnp.inf)
            m = jnp.max(sd, axis=-1, keepdims=True)
            if i > 0:
                so = _nt_dot(qa, kk_ref[:lo, :])
                m = jnp.maximum(m, jnp.max(so, axis=-1, keepdims=True))
            pd = jnp.exp(sd - m)
            l = jnp.sum(pd, axis=-1, keepdims=True)
            acc = _dot(pd.astype(BF16), v_ref[lo:hi, :])
            if i > 0:
                po = jnp.exp(so - m)
                l = l + jnp.sum(po, axis=-1, keepdims=True)
                acc = acc + _dot(po.astype(BF16), v_ref[:lo, :])
            outs.append(acc / l)
        o_ref[lo:hi, :] = jnp.where(lane < 64, outs[0], outs[1]).astype(BF16)


def _mla_attention(qnope, qrope, knope, krope, v, batch, seq):
    T = qnope.shape[0]
    blk = lambda f: pl.BlockSpec((seq, LANES), f)
    return pl.pallas_call(
        functools.partial(_mla_kernel, tq=MLA_TQ),
        grid=(batch, MLA_HEADS // 2),
        in_specs=[blk(lambda b, p: (b, p)), blk(lambda b, p: (b, p // 2)), blk(lambda b, p: (b, p)),
                  blk(lambda b, p: (b, 0)), blk(lambda b, p: (b, p))],
        out_specs=blk(lambda b, p: (b, p)),
        out_shape=jax.ShapeDtypeStruct((T, MLA_HEADS * MLA_V), BF16),
        scratch_shapes=[pltpu.VMEM((seq, 2 * LANES), BF16)],
        compiler_params=_cparams(("parallel", "parallel")),
        name="mla_attention",
    )(qnope, qrope, knope, krope, v)


def _ret_kernel(q_ref, k_ref, v_ref, g_ref, gn_ref, din_ref, qd_ref, kd_ref, cd_ref, o_ref):
    S = q_ref.shape[0]
    h = pl.program_id(1)
    lane = lax.broadcasted_iota(jnp.int32, (1, LANES), 1)
    qmask = (lane // RET_DK == h % 2).astype(F32)
    din = din_ref[0]
    qd = qd_ref[0] * qmask
    kd = kd_ref[0]
    cd = cd_ref[0]
    gain = gn_ref[...]

    def chunk(c, state):
        rows = pl.ds(pl.multiple_of(c * CHUNK, CHUNK), CHUNK)
        q = q_ref[rows, :].astype(F32)
        k = k_ref[rows, :]
        v = v_ref[rows, :]
        inner = _nt_dot((q * qmask).astype(BF16), k) * din
        y = _dot(inner.astype(BF16), v) + _dot((q * qd).astype(BF16), state.astype(BF16))
        kt = (k.astype(F32) * kd).T.astype(BF16)
        state = state * cd + _dot(kt, v)
        mu = jnp.mean(y, axis=-1, keepdims=True)
        var = jnp.mean(jnp.square(y - mu), axis=-1, keepdims=True)
        yn = (y - mu) * lax.rsqrt(var + EPS) * gain
        o_ref[rows, :] = (g_ref[rows, :].astype(F32) * yn).astype(BF16)
        return state

    lax.fori_loop(0, S // CHUNK, chunk, jnp.zeros((LANES, RET_DV), F32))


def _retention(rq, rk, rv, rg, gn, din, qd, kd, cd, batch, seq):
    T = rq.shape[0]
    blk = lambda f: pl.BlockSpec((seq, LANES), f)
    tab = pl.BlockSpec((1, CHUNK, LANES), lambda b, h: (h, 0, 0))
    return pl.pallas_call(
        _ret_kernel,
        grid=(batch, RET_HEADS),
        in_specs=[blk(lambda b, h: (b, h // 2)), blk(lambda b, h: (b, h // 2)), blk(lambda b, h: (b, h)),
                  blk(lambda b, h: (b, h)), pl.BlockSpec((1, LANES), lambda b, h: (0, h)), tab, tab, tab,
                  pl.BlockSpec((1, 1, LANES), lambda b, h: (h, 0, 0))],
        out_specs=blk(lambda b, h: (b, h)),
        out_shape=jax.ShapeDtypeStruct((T, RET_HEADS * RET_DV), BF16),
        compiler_params=_cparams(("parallel", "parallel")),
        name="retention",
    )(rq, rk, rv, rg, gn, din, qd, kd, cd)


def _ffn_kernel(h_ref, a_ref, b_ref, wa_ref, wb_ref, g_ref, wg_ref, wu_ref, wd_ref, o_ref, hn_ref, xn_ref, acc_ref):
    j = pl.program_id(1)

    @pl.when(j == 0)
    def _():
        hn = h_ref[...] + _dot(a_ref[...], wa_ref[...]) + _dot(b_ref[...], wb_ref[...])
        hn_ref[...] = hn
        xn_ref[...] = _rms(hn, g_ref[...]).astype(BF16)
        acc_ref[...] = jnp.zeros_like(acc_ref)

    xn = xn_ref[...]
    gate = _dot(xn, wg_ref[...])
    act = gate * (1.0 / (1.0 + jnp.exp(-gate))) * _dot(xn, wu_ref[...])
    acc_ref[...] += _dot(act.astype(BF16), wd_ref[...])

    @pl.when(j == pl.num_programs(1) - 1)
    def _():
        o_ref[...] = hn_ref[...] + acc_ref[...]


def _mixer_out_ffn(h, a, b, wa, wb, g, wg, wu, wd):
    T, D = h.shape
    F = wg.shape[1]
    tm, tf = ROW_TILE, FF_TILE
    row = lambda w: pl.BlockSpec((tm, w), lambda i, j: (i, 0))
    full = lambda x: pl.BlockSpec(x.shape, lambda i, j: (0, 0))
    return pl.pallas_call(
        _ffn_kernel,
        grid=(T // tm, F // tf),
        in_specs=[row(D), row(a.shape[1]), row(b.shape[1]), full(wa), full(wb), full(g),
                  pl.BlockSpec((D, tf), lambda i, j: (0, j)), pl.BlockSpec((D, tf), lambda i, j: (0, j)),
                  pl.BlockSpec((tf, D), lambda i, j: (j, 0))],
        out_specs=row(D),
        out_shape=jax.ShapeDtypeStruct((T, D), F32),
        scratch_shapes=[pltpu.VMEM((tm, D), F32), pltpu.VMEM((tm, D), BF16), pltpu.VMEM((tm, D), F32)],
        compiler_params=_cparams(("parallel", "arbitrary")),
        name="mixer_out_ffn",
    )(h, a, b, wa, wb, g, wg, wu, wd)


def _proj1_kernel(h_ref, g_ref, win_ref, sq_ref, sk_ref, sv_ref, bq_ref, bk_ref, bv_ref):
    xn = _rms(h_ref[...], g_ref[...]).astype(BF16)

    def mm(lo, hi):
        return _dot(xn, win_ref[:, lo:hi])

    sq_ref[...] = (mm(0, 1024) * (SWA_DIM ** -0.5)).astype(BF16)
    sk_ref[...] = mm(1024, 1152).astype(BF16)
    sv_ref[...] = mm(1152, 1280).astype(BF16)
    bq_ref[...] = (mm(1280, 1792) * (SB_DIM ** -0.5)).astype(BF16)
    bk_ref[...] = mm(1792, 2304).astype(BF16)
    bv_ref[...] = mm(2304, 2816).astype(BF16)


def _proj1(h, g, win):
    T, D = h.shape
    tm = ROW_TILE
    row = lambda w: pl.BlockSpec((tm, w), lambda i: (i, 0))
    full = lambda a: pl.BlockSpec(a.shape, lambda i: (0, 0))
    widths = (1024, 128, 128, 512, 512, 512)
    return pl.pallas_call(
        _proj1_kernel,
        grid=(T // tm,),
        in_specs=[row(D), full(g), full(win)],
        out_specs=[row(w) for w in widths],
        out_shape=[jax.ShapeDtypeStruct((T, w), BF16) for w in widths],
        compiler_params=_cparams(("parallel",)),
        name="proj1",
    )(h, g, win)


def _swa_kernel(tab_ref, q_ref, k_ref, v_ref, o_ref):
    S = q_ref.shape[0]
    W = WINDOW
    j = pl.program_id(1)
    lane = lax.broadcasted_iota(jnp.int32, (1, LANES), 1)
    r2 = lax.broadcasted_iota(jnp.int32, (W, 2 * W), 0)
    c2 = lax.broadcasted_iota(jnp.int32, (W, 2 * W), 1)
    dist2 = r2 + W - c2
    r1 = lax.broadcasted_iota(jnp.int32, (W, W), 0)
    c1 = lax.broadcasted_iota(jnp.int32, (W, W), 1)
    dist1 = r1 - c1
    for i in range(S // W):
        q = q_ref[i * W:(i + 1) * W, :]
        if i == 0:
            k, v, dist = k_ref[0:W, :], v_ref[0:W, :], dist1
        else:
            k, v, dist = k_ref[(i - 1) * W:(i + 1) * W, :], v_ref[(i - 1) * W:(i + 1) * W, :], dist2
        valid = (dist >= 0) & (dist < W)
        distf = dist.astype(F32)
        outs = []
        for hh in range(2):
            head = j + (SWA_HEADS // 2) * hh
            sink = tab_ref[head]
            slope = tab_ref[SWA_HEADS + head]
            s = _nt_dot(q * (lane // 64 == hh).astype(BF16), k) - slope * distf
            s = jnp.where(valid, s, -jnp.inf)
            m = jnp.maximum(jnp.max(s, axis=-1, keepdims=True), sink)
            p = jnp.exp(s - m)
            den = jnp.sum(p, axis=-1, keepdims=True) + jnp.exp(sink - m)
            outs.append(_dot(p.astype(BF16), v) / den)
        o_ref[i * W:(i + 1) * W, :] = jnp.where(lane < 64, outs[0], outs[1]).astype(BF16)


def _swa_attention(tab, sq, sk, sv, batch, seq):
    T = sq.shape[0]
    blk = lambda f: pl.BlockSpec((seq, LANES), f)
    return pl.pallas_call(
        _swa_kernel,
        grid_spec=pltpu.PrefetchScalarGridSpec(
            num_scalar_prefetch=1,
            grid=(batch, SWA_HEADS // 2),
            in_specs=[blk(lambda b, j, t: (b, j)), blk(lambda b, j, t: (b, 0)), blk(lambda b, j, t: (b, 0))],
            out_specs=blk(lambda b, j, t: (b, j)),
        ),
        out_shape=jax.ShapeDtypeStruct((T, SWA_HEADS * SWA_DIM), BF16),
        compiler_params=_cparams(("parallel", "parallel")),
        name="swa_attention",
    )(tab, sq, sk, sv)


def _sb_kernel(q_ref, k_ref, v_ref, o_ref, *, t):
    S = q_ref.shape[0]
    lane = lax.broadcasted_iota(jnp.int32, (1, LANES), 1)
    row = lax.broadcasted_iota(jnp.int32, (t, t), 0)
    col = lax.broadcasted_iota(jnp.int32, (t, t), 1)
    before = col < row
    later = (row > col).astype(BF16)

    def log_keep(z):
        return -(jnp.maximum(z, 0.0) + jnp.log(1.0 + jnp.exp(-jnp.abs(z))))

    def q_tile(i, carry):
        rows = pl.ds(pl.multiple_of(i * t, t), t)
        q = q_ref[rows, :]
        outs = []
        for hh in range(2):
            qm = q * (lane // 64 == hh).astype(BF16)
            z = _nt_dot(qm, k_ref[rows, :])
            lk = jnp.where(before, log_keep(z), 0.0)
            after = _dot(lk.astype(BF16), later)
            a = jnp.where(before, jnp.exp(z + lk + after), 0.0)
            acc = _dot(a.astype(BF16), v_ref[rows, :])
            run = jnp.sum(lk, axis=-1, keepdims=True)

            def k_tile(n, c):
                acc, run = c
                krows = pl.ds(pl.multiple_of((i - 1 - n) * t, t), t)
                z = _nt_dot(qm, k_ref[krows, :])
                lk = log_keep(z)
                after = _dot(lk.astype(BF16), later) + run
                a = jnp.exp(z + lk + after)
                return acc + _dot(a.astype(BF16), v_ref[krows, :]), run + jnp.sum(lk, axis=-1, keepdims=True)

            acc, _ = lax.fori_loop(0, i, k_tile, (acc, run))
            outs.append(acc)
        o_ref[rows, :] = jnp.where(lane < 64, outs[0], outs[1]).astype(BF16)
        return carry

    lax.fori_loop(0, S // t, q_tile, 0)


def _sb_attention(bq, bk, bv, batch, seq):
    T = bq.shape[0]
    blk = pl.BlockSpec((seq, LANES), lambda b, p: (b, p))
    return pl.pallas_call(
        functools.partial(_sb_kernel, t=SB_T),
        grid=(batch, SB_HEADS // 2),
        in_specs=[blk, blk, blk],
        out_specs=blk,
        out_shape=jax.ShapeDtypeStruct((T, SB_HEADS * SB_DIM), BF16),
        compiler_params=_cparams(("parallel", "parallel")),
        name="sb_attention",
    )(bq, bk, bv)


def _split_bf16(x):
    hi = x.astype(BF16)
    return hi, (x - hi.astype(F32)).astype(BF16)


def _out_router_kernel(h_ref, a_ref, b_ref, wa_ref, wb_ref, g_ref, rhi_ref, rlo_ref, o_ref, route_ref):
    hn = h_ref[...] + _dot(a_ref[...], wa_ref[...]) + _dot(b_ref[...], wb_ref[...])
    o_ref[...] = hn
    xhi, xlo = _split_bf16(_rms(hn, g_ref[...]))
    logits = _dot(xhi, rhi_ref[...]) + (_dot(xhi, rlo_ref[...]) + _dot(xlo, rhi_ref[...]))
    lane = lax.broadcasted_iota(jnp.int32, logits.shape, 1)
    logits = jnp.where(lane < N_EXPERTS, logits, -jnp.inf)
    m1 = jnp.max(logits, axis=-1, keepdims=True)
    i1 = jnp.min(jnp.where(logits == m1, lane, LANES), axis=-1, keepdims=True)
    rest = jnp.where(lane == i1, -jnp.inf, logits)
    m2 = jnp.max(rest, axis=-1, keepdims=True)
    i2 = jnp.min(jnp.where(rest == m2, lane, LANES), axis=-1, keepdims=True)
    e2 = jnp.exp(m2 - m1)
    den = 1.0 + e2
    route = jnp.where(lane == 0, i1.astype(F32), 0.0)
    route = jnp.where(lane == 1, i2.astype(F32), route)
    route = jnp.where(lane == 2, 1.0 / den, route)
    route = jnp.where(lane == 3, e2 / den, route)
    route_ref[...] = route


def _mixer_out_router(h, a, b, wa, wb, g, rhi, rlo):
    T, D = h.shape
    tm = ROW_TILE
    row = lambda w: pl.BlockSpec((tm, w), lambda i: (i, 0))
    full = lambda x: pl.BlockSpec(x.shape, lambda i: (0, 0))
    return pl.pallas_call(
        _out_router_kernel,
        grid=(T // tm,),
        in_specs=[row(D), row(a.shape[1]), row(b.shape[1]), full(wa), full(wb), full(g), full(rhi), full(rlo)],
        out_specs=[row(D), row(LANES)],
        out_shape=[jax.ShapeDtypeStruct((T, D), F32), jax.ShapeDtypeStruct((T, LANES), F32)],
        compiler_params=_cparams(("parallel",)),
        name="mixer_out_router",
    )(h, a, b, wa, wb, g, rhi, rlo)


def _moe_kernel(sched_ref, x_ref, g_ref, wg_ref, wu_ref, wd_ref, o_ref, xn_ref, acc_ref):
    i, j = pl.program_id(0), pl.program_id(1)
    nblk = pl.num_programs(0)
    used = i < sched_ref[nblk]

    @pl.when(j == 0)
    def _():
        xn_ref[...] = _rms(x_ref[...], g_ref[...]).astype(BF16)
        acc_ref[...] = jnp.zeros_like(acc_ref)

    @pl.when(used)
    def _():
        xn = xn_ref[...]
        gate = _dot(xn, wg_ref[0])
        act = gate * (1.0 / (1.0 + jnp.exp(-gate))) * _dot(xn, wu_ref[0])
        acc_ref[...] += _dot(act.astype(BF16), wd_ref[0])

    @pl.when(j == pl.num_programs(1) - 1)
    def _():
        o_ref[...] = acc_ref[...]


def _moe_experts(sched, xg, g, wg, wu, wd):
    cap, D = xg.shape
    F = wg.shape[2]
    R, tf = MOE_ROWS, FF_TILE
    nblk, nf = cap // R, F // tf

    def blk_i(i, s):
        return jnp.minimum(i, s[nblk] - 1)

    def col_j(i, j, s):
        return jnp.where(i < s[nblk], j, nf - 1)

    return pl.pallas_call(
        _moe_kernel,
        grid_spec=pltpu.PrefetchScalarGridSpec(
            num_scalar_prefetch=1,
            grid=(nblk, nf),
            in_specs=[pl.BlockSpec((R, D), lambda i, j, s: (blk_i(i, s), 0)),
                      pl.BlockSpec(g.shape, lambda i, j, s: (0, 0)),
                      pl.BlockSpec((1, D, tf), lambda i, j, s: (s[blk_i(i, s)], 0, col_j(i, j, s))),
                      pl.BlockSpec((1, D, tf), lambda i, j, s: (s[blk_i(i, s)], 0, col_j(i, j, s))),
                      pl.BlockSpec((1, tf, D), lambda i, j, s: (s[blk_i(i, s)], col_j(i, j, s), 0))],
            out_specs=pl.BlockSpec((R, D), lambda i, j, s: (i, 0)),
            scratch_shapes=[pltpu.VMEM((R, D), BF16), pltpu.VMEM((R, D), F32)],
        ),
        out_shape=jax.ShapeDtypeStruct((cap, D), F32),
        compiler_params=_cparams(("arbitrary", "arbitrary")),
        name="moe_experts",
    )(sched, xg, g, wg, wu, wd)


def _combine_kernel(h_ref, y0_ref, y1_ref, route_ref, g_ref, o_ref):
    route = route_ref[...]
    out = h_ref[...] + (y0_ref[...] * route[:, 2:3] + y1_ref[...] * route[:, 3:4])
    o_ref[...] = _rms(out, g_ref[...])


def _combine_norm(h, y0, y1, route, g):
    T, D = h.shape
    tm = ROW_TILE
    row = lambda w: pl.BlockSpec((tm, w), lambda i: (i, 0))
    return pl.pallas_call(
        _combine_kernel,
        grid=(T // tm,),
        in_specs=[row(D), row(D), row(D), row(LANES), pl.BlockSpec(g.shape, lambda i: (0, 0))],
        out_specs=row(D),
        out_shape=jax.ShapeDtypeStruct((T, D), F32),
        compiler_params=_cparams(("parallel",)),
        name="combine_norm",
    )(h, y0, y1, route, g)


def _rope_tables(seq, half):
    inv_freq = ROPE_BASE ** (-jnp.arange(half, dtype=F32) / half)
    ang = jnp.arange(seq).astype(F32)[:, None] * inv_freq[None, :]
    cos, sin = jnp.cos(ang), jnp.sin(ang)
    reps = LANES // (2 * half)
    return jnp.tile(jnp.concatenate([cos, cos], 1), (1, reps)), jnp.tile(jnp.concatenate([-sin, sin], 1), (1, reps))


def _retention_tables():
    log_gamma = jnp.log1p(-(2.0 ** (-5.0 - jnp.arange(RET_HEADS, dtype=F32))))
    idx = jnp.arange(CHUNK, dtype=F32)
    rel = idx[:, None] - idx[None, :]
    din = jnp.where(rel >= 0, jnp.exp(log_gamma[:, None, None] * jnp.maximum(rel, 0.0)), 0.0)
    qd = jnp.exp(log_gamma[:, None] * (idx + 1.0))
    kd = jnp.exp(log_gamma[:, None] * (CHUNK - 1.0 - idx))
    cd = jnp.exp(log_gamma * CHUNK)
    bc = lambda a: jnp.broadcast_to(a[:, :, None], (RET_HEADS, CHUNK, LANES))
    return din, bc(qd), bc(kd), jnp.broadcast_to(cd[:, None, None], (RET_HEADS, 1, LANES))


def kernel(x, ab_norm, ab_w_in, mla_q_norm, mla_w_uq, mla_kv_norm, mla_w_ukv, ret_gn, ab_w_out, ffn_norm,
           ffn_w_gate, ffn_w_up, ffn_w_down, cd_norm, cd_w_in, swa_sinks, cd_w_out, moe_norm, moe_router,
           moe_w_gate, moe_w_up, moe_w_down, final_norm):
    B, S, D = x.shape
    T = B * S
    h = x.reshape(T, D)
    row = lambda g: g.reshape(1, -1)

    w = ab_w_in[0]
    win0 = jnp.concatenate([w[:, :640], jnp.tile(w[:, 640:672], (1, 4)), w[:, 672:]], axis=1).astype(BF16)
    wuq = mla_w_uq[0].reshape(MLA_Q_RANK, MLA_HEADS, MLA_NOPE + MLA_ROPE)
    wuq = jnp.concatenate([wuq[:, :, :MLA_NOPE].reshape(MLA_Q_RANK, -1),
                           wuq[:, :, MLA_NOPE:].reshape(MLA_Q_RANK, -1)], axis=1).astype(BF16)
    wukv = mla_w_ukv[0].reshape(MLA_KV_RANK, MLA_HEADS, MLA_NOPE + MLA_V)
    wukv = jnp.concatenate([wukv[:, :, :MLA_NOPE].reshape(MLA_KV_RANK, -1),
                            wukv[:, :, MLA_NOPE:].reshape(MLA_KV_RANK, -1)], axis=1).astype(BF16)
    cm, sm = _rope_tables(S, MLA_ROPE // 2)
    cr, sr = _rope_tables(S, RET_DK // 2)
    qnope, qrope, knope, krope, v, rq, rk, rv, rg = _proj0(
        h, row(ab_norm[0]), win0, row(mla_q_norm[0]), wuq, row(mla_kv_norm[0]), wukv, cm, sm, cr, sr, S)
    mla = _mla_attention(qnope, qrope, knope, krope, v, B, S)
    ret = _retention(rq, rk, rv, rg, row(ret_gn[0]), *_retention_tables(), B, S)
    wo = ab_w_out[0].astype(BF16)
    nm = MLA_HEADS * MLA_V
    h = _mixer_out_ffn(h, mla, ret, wo[:nm], wo[nm:], row(ffn_norm[0]), ffn_w_gate[0].astype(BF16),
                       ffn_w_up[0].astype(BF16), ffn_w_down[0].astype(BF16))

    w = cd_w_in[0]
    nq = SWA_HEADS * SWA_DIM
    pair_order = np.stack([np.arange(SWA_HEADS // 2), np.arange(SWA_HEADS // 2) + SWA_HEADS // 2], 1).reshape(-1)
    cols = (pair_order[:, None] * SWA_DIM + np.arange(SWA_DIM)[None, :]).reshape(-1)
    win1 = jnp.concatenate([w[:, :nq][:, cols], w[:, nq:]], axis=1).astype(BF16)
    sq, sk, sv, bq, bk, bv = _proj1(h, row(cd_norm[0]), win1)
    slopes = 2.0 ** (-8.0 * jnp.arange(1, SWA_HEADS + 1, dtype=F32) / SWA_HEADS)
    swa = _swa_attention(jnp.concatenate([swa_sinks[0].astype(F32), slopes]), sq, sk, sv, B, S)
    sb = _sb_attention(bq, bk, bv, B, S)
    wo = cd_w_out[0]
    wo_swa = wo[:nq][cols].astype(BF16)
    router = jnp.pad(moe_router[0], ((0, 0), (0, LANES - N_EXPERTS)))
    rhi = router.astype(BF16)
    rlo = (router - rhi.astype(F32)).astype(BF16)
    h, route = _mixer_out_router(h, swa, sb, wo_swa, wo[nq:].astype(BF16), row(moe_norm[0]), rhi, rlo)

    R = MOE_ROWS
    TK = T * TOP_K
    expert = route[:, :TOP_K].astype(jnp.int32).reshape(-1)
    onehot = (expert[:, None] == jnp.arange(N_EXPERTS, dtype=jnp.int32)[None, :]).astype(jnp.int32)
    csum = jnp.cumsum(onehot, axis=0)
    counts = csum[-1]
    rank = jnp.sum((csum - 1) * onehot, axis=1)
    padded = ((counts + R - 1) // R) * R
    pad_end = jnp.cumsum(padded)
    pad_start = pad_end - padded
    dest = pad_start[expert] + rank
    cap = -(-TK // R) * R + N_EXPERTS * R
    nblk = cap // R
    token = jnp.repeat(jnp.arange(T, dtype=jnp.int32), TOP_K)
    buf_tok = jnp.zeros((cap,), jnp.int32).at[dest].set(token)
    blk_exp = jnp.minimum(jnp.searchsorted(pad_end, jnp.arange(nblk, dtype=jnp.int32) * R, side='right'),
                          N_EXPERTS - 1).astype(jnp.int32)
    sched = jnp.concatenate([blk_exp, (pad_end[-1:] // R).astype(jnp.int32)])
    yg = _moe_experts(sched, h[buf_tok], row(moe_norm[0]), moe_w_gate[0].astype(BF16), moe_w_up[0].astype(BF16),
                      moe_w_down[0].astype(BF16))
    pos = dest.reshape(T, TOP_K)
    out = _combine_norm(h, yg[pos[:, 0]], yg[pos[:, 1]], route, row(final_norm))
    return out.reshape(B, S, D)
```

```python
import functools

import jax
import jax.numpy as jnp
import numpy as np
from jax import lax
from jax.experimental import pallas as pl
from jax.experimental.pallas import tpu as pltpu

F32 = jnp.float32
BF16 = jnp.bfloat16

LANES = 128
EPS = 1e-6
LOG2E = 1.4426950408889634
ROPE_BASE = 10000.0
CHUNK = 128
MLA_HEADS, MLA_NOPE, MLA_ROPE, MLA_V = 8, 64, 32, 64
MLA_Q_RANK, MLA_KV_RANK = 384, 256
RET_HEADS, RET_DK, RET_DV = 8, 64, 128
SWA_HEADS, SWA_KV_HEADS, SWA_DIM, WINDOW = 16, 2, 64, 128
SB_HEADS, SB_DIM = 8, 64
N_EXPERTS, TOP_K = 8, 2

VMEM_LIMIT = 56 * 1024 * 1024
ROW_TILE = 512
FF_TILE = 512
MOE_ROWS = 512
MLA_TQ = 512
SB_T = 256


def _cparams(sem):
    return pltpu.CompilerParams(dimension_semantics=sem, vmem_limit_bytes=VMEM_LIMIT)


def _rms(x, g):
    return x * lax.rsqrt(jnp.mean(x * x, axis=-1, keepdims=True) + EPS) * g


def _rope_slab(x, cos, sin_signed, half):
    lane = lax.broadcasted_iota(jnp.int32, x.shape, 1)
    first = (lane % (2 * half)) < half
    partner = jnp.where(first, pltpu.roll(x, LANES - half, 1), pltpu.roll(x, half, 1))
    return x * cos + partner * sin_signed


def _nt_dot(a, b):
    return lax.dot_general(a, b, (((1,), (1,)), ((), ())), preferred_element_type=F32)


def _dot(a, b):
    return jnp.dot(a, b, preferred_element_type=F32)


def _proj0_kernel(h_ref, g_ref, win_ref, qn_ref, wuq_ref, kvn_ref, wukv_ref, cm_ref, sm_ref, cr_ref, sr_ref,
                  qnope_ref, qrope_ref, knope_ref, krope_ref, v_ref, rq_ref, rk_ref, rv_ref, rg_ref):
    xn = _rms(h_ref[...], g_ref[...]).astype(BF16)

    def mm(lo, hi):
        return _dot(xn, win_ref[:, lo:hi])

    cm, sm, cr, sr = cm_ref[...], sm_ref[...], cr_ref[...], sr_ref[...]
    q_scale = (MLA_NOPE + MLA_ROPE) ** -0.5
    qh = _dot(_rms(mm(0, 384), qn_ref[...]).astype(BF16), wuq_ref[...])
    qnope_ref[...] = (qh[:, :512] * q_scale).astype(BF16)
    for s in range(2):
        slab = qh[:, 512 + LANES * s:512 + LANES * (s + 1)]
        qrope_ref[:, LANES * s:LANES * (s + 1)] = (_rope_slab(slab, cm, sm, MLA_ROPE // 2) * q_scale).astype(BF16)
    kvh = _dot(_rms(mm(384, 640), kvn_ref[...]).astype(BF16), wukv_ref[...])
    knope_ref[...] = kvh[:, :512].astype(BF16)
    v_ref[...] = kvh[:, 512:].astype(BF16)
    krope_ref[...] = _rope_slab(mm(640, 768), cm, sm, MLA_ROPE // 2).astype(BF16)
    rq = mm(768, 1280)
    rk = mm(1280, 1792)
    for s in range(4):
        sl = slice(LANES * s, LANES * (s + 1))
        rq_ref[:, sl] = _rope_slab(rq[:, sl], cr, sr, RET_DK // 2).astype(BF16)
        rk_ref[:, sl] = (_rope_slab(rk[:, sl], cr, sr, RET_DK // 2) * (RET_DK ** -0.5)).astype(BF16)
    rv_ref[...] = mm(1792, 2816).astype(BF16)
    rg = mm(2816, 3840)
    rg_ref[...] = (rg * (1.0 / (1.0 + jnp.exp(-rg)))).astype(BF16)


def _proj0(h, g, win, qn, wuq, kvn, wukv, cm, sm, cr, sr, seq):
    T, D = h.shape
    tm = ROW_TILE
    nseq = seq // tm
    row = lambda w: pl.BlockSpec((tm, w), lambda i: (i, 0))
    full = lambda a: pl.BlockSpec(a.shape, lambda i: (0, 0))
    pos = pl.BlockSpec((tm, LANES), lambda i: (i % nseq, 0))
    widths = (512, 256, 512, 128, 512, 512, 512, 1024, 1024)
    return pl.pallas_call(
        _proj0_kernel,
        grid=(T // tm,),
        in_specs=[row(D), full(g), full(win), full(qn), full(wuq), full(kvn), full(wukv), pos, pos, pos, pos],
        out_specs=[row(w) for w in widths],
        out_shape=[jax.ShapeDtypeStruct((T, w), BF16) for w in widths],
        compiler_params=_cparams(("parallel",)),
        name="proj0",
    )(h, g, win, qn, wuq, kvn, wukv, cm, sm, cr, sr)


def _mla_kernel(qn_ref, qr_ref, kn_ref, kr_ref, v_ref, o_ref, kk_ref, *, tq):
    S = qn_ref.shape[0]
    p = pl.program_id(1)
    kk_ref[:, :LANES] = kn_ref[...]
    kk_ref[:, LANES:] = kr_ref[...]
    lane = lax.broadcasted_iota(jnp.int32, (1, LANES), 1)
    quarter = (p % 2) * 2
    row = lax.broadcasted_iota(jnp.int32, (tq, tq), 0)
    col = lax.broadcasted_iota(jnp.int32, (tq, tq), 1)
    causal = col <= row
    for i in range(S // tq):
        lo, hi = i * tq, (i + 1) * tq
        qn = qn_ref[lo:hi, :]
        qr = qr_ref[lo:hi, :]
        outs = []
        for hh in range(2):
            mn = (lane // 64 == hh).astype(BF16)
            mr = (lane // 32 == quarter + hh).astype(BF16)
            qa = jnp.concatenate([qn * mn, qr * mr], axis=1)
            sd = jnp.where(causal, _nt_dot(qa, kk_ref[lo:hi, :]), -jnp.inf)
            m = jnp.max(sd, axis=-1, keepdims=True)
            if i > 0:
                so = _nt_dot(qa, kk_ref[:lo, :])
                m = jnp.maximum(m, jnp.max(so, axis=-1, keepdims=True))
            pd = jnp.exp(sd - m)
            l = jnp.sum(pd, axis=-1, keepdims=True)
            acc = _dot(pd.astype(BF16), v_ref[lo:hi, :])
            if i > 0:
                po = jnp.exp(so - m)
                l = l + jnp.sum(po, axis=-1, keepdims=True)
                acc = acc + _dot(po.astype(BF16), v_ref[:lo, :])
            outs.append(acc / l)
        o_ref[lo:hi, :] = jnp.where(lane < 64, outs[0], outs[1]).astype(BF16)


def _mla_attention(qnope, qrope, knope, krope, v, batch, seq):
    T = qnope.shape[0]
    blk = lambda f: pl.BlockSpec((seq, LANES), f)
    return pl.pallas_call(
        functools.partial(_mla_kernel, tq=MLA_TQ),
        grid=(batch, MLA_HEADS // 2),
        in_specs=[blk(lambda b, p: (b, p)), blk(lambda b, p: (b, p // 2)), blk(lambda b, p: (b, p)),
                  blk(lambda b, p: (b, 0)), blk(lambda b, p: (b, p))],
        out_specs=blk(lambda b, p: (b, p)),
        out_shape=jax.ShapeDtypeStruct((T, MLA_HEADS * MLA_V), BF16),
        scratch_shapes=[pltpu.VMEM((seq, 2 * LANES), BF16)],
        compiler_params=_cparams(("parallel", "parallel")),
        name="mla_attention",
    )(qnope, qrope, knope, krope, v)


def _ret_kernel(q_ref, k_ref, v_ref, g_ref, gn_ref, din_ref, qd_ref, kd_ref, cd_ref, o_ref):
    S = q_ref.shape[0]
    h = pl.program_id(1)
    lane = lax.broadcasted_iota(jnp.int32, (1, LANES), 1)
    qmask = (lane // RET_DK == h % 2).astype(F32)
    din = din_ref[0]
    qd = qd_ref[0] * qmask
    kd = kd_ref[0]
    cd = cd_ref[0]
    gain = gn_ref[...]
    rows = [slice(c * CHUNK, (c + 1) * CHUNK) for c in range(S // CHUNK)]
    kvs = [_dot((k_ref[r, :].astype(F32) * kd).T.astype(BF16), v_ref[r, :]) for r in rows[:-1]]
    states = [None]
    for kv in kvs:
        states.append(kv if states[-1] is None else states[-1] * cd + kv)
    for r, state in zip(rows, states):
        q = q_ref[r, :].astype(F32)
        inner = _nt_dot((q * qmask).astype(BF16), k_ref[r, :]) * din
        y = _dot(inner.astype(BF16), v_ref[r, :])
        if state is not None:
            y = y + _dot((q * qd).astype(BF16), state.astype(BF16))
        mu = jnp.mean(y, axis=-1, keepdims=True)
        var = jnp.mean(jnp.square(y - mu), axis=-1, keepdims=True)
        yn = (y - mu) * lax.rsqrt(var + EPS) * gain
        o_ref[r, :] = (g_ref[r, :].astype(F32) * yn).astype(BF16)


def _retention(rq, rk, rv, rg, gn, din, qd, kd, cd, batch, seq):
    T = rq.shape[0]
    blk = lambda f: pl.BlockSpec((seq, LANES), f)
    tab = pl.BlockSpec((1, CHUNK, LANES), lambda b, h: (h, 0, 0))
    return pl.pallas_call(
        _ret_kernel,
        grid=(batch, RET_HEADS),
        in_specs=[blk(lambda b, h: (b, h // 2)), blk(lambda b, h: (b, h // 2)), blk(lambda b, h: (b, h)),
                  blk(lambda b, h: (b, h)), pl.BlockSpec((1, LANES), lambda b, h: (0, h)), tab, tab, tab,
                  pl.BlockSpec((1, 1, LANES), lambda b, h: (h, 0, 0))],
        out_specs=blk(lambda b, h: (b, h)),
        out_shape=jax.ShapeDtypeStruct((T, RET_HEADS * RET_DV), BF16),
        compiler_params=_cparams(("parallel", "parallel")),
        name="retention",
    )(rq, rk, rv, rg, gn, din, qd, kd, cd)


def _ffn_kernel(h_ref, a_ref, b_ref, wa_ref, wb_ref, g_ref, wg_ref, wu_ref, wd_ref, o_ref, hn_ref, xn_ref, acc_ref):
    j = pl.program_id(1)

    @pl.when(j == 0)
    def _():
        hn = h_ref[...] + _dot(a_ref[...], wa_ref[...]) + _dot(b_ref[...], wb_ref[...])
        hn_ref[...] = hn
        xn_ref[...] = _rms(hn, g_ref[...]).astype(BF16)
        acc_ref[...] = jnp.zeros_like(acc_ref)

    xn = xn_ref[...]
    gate = _dot(xn, wg_ref[...])
    act = gate * (1.0 / (1.0 + jnp.exp(-gate))) * _dot(xn, wu_ref[...])
    acc_ref[...] += _dot(act.astype(BF16), wd_ref[...])

    @pl.when(j == pl.num_programs(1) - 1)
    def _():
        o_ref[...] = hn_ref[...] + acc_ref[...]


def _mixer_out_ffn(h, a, b, wa, wb, g, wg, wu, wd):
    T, D = h.shape
    F = wg.shape[1]
    tm, tf = ROW_TILE, FF_TILE
    row = lambda w: pl.BlockSpec((tm, w), lambda i, j: (i, 0))
    full = lambda x: pl.BlockSpec(x.shape, lambda i, j: (0, 0))
    return pl.pallas_call(
        _ffn_kernel,
        grid=(T // tm, F // tf),
        in_specs=[row(D), row(a.shape[1]), row(b.shape[1]), full(wa), full(wb), full(g),
                  pl.BlockSpec((D, tf), lambda i, j: (0, j)), pl.BlockSpec((D, tf), lambda i, j: (0, j)),
                  pl.BlockSpec((tf, D), lambda i, j: (j, 0))],
        out_specs=row(D),
        out_shape=jax.ShapeDtypeStruct((T, D), F32),
        scratch_shapes=[pltpu.VMEM((tm, D), F32), pltpu.VMEM((tm, D), BF16), pltpu.VMEM((tm, D), F32)],
        compiler_params=_cparams(("parallel", "arbitrary")),
        name="mixer_out_ffn",
    )(h, a, b, wa, wb, g, wg, wu, wd)


def _proj1_kernel(h_ref, g_ref, win_ref, sq_ref, sk_ref, sv_ref, bq_ref, bk_ref, bv_ref):
    xn = _rms(h_ref[...], g_ref[...]).astype(BF16)

    def mm(lo, hi):
        return _dot(xn, win_ref[:, lo:hi])

    sq_ref[...] = (mm(0, 1024) * (SWA_DIM ** -0.5)).astype(BF16)
    sk_ref[...] = mm(1024, 1152).astype(BF16)
    sv_ref[...] = mm(1152, 1280).astype(BF16)
    bq_ref[...] = (mm(1280, 1792) * (SB_DIM ** -0.5)).astype(BF16)
    bk_ref[...] = mm(1792, 2304).astype(BF16)
    bv_ref[...] = mm(2304, 2816).astype(BF16)


def _proj1(h, g, win):
    T, D = h.shape
    tm = ROW_TILE
    row = lambda w: pl.BlockSpec((tm, w), lambda i: (i, 0))
    full = lambda a: pl.BlockSpec(a.shape, lambda i: (0, 0))
    widths = (1024, 128, 128, 512, 512, 512)
    return pl.pallas_call(
        _proj1_kernel,
        grid=(T // tm,),
        in_specs=[row(D), full(g), full(win)],
        out_specs=[row(w) for w in widths],
        out_shape=[jax.ShapeDtypeStruct((T, w), BF16) for w in widths],
        compiler_params=_cparams(("parallel",)),
        name="proj1",
    )(h, g, win)


def _swa_kernel(tab_ref, q_ref, k_ref, v_ref, o_ref):
    S = q_ref.shape[0]
    W = WINDOW
    j = pl.program_id(1)
    lane = lax.broadcasted_iota(jnp.int32, (1, LANES), 1)
    r2 = lax.broadcasted_iota(jnp.int32, (W, 2 * W), 0)
    c2 = lax.broadcasted_iota(jnp.int32, (W, 2 * W), 1)
    dist2 = r2 + W - c2
    r1 = lax.broadcasted_iota(jnp.int32, (W, W), 0)
    c1 = lax.broadcasted_iota(jnp.int32, (W, W), 1)
    dist1 = r1 - c1
    for i in range(S // W):
        q = q_ref[i * W:(i + 1) * W, :]
        if i == 0:
            k, v, dist = k_ref[0:W, :], v_ref[0:W, :], dist1
        else:
            k, v, dist = k_ref[(i - 1) * W:(i + 1) * W, :], v_ref[(i - 1) * W:(i + 1) * W, :], dist2
        valid = (dist >= 0) & (dist < W)
        distf = dist.astype(F32)
        outs = []
        for hh in range(2):
            head = j + (SWA_HEADS // 2) * hh
            sink = tab_ref[head]
            slope = tab_ref[SWA_HEADS + head]
            s = _nt_dot(q * (lane // 64 == hh).astype(BF16), k) - slope * distf
            s = jnp.where(valid, s, -jnp.inf)
            m = jnp.maximum(jnp.max(s, axis=-1, keepdims=True), sink)
            p = jnp.exp(s - m)
            den = jnp.sum(p, axis=-1, keepdims=True) + jnp.exp(sink - m)
            outs.append(_dot(p.astype(BF16), v) / den)
        o_ref[i * W:(i + 1) * W, :] = jnp.where(lane < 64, outs[0], outs[1]).astype(BF16)


def _swa_attention(tab, sq, sk, sv, batch, seq):
    T = sq.shape[0]
    blk = lambda f: pl.BlockSpec((seq, LANES), f)
    return pl.pallas_call(
        _swa_kernel,
        grid_spec=pltpu.PrefetchScalarGridSpec(
            num_scalar_prefetch=1,
            grid=(batch, SWA_HEADS // 2),
            in_specs=[blk(lambda b, j, t: (b, j)), blk(lambda b, j, t: (b, 0)), blk(lambda b, j, t: (b, 0))],
            out_specs=blk(lambda b, j, t: (b, j)),
        ),
        out_shape=jax.ShapeDtypeStruct((T, SWA_HEADS * SWA_DIM), BF16),
        compiler_params=_cparams(("parallel", "parallel")),
        name="swa_attention",
    )(tab, sq, sk, sv)


def _sb_kernel(q_ref, k_ref, v_ref, o_ref, *, t):
    S = q_ref.shape[0]
    lane = lax.broadcasted_iota(jnp.int32, (1, LANES), 1)
    row = lax.broadcasted_iota(jnp.int32, (t, t), 0)
    col = lax.broadcasted_iota(jnp.int32, (t, t), 1)
    before = col < row
    neg_later = jnp.where(row > col, -1.0, 0.0).astype(BF16)

    def softplus(z):
        return jnp.maximum(z, 0.0) + jnp.log(1.0 + jnp.exp2(jnp.abs(z) * (-LOG2E)))

    for i in range(S // t):
        rows = slice(i * t, (i + 1) * t)
        q = q_ref[rows, :]
        outs = []
        for hh in range(2):
            qm = q * (lane // 64 == hh).astype(BF16)
            z = _nt_dot(qm, k_ref[rows, :])
            sp = jnp.where(before, softplus(z), 0.0)
            a = jnp.where(before, jnp.exp((z - sp) + _dot(sp.astype(BF16), neg_later)), 0.0)
            acc = _dot(a.astype(BF16), v_ref[rows, :])
            run = -jnp.sum(sp, axis=-1, keepdims=True)
            for j in range(i - 1, -1, -1):
                krows = slice(j * t, (j + 1) * t)
                z = _nt_dot(qm, k_ref[krows, :])
                sp = softplus(z)
                a = jnp.exp((z - sp) + (_dot(sp.astype(BF16), neg_later) + run))
                acc = acc + _dot(a.astype(BF16), v_ref[krows, :])
                run = run - jnp.sum(sp, axis=-1, keepdims=True)
            outs.append(acc)
        o_ref[rows, :] = jnp.where(lane < 64, outs[0], outs[1]).astype(BF16)


def _sb_attention(bq, bk, bv, batch, seq):
    T = bq.shape[0]
    blk = pl.BlockSpec((seq, LANES), lambda b, p: (b, p))
    return pl.pallas_call(
        functools.partial(_sb_kernel, t=SB_T),
        grid=(batch, SB_HEADS // 2),
        in_specs=[blk, blk, blk],
        out_specs=blk,
        out_shape=jax.ShapeDtypeStruct((T, SB_HEADS * SB_DIM), BF16),
        compiler_params=_cparams(("parallel", "parallel")),
        name="sb_attention",
    )(bq, bk, bv)


def _split_bf16(x):
    hi = x.astype(BF16)
    return hi, (x - hi.astype(F32)).astype(BF16)


def _out_router_kernel(h_ref, a_ref, b_ref, wa_ref, wb_ref, g_ref, rhi_ref, rlo_ref, o_ref, route_ref):
    hn = h_ref[...] + _dot(a_ref[...], wa_ref[...]) + _dot(b_ref[...], wb_ref[...])
    o_ref[...] = hn
    xhi, xlo = _split_bf16(_rms(hn, g_ref[...]))
    logits = _dot(xhi, rhi_ref[...]) + (_dot(xhi, rlo_ref[...]) + _dot(xlo, rhi_ref[...]))
    lane = lax.broadcasted_iota(jnp.int32, logits.shape, 1)
    logits = jnp.where(lane < N_EXPERTS, logits, -jnp.inf)
    m1 = jnp.max(logits, axis=-1, keepdims=True)
    i1 = jnp.min(jnp.where(logits == m1, lane, LANES), axis=-1, keepdims=True)
    rest = jnp.where(lane == i1, -jnp.inf, logits)
    m2 = jnp.max(rest, axis=-1, keepdims=True)
    i2 = jnp.min(jnp.where(rest == m2, lane, LANES), axis=-1, keepdims=True)
    e2 = jnp.exp(m2 - m1)
    den = 1.0 + e2
    route = jnp.where(lane == 0, i1.astype(F32), 0.0)
    route = jnp.where(lane == 1, i2.astype(F32), route)
    route = jnp.where(lane == 2, 1.0 / den, route)
    route = jnp.where(lane == 3, e2 / den, route)
    route_ref[...] = route


def _mixer_out_router(h, a, b, wa, wb, g, rhi, rlo):
    T, D = h.shape
    tm = ROW_TILE
    row = lambda w: pl.BlockSpec((tm, w), lambda i: (i, 0))
    full = lambda x: pl.BlockSpec(x.shape, lambda i: (0, 0))
    return pl.pallas_call(
        _out_router_kernel,
        grid=(T // tm,),
        in_specs=[row(D), row(a.shape[1]), row(b.shape[1]), full(wa), full(wb), full(g), full(rhi), full(rlo)],
        out_specs=[row(D), row(LANES)],
        out_shape=[jax.ShapeDtypeStruct((T, D), F32), jax.ShapeDtypeStruct((T, LANES), F32)],
        compiler_params=_cparams(("parallel",)),
        name="mixer_out_router",
    )(h, a, b, wa, wb, g, rhi, rlo)


def _moe_kernel(sched_ref, x_ref, g_ref, wg_ref, wu_ref, wd_ref, o_ref, xn_ref, acc_ref):
    i, j = pl.program_id(0), pl.program_id(1)
    nblk = pl.num_programs(0)
    used = i < sched_ref[nblk]

    @pl.when(j == 0)
    def _():
        xn_ref[...] = _rms(x_ref[...], g_ref[...]).astype(BF16)
        acc_ref[...] = jnp.zeros_like(acc_ref)

    @pl.when(used)
    def _():
        xn = xn_ref[...]
        gate = _dot(xn, wg_ref[0])
        act = gate * (1.0 / (1.0 + jnp.exp(-gate))) * _dot(xn, wu_ref[0])
        acc_ref[...] += _dot(act.astype(BF16), wd_ref[0])

    @pl.when(j == pl.num_programs(1) - 1)
    def _():
        o_ref[...] = acc_ref[...]


def _moe_experts(sched, xg, g, wg, wu, wd):
    cap, D = xg.shape
    F = wg.shape[2]
    R, tf = MOE_ROWS, FF_TILE
    nblk, nf = cap // R, F // tf

    def blk_i(i, s):
        return jnp.minimum(i, s[nblk] - 1)

    def col_j(i, j, s):
        return jnp.where(i < s[nblk], j, nf - 1)

    return pl.pallas_call(
        _moe_kernel,
        grid_spec=pltpu.PrefetchScalarGridSpec(
            num_scalar_prefetch=1,
            grid=(nblk, nf),
            in_specs=[pl.BlockSpec((R, D), lambda i, j, s: (blk_i(i, s), 0)),
                      pl.BlockSpec(g.shape, lambda i, j, s: (0, 0)),
                      pl.BlockSpec((1, D, tf), lambda i, j, s: (s[blk_i(i, s)], 0, col_j(i, j, s))),
                      pl.BlockSpec((1, D, tf), lambda i, j, s: (s[blk_i(i, s)], 0, col_j(i, j, s))),
                      pl.BlockSpec((1, tf, D), lambda i, j, s: (s[blk_i(i, s)], col_j(i, j, s), 0))],
            out_specs=pl.BlockSpec((R, D), lambda i, j, s: (i, 0)),
            scratch_shapes=[pltpu.VMEM((R, D), BF16), pltpu.VMEM((R, D), F32)],
        ),
        out_shape=jax.ShapeDtypeStruct((cap, D), F32),
        compiler_params=_cparams(("arbitrary", "arbitrary")),
        name="moe_experts",
    )(sched, xg, g, wg, wu, wd)


def _combine_kernel(h_ref, y0_ref, y1_ref, route_ref, g_ref, o_ref):
    route = route_ref[...]
    out = h_ref[...] + (y0_ref[...] * route[:, 2:3] + y1_ref[...] * route[:, 3:4])
    o_ref[...] = _rms(out, g_ref[...])


def _combine_norm(h, y0, y1, route, g):
    T, D = h.shape
    tm = ROW_TILE
    row = lambda w: pl.BlockSpec((tm, w), lambda i: (i, 0))
    return pl.pallas_call(
        _combine_kernel,
        grid=(T // tm,),
        in_specs=[row(D), row(D), row(D), row(LANES), pl.BlockSpec(g.shape, lambda i: (0, 0))],
        out_specs=row(D),
        out_shape=jax.ShapeDtypeStruct((T, D), F32),
        compiler_params=_cparams(("parallel",)),
        name="combine_norm",
    )(h, y0, y1, route, g)


def _rope_tables(seq, half):
    inv_freq = ROPE_BASE ** (-jnp.arange(half, dtype=F32) / half)
    ang = jnp.arange(seq).astype(F32)[:, None] * inv_freq[None, :]
    cos, sin = jnp.cos(ang), jnp.sin(ang)
    reps = LANES // (2 * half)
    return jnp.tile(jnp.concatenate([cos, cos], 1), (1, reps)), jnp.tile(jnp.concatenate([-sin, sin], 1), (1, reps))


def _retention_tables():
    log_gamma = jnp.log1p(-(2.0 ** (-5.0 - jnp.arange(RET_HEADS, dtype=F32))))
    idx = jnp.arange(CHUNK, dtype=F32)
    rel = idx[:, None] - idx[None, :]
    din = jnp.where(rel >= 0, jnp.exp(log_gamma[:, None, None] * jnp.maximum(rel, 0.0)), 0.0)
    qd = jnp.exp(log_gamma[:, None] * (idx + 1.0))
    kd = jnp.exp(log_gamma[:, None] * (CHUNK - 1.0 - idx))
    cd = jnp.exp(log_gamma * CHUNK)
    bc = lambda a: jnp.broadcast_to(a[:, :, None], (RET_HEADS, CHUNK, LANES))
    return din, bc(qd), bc(kd), jnp.broadcast_to(cd[:, None, None], (RET_HEADS, 1, LANES))


def kernel(x, ab_norm, ab_w_in, mla_q_norm, mla_w_uq, mla_kv_norm, mla_w_ukv, ret_gn, ab_w_out, ffn_norm,
           ffn_w_gate, ffn_w_up, ffn_w_down, cd_norm, cd_w_in, swa_sinks, cd_w_out, moe_norm, moe_router,
           moe_w_gate, moe_w_up, moe_w_down, final_norm):
    B, S, D = x.shape
    T = B * S
    h = x.reshape(T, D)
    row = lambda g: g.reshape(1, -1)

    w = ab_w_in[0]
    win0 = jnp.concatenate([w[:, :640], jnp.tile(w[:, 640:672], (1, 4)), w[:, 672:]], axis=1).astype(BF16)
    wuq = mla_w_uq[0].reshape(MLA_Q_RANK, MLA_HEADS, MLA_NOPE + MLA_ROPE)
    wuq = jnp.concatenate([wuq[:, :, :MLA_NOPE].reshape(MLA_Q_RANK, -1),
                           wuq[:, :, MLA_NOPE:].reshape(MLA_Q_RANK, -1)], axis=1).astype(BF16)
    wukv = mla_w_ukv[0].reshape(MLA_KV_RANK, MLA_HEADS, MLA_NOPE + MLA_V)
    wukv = jnp.concatenate([wukv[:, :, :MLA_NOPE].reshape(MLA_KV_RANK, -1),
                            wukv[:, :, MLA_NOPE:].reshape(MLA_KV_RANK, -1)], axis=1).astype(BF16)
    cm, sm = _rope_tables(S, MLA_ROPE // 2)
    cr, sr = _rope_tables(S, RET_DK // 2)
    qnope, qrope, knope, krope, v, rq, rk, rv, rg = _proj0(
        h, row(ab_norm[0]), win0, row(mla_q_norm[0]), wuq, row(mla_kv_norm[0]), wukv, cm, sm, cr, sr, S)
    mla = _mla_attention(qnope, qrope, knope, krope, v, B, S)
    ret = _retention(rq, rk, rv, rg, row(ret_gn[0]), *_retention_tables(), B, S)
    wo = ab_w_out[0].astype(BF16)
    nm = MLA_HEADS * MLA_V
    h = _mixer_out_ffn(h, mla, ret, wo[:nm], wo[nm:], row(ffn_norm[0]), ffn_w_gate[0].astype(BF16),
                       ffn_w_up[0].astype(BF16), ffn_w_down[0].astype(BF16))

    w = cd_w_in[0]
    nq = SWA_HEADS * SWA_DIM
    pair_order = np.stack([np.arange(SWA_HEADS // 2), np.arange(SWA_HEADS // 2) + SWA_HEADS // 2], 1).reshape(-1)
    cols = (pair_order[:, None] * SWA_DIM + np.arange(SWA_DIM)[None, :]).reshape(-1)
    win1 = jnp.concatenate([w[:, :nq][:, cols], w[:, nq:]], axis=1).astype(BF16)
    sq, sk, sv, bq, bk, bv = _proj1(h, row(cd_norm[0]), win1)
    slopes = 2.0 ** (-8.0 * jnp.arange(1, SWA_HEADS + 1, dtype=F32) / SWA_HEADS)
    swa = _swa_attention(jnp.concatenate([swa_sinks[0].astype(F32), slopes]), sq, sk, sv, B, S)
    sb = _sb_attention(bq, bk, bv, B, S)
    wo = cd_w_out[0]
    wo_swa = wo[:nq][cols].astype(BF16)
    router = jnp.pad(moe_router[0], ((0, 0), (0, LANES - N_EXPERTS)))
    rhi = router.astype(BF16)
    rlo = (router - rhi.astype(F32)).astype(BF16)
    h, route = _mixer_out_router(h, swa, sb, wo_swa, wo[nq:].astype(BF16), row(moe_norm[0]), rhi, rlo)

    R = MOE_ROWS
    TK = T * TOP_K
    expert = route[:, :TOP_K].astype(jnp.int32).reshape(-1)
    onehot = (expert[:, None] == jnp.arange(N_EXPERTS, dtype=jnp.int32)[None, :]).astype(jnp.int32)
    csum = jnp.cumsum(onehot, axis=0)
    counts = csum[-1]
    rank = jnp.sum((csum - 1) * onehot, axis=1)
    padded = ((counts + R - 1) // R) * R
    pad_end = jnp.cumsum(padded)
    pad_start = pad_end - padded
    dest = pad_start[expert] + rank
    cap = -(-TK // R) * R + N_EXPERTS * R
    nblk = cap // R
    token = jnp.repeat(jnp.arange(T, dtype=jnp.int32), TOP_K)
    buf_tok = jnp.zeros((cap,), jnp.int32).at[dest].set(token)
    blk_exp = jnp.minimum(jnp.searchsorted(pad_end, jnp.arange(nblk, dtype=jnp.int32) * R, side='right'),
                          N_EXPERTS - 1).astype(jnp.int32)
    sched = jnp.concatenate([blk_exp, (pad_end[-1:] // R).astype(jnp.int32)])
    yg = _moe_experts(sched, h[buf_tok], row(moe_norm[0]), moe_w_gate[0].astype(BF16), moe_w_up[0].astype(BF16),
                      moe_w_down[0].astype(BF16))
    pos = dest.reshape(T, TOP_K)
    out = _combine_norm(h, yg[pos[:, 0]], yg[pos[:, 1]], route, row(final_norm))
    return out.reshape(B, S, D)
```

```python
import functools

import jax
import jax.numpy as jnp
import numpy as np
from jax import lax
from jax.experimental import pallas as pl
from jax.experimental.pallas import tpu as pltpu

F32 = jnp.float32
BF16 = jnp.bfloat16

LANES = 128
EPS = 1e-6
LOG2E = 1.4426950408889634
ROPE_BASE = 10000.0
CHUNK = 128
MLA_HEADS, MLA_NOPE, MLA_ROPE, MLA_V = 8, 64, 32, 64
MLA_Q_RANK, MLA_KV_RANK = 384, 256
RET_HEADS, RET_DK, RET_DV = 8, 64, 128
SWA_HEADS, SWA_KV_HEADS, SWA_DIM, WINDOW = 16, 2, 64, 128
SB_HEADS, SB_DIM = 8, 64
N_EXPERTS, TOP_K = 8, 2

VMEM_LIMIT = 56 * 1024 * 1024
ROW_TILE = 512
FF_STEPS = 2
MOE_FF_STEPS = 2
CAST_ROWS = 512
COMBINE_ROWS = 256
MOE_ROWS = 512
MLA_TQ = 512
SB_T = 256


def _cparams(sem):
    return pltpu.CompilerParams(dimension_semantics=sem, vmem_limit_bytes=VMEM_LIMIT)


def _rms(x, g):
    return x * lax.rsqrt(jnp.mean(x * x, axis=-1, keepdims=True) + EPS) * g


def _rope_slab(x, cos, sin_signed, half):
    lane = lax.broadcasted_iota(jnp.int32, x.shape, 1)
    first = (lane % (2 * half)) < half
    partner = jnp.where(first, pltpu.roll(x, LANES - half, 1), pltpu.roll(x, half, 1))
    return x * cos + partner * sin_signed


def _nt_dot(a, b):
    return lax.dot_general(a, b, (((1,), (1,)), ((), ())), preferred_element_type=F32)


def _dot(a, b):
    return jnp.dot(a, b, preferred_element_type=F32)


def _proj0_kernel(h_ref, g_ref, win_ref, qn_ref, wuq_ref, kvn_ref, wukv_ref, cm_ref, sm_ref, cr_ref, sr_ref,
                  qnope_ref, qrope_ref, knope_ref, krope_ref, v_ref, rq_ref, rk_ref, rv_ref, rg_ref):
    xn = _rms(h_ref[...], g_ref[...]).astype(BF16)

    def mm(lo, hi):
        return _dot(xn, win_ref[:, lo:hi])

    cm, sm, cr, sr = cm_ref[...], sm_ref[...], cr_ref[...], sr_ref[...]
    q_scale = (MLA_NOPE + MLA_ROPE) ** -0.5
    qh = _dot(_rms(mm(0, 384), qn_ref[...]).astype(BF16), wuq_ref[...])
    qnope_ref[...] = (qh[:, :512] * q_scale).astype(BF16)
    for s in range(2):
        slab = qh[:, 512 + LANES * s:512 + LANES * (s + 1)]
        qrope_ref[:, LANES * s:LANES * (s + 1)] = (_rope_slab(slab, cm, sm, MLA_ROPE // 2) * q_scale).astype(BF16)
    kvh = _dot(_rms(mm(384, 640), kvn_ref[...]).astype(BF16), wukv_ref[...])
    knope_ref[...] = kvh[:, :512].astype(BF16)
    v_ref[...] = kvh[:, 512:].astype(BF16)
    krope_ref[...] = _rope_slab(mm(640, 768), cm, sm, MLA_ROPE // 2).astype(BF16)
    rq = mm(768, 1280)
    rk = mm(1280, 1792)
    for s in range(4):
        sl = slice(LANES * s, LANES * (s + 1))
        rq_ref[:, sl] = _rope_slab(rq[:, sl], cr, sr, RET_DK // 2).astype(BF16)
        rk_ref[:, sl] = (_rope_slab(rk[:, sl], cr, sr, RET_DK // 2) * (RET_DK ** -0.5)).astype(BF16)
    rv_ref[...] = mm(1792, 2816).astype(BF16)
    rg = mm(2816, 3840)
    rg_ref[...] = (rg * (1.0 / (1.0 + jnp.exp(-rg)))).astype(BF16)


def _proj0(h, g, win, qn, wuq, kvn, wukv, cm, sm, cr, sr, seq):
    T, D = h.shape
    tm = ROW_TILE
    nseq = seq // tm
    row = lambda w: pl.BlockSpec((tm, w), lambda i: (i, 0))
    full = lambda a: pl.BlockSpec(a.shape, lambda i: (0, 0))
    pos = pl.BlockSpec((tm, LANES), lambda i: (i % nseq, 0))
    widths = (512, 256, 512, 128, 512, 512, 512, 1024, 1024)
    return pl.pallas_call(
        _proj0_kernel,
        grid=(T // tm,),
        in_specs=[row(D), full(g), full(win), full(qn), full(wuq), full(kvn), full(wukv), pos, pos, pos, pos],
        out_specs=[row(w) for w in widths],
        out_shape=[jax.ShapeDtypeStruct((T, w), BF16) for w in widths],
        compiler_params=_cparams(("parallel",)),
        name="proj0",
    )(h, g, win, qn, wuq, kvn, wukv, cm, sm, cr, sr)


def _mla_kernel(qn_ref, qr_ref, kn_ref, kr_ref, v_ref, o_ref, kk_ref, *, tq):
    S = qn_ref.shape[0]
    p = pl.program_id(1)
    kk_ref[:, :LANES] = kn_ref[...]
    kk_ref[:, LANES:] = kr_ref[...]
    lane = lax.broadcasted_iota(jnp.int32, (1, LANES), 1)
    quarter = (p % 2) * 2
    row = lax.broadcasted_iota(jnp.int32, (tq, tq), 0)
    col = lax.broadcasted_iota(jnp.int32, (tq, tq), 1)
    causal = col <= row
    for i in range(S // tq):
        lo, hi = i * tq, (i + 1) * tq
        qn = qn_ref[lo:hi, :]
        qr = qr_ref[lo:hi, :]
        outs = []
        for hh in range(2):
            mn = (lane // 64 == hh).astype(BF16)
            mr = (lane // 32 == quarter + hh).astype(BF16)
            qa = jnp.concatenate([qn * mn, qr * mr], axis=1)
            sd = jnp.where(causal, _nt_dot(qa, kk_ref[lo:hi, :]), -jnp.inf)
            m = jnp.max(sd, axis=-1, keepdims=True)
            if i > 0:
                so = _nt_dot(qa, kk_ref[:lo, :])
                m = jnp.maximum(m, jnp.max(so, axis=-1, keepdims=True))
            pd = jnp.exp(sd - m)
            l = jnp.sum(pd, axis=-1, keepdims=True)
            acc = _dot(pd.astype(BF16), v_ref[lo:hi, :])
            if i > 0:
                po = jnp.exp(so - m)
                l = l + jnp.sum(po, axis=-1, keepdims=True)
                acc = acc + _dot(po.astype(BF16), v_ref[:lo, :])
            outs.append(acc / l)
        o_ref[lo:hi, :] = jnp.where(lane < 64, outs[0], outs[1]).astype(BF16)


def _mla_attention(qnope, qrope, knope, krope, v, batch, seq):
    T = qnope.shape[0]
    blk = lambda f: pl.BlockSpec((seq, LANES), f)
    return pl.pallas_call(
        functools.partial(_mla_kernel, tq=MLA_TQ),
        grid=(batch, MLA_HEADS // 2),
        in_specs=[blk(lambda b, p: (b, p)), blk(lambda b, p: (b, p // 2)), blk(lambda b, p: (b, p)),
                  blk(lambda b, p: (b, 0)), blk(lambda b, p: (b, p))],
        out_specs=blk(lambda b, p: (b, p)),
        out_shape=jax.ShapeDtypeStruct((T, MLA_HEADS * MLA_V), BF16),
        scratch_shapes=[pltpu.VMEM((seq, 2 * LANES), BF16)],
        compiler_params=_cparams(("parallel", "parallel")),
        name="mla_attention",
    )(qnope, qrope, knope, krope, v)


def _ret_kernel(q_ref, k_ref, v_ref, g_ref, gn_ref, din_ref, qd_ref, kd_ref, cd_ref, o_ref):
    S = q_ref.shape[0]
    h = pl.program_id(1)
    lane = lax.broadcasted_iota(jnp.int32, (1, LANES), 1)
    qmask = (lane // RET_DK == h % 2).astype(F32)
    din = din_ref[0]
    qd = qd_ref[0] * qmask
    kd = kd_ref[0]
    cd = cd_ref[0]
    gain = gn_ref[...]
    rows = [slice(c * CHUNK, (c + 1) * CHUNK) for c in range(S // CHUNK)]
    kvs = [_dot((k_ref[r, :].astype(F32) * kd).T.astype(BF16), v_ref[r, :]) for r in rows[:-1]]
    states = [None]
    for kv in kvs:
        states.append(kv if states[-1] is None else states[-1] * cd + kv)
    for r, state in zip(rows, states):
        q = q_ref[r, :].astype(F32)
        inner = _nt_dot((q * qmask).astype(BF16), k_ref[r, :]) * din
        y = _dot(inner.astype(BF16), v_ref[r, :])
        if state is not None:
            y = y + _dot((q * qd).astype(BF16), state.astype(BF16))
        mu = jnp.mean(y, axis=-1, keepdims=True)
        var = jnp.mean(jnp.square(y - mu), axis=-1, keepdims=True)
        yn = (y - mu) * lax.rsqrt(var + EPS) * gain
        o_ref[r, :] = (g_ref[r, :].astype(F32) * yn).astype(BF16)


def _retention(rq, rk, rv, rg, gn, din, qd, kd, cd, batch, seq):
    T = rq.shape[0]
    blk = lambda f: pl.BlockSpec((seq, LANES), f)
    tab = pl.BlockSpec((1, CHUNK, LANES), lambda b, h: (h, 0, 0))
    return pl.pallas_call(
        _ret_kernel,
        grid=(batch, RET_HEADS),
        in_specs=[blk(lambda b, h: (b, h // 2)), blk(lambda b, h: (b, h // 2)), blk(lambda b, h: (b, h)),
                  blk(lambda b, h: (b, h)), pl.BlockSpec((1, LANES), lambda b, h: (0, h)), tab, tab, tab,
                  pl.BlockSpec((1, 1, LANES), lambda b, h: (h, 0, 0))],
        out_specs=blk(lambda b, h: (b, h)),
        out_shape=jax.ShapeDtypeStruct((T, RET_HEADS * RET_DV), BF16),
        compiler_params=_cparams(("parallel", "parallel")),
        name="retention",
    )(rq, rk, rv, rg, gn, din, qd, kd, cd)


def _ffn_kernel(h_ref, a_ref, b_ref, wa_ref, wb_ref, g_ref, wg_ref, wu_ref, wd_ref, o_ref, xn_ref, acc_ref):
    j = pl.program_id(1)

    @pl.when(j == 0)
    def _():
        hn = h_ref[...] + _dot(a_ref[...], wa_ref[...]) + _dot(b_ref[...], wb_ref[...])
        acc_ref[...] = hn
        xn_ref[...] = _rms(hn, g_ref[...]).astype(BF16)

    xn = xn_ref[...]
    gate = _dot(xn, wg_ref[...])
    act = gate * (1.0 / (1.0 + jnp.exp(-gate))) * _dot(xn, wu_ref[...])
    acc_ref[...] += _dot(act.astype(BF16), wd_ref[...])

    @pl.when(j == pl.num_programs(1) - 1)
    def _():
        o_ref[...] = acc_ref[...]


def _mixer_out_ffn(h, a, b, wa, wb, g, wg, wu, wd):
    T, D = h.shape
    F = wg.shape[1]
    tm, nf = ROW_TILE, FF_STEPS
    tf = F // nf
    row = lambda w: pl.BlockSpec((tm, w), lambda i, j: (i, 0))
    full = lambda x: pl.BlockSpec(x.shape, lambda i, j: (0, 0))
    return pl.pallas_call(
        _ffn_kernel,
        grid=(T // tm, nf),
        in_specs=[row(D), row(a.shape[1]), row(b.shape[1]), full(wa), full(wb), full(g),
                  pl.BlockSpec((D, tf), lambda i, j: (0, j)), pl.BlockSpec((D, tf), lambda i, j: (0, j)),
                  pl.BlockSpec((tf, D), lambda i, j: (j, 0))],
        out_specs=row(D),
        out_shape=jax.ShapeDtypeStruct((T, D), F32),
        scratch_shapes=[pltpu.VMEM((tm, D), BF16), pltpu.VMEM((tm, D), F32)],
        compiler_params=_cparams(("parallel", "arbitrary")),
        name="mixer_out_ffn",
    )(h, a, b, wa, wb, g, wg, wu, wd)


def _proj1_kernel(h_ref, g_ref, win_ref, sq_ref, sk_ref, sv_ref, bq_ref, bk_ref, bv_ref):
    xn = _rms(h_ref[...], g_ref[...]).astype(BF16)

    def mm(lo, hi):
        return _dot(xn, win_ref[:, lo:hi])

    sq_ref[...] = (mm(0, 1024) * (SWA_DIM ** -0.5)).astype(BF16)
    sk_ref[...] = mm(1024, 1152).astype(BF16)
    sv_ref[...] = mm(1152, 1280).astype(BF16)
    bq_ref[...] = (mm(1280, 1792) * (SB_DIM ** -0.5)).astype(BF16)
    bk_ref[...] = mm(1792, 2304).astype(BF16)
    bv_ref[...] = mm(2304, 2816).astype(BF16)


def _proj1(h, g, win):
    T, D = h.shape
    tm = ROW_TILE
    row = lambda w: pl.BlockSpec((tm, w), lambda i: (i, 0))
    full = lambda a: pl.BlockSpec(a.shape, lambda i: (0, 0))
    widths = (1024, 128, 128, 512, 512, 512)
    return pl.pallas_call(
        _proj1_kernel,
        grid=(T // tm,),
        in_specs=[row(D), full(g), full(win)],
        out_specs=[row(w) for w in widths],
        out_shape=[jax.ShapeDtypeStruct((T, w), BF16) for w in widths],
        compiler_params=_cparams(("parallel",)),
        name="proj1",
    )(h, g, win)


def _swa_kernel(tab_ref, q_ref, k_ref, v_ref, o_ref):
    S = q_ref.shape[0]
    W = WINDOW
    j = pl.program_id(1)
    lane = lax.broadcasted_iota(jnp.int32, (1, LANES), 1)
    r2 = lax.broadcasted_iota(jnp.int32, (W, 2 * W), 0)
    c2 = lax.broadcasted_iota(jnp.int32, (W, 2 * W), 1)
    dist2 = r2 + W - c2
    r1 = lax.broadcasted_iota(jnp.int32, (W, W), 0)
    c1 = lax.broadcasted_iota(jnp.int32, (W, W), 1)
    dist1 = r1 - c1
    for i in range(S // W):
        q = q_ref[i * W:(i + 1) * W, :]
        if i == 0:
            k, v, dist = k_ref[0:W, :], v_ref[0:W, :], dist1
        else:
            k, v, dist = k_ref[(i - 1) * W:(i + 1) * W, :], v_ref[(i - 1) * W:(i + 1) * W, :], dist2
        valid = (dist >= 0) & (dist < W)
        distf = dist.astype(F32)
        outs = []
        for hh in range(2):
            head = j + (SWA_HEADS // 2) * hh
            sink = tab_ref[head]
            slope = tab_ref[SWA_HEADS + head]
            s = _nt_dot(q * (lane // 64 == hh).astype(BF16), k) - slope * distf
            s = jnp.where(valid, s, -jnp.inf)
            m = jnp.maximum(jnp.max(s, axis=-1, keepdims=True), sink)
            p = jnp.exp(s - m)
            den = jnp.sum(p, axis=-1, keepdims=True) + jnp.exp(sink - m)
            outs.append(_dot(p.astype(BF16), v) / den)
        o_ref[i * W:(i + 1) * W, :] = jnp.where(lane < 64, outs[0], outs[1]).astype(BF16)


def _swa_attention(tab, sq, sk, sv, batch, seq):
    T = sq.shape[0]
    blk = lambda f: pl.BlockSpec((seq, LANES), f)
    return pl.pallas_call(
        _swa_kernel,
        grid_spec=pltpu.PrefetchScalarGridSpec(
            num_scalar_prefetch=1,
            grid=(batch, SWA_HEADS // 2),
            in_specs=[blk(lambda b, j, t: (b, j)), blk(lambda b, j, t: (b, 0)), blk(lambda b, j, t: (b, 0))],
            out_specs=blk(lambda b, j, t: (b, j)),
        ),
        out_shape=jax.ShapeDtypeStruct((T, SWA_HEADS * SWA_DIM), BF16),
        compiler_params=_cparams(("parallel", "parallel")),
        name="swa_attention",
    )(tab, sq, sk, sv)


def _sb_kernel(q_ref, k_ref, v_ref, o_ref, *, t):
    S = q_ref.shape[0]
    lane = lax.broadcasted_iota(jnp.int32, (1, LANES), 1)
    row = lax.broadcasted_iota(jnp.int32, (t, t), 0)
    col = lax.broadcasted_iota(jnp.int32, (t, t), 1)
    before = col < row
    neg_later = jnp.where(row > col, -1.0, 0.0).astype(BF16)

    def softplus(z):
        return jnp.maximum(z, 0.0) + jnp.log(1.0 + jnp.exp2(jnp.abs(z) * (-LOG2E)))

    for i in range(S // t):
        rows = slice(i * t, (i + 1) * t)
        q = q_ref[rows, :]
        outs = []
        for hh in range(2):
            qm = q * (lane // 64 == hh).astype(BF16)
            z = _nt_dot(qm, k_ref[rows, :])
            sp = jnp.where(before, softplus(z), 0.0)
            a = jnp.where(before, jnp.exp((z - sp) + _dot(sp.astype(BF16), neg_later)), 0.0)
            acc = _dot(a.astype(BF16), v_ref[rows, :])
            run = -jnp.sum(sp, axis=-1, keepdims=True)
            for j in range(i - 1, -1, -1):
                krows = slice(j * t, (j + 1) * t)
                z = _nt_dot(qm, k_ref[krows, :])
                sp = softplus(z)
                a = jnp.exp((z - sp) + (_dot(sp.astype(BF16), neg_later) + run))
                acc = acc + _dot(a.astype(BF16), v_ref[krows, :])
                run = run - jnp.sum(sp, axis=-1, keepdims=True)
            outs.append(acc)
        o_ref[rows, :] = jnp.where(lane < 64, outs[0], outs[1]).astype(BF16)


def _sb_attention(bq, bk, bv, batch, seq):
    T = bq.shape[0]
    blk = pl.BlockSpec((seq, LANES), lambda b, p: (b, p))
    return pl.pallas_call(
        functools.partial(_sb_kernel, t=SB_T),
        grid=(batch, SB_HEADS // 2),
        in_specs=[blk, blk, blk],
        out_specs=blk,
        out_shape=jax.ShapeDtypeStruct((T, SB_HEADS * SB_DIM), BF16),
        compiler_params=_cparams(("parallel", "parallel")),
        name="sb_attention",
    )(bq, bk, bv)


def _split_bf16(x):
    hi = x.astype(BF16)
    return hi, (x - hi.astype(F32)).astype(BF16)


def _out_router_kernel(h_ref, a_ref, b_ref, wa_ref, wb_ref, g_ref, rhi_ref, rlo_ref, o_ref, route_ref):
    hn = h_ref[...] + _dot(a_ref[...], wa_ref[...]) + _dot(b_ref[...], wb_ref[...])
    o_ref[...] = hn
    xhi, xlo = _split_bf16(_rms(hn, g_ref[...]))
    logits = _dot(xhi, rhi_ref[...]) + (_dot(xhi, rlo_ref[...]) + _dot(xlo, rhi_ref[...]))
    lane = lax.broadcasted_iota(jnp.int32, logits.shape, 1)
    logits = jnp.where(lane < N_EXPERTS, logits, -jnp.inf)
    m1 = jnp.max(logits, axis=-1, keepdims=True)
    i1 = jnp.min(jnp.where(logits == m1, lane, LANES), axis=-1, keepdims=True)
    rest = jnp.where(lane == i1, -jnp.inf, logits)
    m2 = jnp.max(rest, axis=-1, keepdims=True)
    i2 = jnp.min(jnp.where(rest == m2, lane, LANES), axis=-1, keepdims=True)
    e2 = jnp.exp(m2 - m1)
    den = 1.0 + e2
    route = jnp.where(lane == 0, i1.astype(F32), 0.0)
    route = jnp.where(lane == 1, i2.astype(F32), route)
    route = jnp.where(lane == 2, 1.0 / den, route)
    route = jnp.where(lane == 3, e2 / den, route)
    route_ref[...] = route


def _mixer_out_router(h, a, b, wa, wb, g, rhi, rlo):
    T, D = h.shape
    tm = ROW_TILE
    row = lambda w: pl.BlockSpec((tm, w), lambda i: (i, 0))
    full = lambda x: pl.BlockSpec(x.shape, lambda i: (0, 0))
    return pl.pallas_call(
        _out_router_kernel,
        grid=(T // tm,),
        in_specs=[row(D), row(a.shape[1]), row(b.shape[1]), full(wa), full(wb), full(g), full(rhi), full(rlo)],
        out_specs=[row(D), row(LANES)],
        out_shape=[jax.ShapeDtypeStruct((T, D), F32), jax.ShapeDtypeStruct((T, LANES), F32)],
        compiler_params=_cparams(("parallel",)),
        name="mixer_out_router",
    )(h, a, b, wa, wb, g, rhi, rlo)


def _gather_rows(idx_ref, src_hbm, dst_ref, sem, base, n):
    for r in range(n):
        pltpu.make_async_copy(src_hbm.at[pl.ds(idx_ref[0, 0, base + r], 1)], dst_ref.at[pl.ds(base + r, 1)], sem).start()


def _wait_rows(src_hbm, dst_ref, sem):
    pltpu.make_async_copy(src_hbm.at[pl.ds(0, dst_ref.shape[0])], dst_ref, sem).wait()


def _cast_kernel(w_ref, o_ref):
    o_ref[...] = w_ref[...].astype(BF16)


def _to_bf16(w):
    E, K, N = w.shape
    blk = pl.BlockSpec((1, CAST_ROWS, N), lambda e, k: (e, k, 0))
    return pl.pallas_call(
        _cast_kernel,
        grid=(E, K // CAST_ROWS),
        in_specs=[blk],
        out_specs=blk,
        out_shape=jax.ShapeDtypeStruct(w.shape, BF16),
        compiler_params=_cparams(("parallel", "parallel")),
        name="cast_bf16",
    )(w)


def _moe_kernel(sched_ref, tok_ref, nxt_ref, h_hbm, g_ref, wg_ref, wu_ref, wd_ref, o_ref, xbuf, xn_ref, acc_ref, sem):
    i, j = pl.program_id(0), pl.program_id(1)
    nblk, nf = pl.num_programs(0), pl.num_programs(1)
    used = i < sched_ref[nblk]
    R = xbuf.shape[1]
    slot = i % 2
    share = R // MOE_FF_STEPS

    def gather_next_share():
        _gather_rows(nxt_ref, h_hbm, xbuf.at[1 - slot], sem.at[1 - slot], j * share, share)

    @pl.when(j == 0)
    def _():
        @pl.when(i == 0)
        def _():
            _gather_rows(tok_ref, h_hbm, xbuf.at[0], sem.at[0], 0, R)

        _wait_rows(h_hbm, xbuf.at[slot], sem.at[slot])
        xn_ref[...] = _rms(xbuf[slot], g_ref[...]).astype(BF16)

    @pl.when(used)
    def _():
        gather_next_share()
        xn = xn_ref[...]
        gate = _dot(xn, wg_ref[0])
        act = gate * (1.0 / (1.0 + jnp.exp(-gate))) * _dot(xn, wu_ref[0])
        y = _dot(act.astype(BF16), wd_ref[0])

        @pl.when(j == 0)
        def _():
            acc_ref[...] = y

        @pl.when(j > 0)
        def _():
            acc_ref[...] += y

    @pl.when(jnp.logical_not(used))
    def _():
        gather_next_share()

    @pl.when(j == nf - 1)
    def _():
        o_ref[...] = jnp.where(used, acc_ref[...], 0.0)

        @pl.when(i == nblk - 1)
        def _():
            _wait_rows(h_hbm, xbuf.at[1 - slot], sem.at[1 - slot])


def _moe_experts(sched, tok, h, g, wg, wu, wd):
    nblk, _, R = tok.shape
    D = h.shape[1]
    F = wg.shape[2]
    nf = MOE_FF_STEPS
    tf = F // nf

    def blk_i(i, s):
        return jnp.minimum(i, s[nblk] - 1)

    def col_j(i, j, s):
        return jnp.where(i < s[nblk], j, nf - 1)

    smem = lambda f: pl.BlockSpec((1, 1, R), f, memory_space=pltpu.SMEM)
    return pl.pallas_call(
        _moe_kernel,
        grid_spec=pltpu.PrefetchScalarGridSpec(
            num_scalar_prefetch=1,
            grid=(nblk, nf),
            in_specs=[smem(lambda i, j, s: (i, 0, 0)),
                      smem(lambda i, j, s: (jnp.minimum(i + 1, nblk - 1), 0, 0)),
                      pl.BlockSpec(memory_space=pl.ANY),
                      pl.BlockSpec(g.shape, lambda i, j, s: (0, 0)),
                      pl.BlockSpec((1, D, tf), lambda i, j, s: (s[blk_i(i, s)], 0, col_j(i, j, s))),
                      pl.BlockSpec((1, D, tf), lambda i, j, s: (s[blk_i(i, s)], 0, col_j(i, j, s))),
                      pl.BlockSpec((1, tf, D), lambda i, j, s: (s[blk_i(i, s)], col_j(i, j, s), 0))],
            out_specs=pl.BlockSpec((R, D), lambda i, j, s: (i, 0)),
            scratch_shapes=[pltpu.VMEM((2, R, D), F32), pltpu.VMEM((R, D), BF16), pltpu.VMEM((R, D), F32),
                            pltpu.SemaphoreType.DMA((2,))],
        ),
        out_shape=jax.ShapeDtypeStruct((nblk * R, D), F32),
        compiler_params=_cparams(("arbitrary", "arbitrary")),
        name="moe_experts",
    )(sched, tok, tok, h, g, wg, wu, wd)


def _combine_kernel(pos_ref, nxt_ref, h_ref, route_ref, g_ref, y_hbm, o_ref, ybuf, sem):
    i = pl.program_id(0)
    n = ybuf.shape[1]
    slot = i % 2

    @pl.when(i == 0)
    def _():
        _gather_rows(pos_ref, y_hbm, ybuf.at[0], sem.at[0], 0, n)

    _wait_rows(y_hbm, ybuf.at[slot], sem.at[slot])
    _gather_rows(nxt_ref, y_hbm, ybuf.at[1 - slot], sem.at[1 - slot], 0, n)
    tm = n // TOP_K
    route = route_ref[...]
    out = h_ref[...] + (ybuf[slot, :tm, :] * route[:, 2:3] + ybuf[slot, tm:, :] * route[:, 3:4])
    o_ref[...] = _rms(out, g_ref[...])

    @pl.when(i == pl.num_programs(0) - 1)
    def _():
        _wait_rows(y_hbm, ybuf.at[1 - slot], sem.at[1 - slot])


def _combine_norm(pos, h, route, g, yg):
    T, D = h.shape
    ntile, _, n = pos.shape
    tm = n // TOP_K
    row = lambda w: pl.BlockSpec((tm, w), lambda i: (i, 0))
    smem = lambda f: pl.BlockSpec((1, 1, n), f, memory_space=pltpu.SMEM)
    return pl.pallas_call(
        _combine_kernel,
        grid=(ntile,),
        in_specs=[smem(lambda i: (i, 0, 0)), smem(lambda i: (jnp.minimum(i + 1, ntile - 1), 0, 0)),
                  row(D), row(LANES), pl.BlockSpec(g.shape, lambda i: (0, 0)), pl.BlockSpec(memory_space=pl.ANY)],
        out_specs=row(D),
        out_shape=jax.ShapeDtypeStruct((T, D), F32),
        scratch_shapes=[pltpu.VMEM((2, n, D), F32), pltpu.SemaphoreType.DMA((2,))],
        compiler_params=_cparams(("arbitrary",)),
        name="combine_norm",
    )(pos, pos, h, route, g, yg)


def _rope_tables(seq, half):
    inv_freq = ROPE_BASE ** (-jnp.arange(half, dtype=F32) / half)
    ang = jnp.arange(seq).astype(F32)[:, None] * inv_freq[None, :]
    cos, sin = jnp.cos(ang), jnp.sin(ang)
    reps = LANES // (2 * half)
    return jnp.tile(jnp.concatenate([cos, cos], 1), (1, reps)), jnp.tile(jnp.concatenate([-sin, sin], 1), (1, reps))


def _retention_tables():
    log_gamma = jnp.log1p(-(2.0 ** (-5.0 - jnp.arange(RET_HEADS, dtype=F32))))
    idx = jnp.arange(CHUNK, dtype=F32)
    rel = idx[:, None] - idx[None, :]
    din = jnp.where(rel >= 0, jnp.exp(log_gamma[:, None, None] * jnp.maximum(rel, 0.0)), 0.0)
    qd = jnp.exp(log_gamma[:, None] * (idx + 1.0))
    kd = jnp.exp(log_gamma[:, None] * (CHUNK - 1.0 - idx))
    cd = jnp.exp(log_gamma * CHUNK)
    bc = lambda a: jnp.broadcast_to(a[:, :, None], (RET_HEADS, CHUNK, LANES))
    return din, bc(qd), bc(kd), jnp.broadcast_to(cd[:, None, None], (RET_HEADS, 1, LANES))


def kernel(x, ab_norm, ab_w_in, mla_q_norm, mla_w_uq, mla_kv_norm, mla_w_ukv, ret_gn, ab_w_out, ffn_norm,
           ffn_w_gate, ffn_w_up, ffn_w_down, cd_norm, cd_w_in, swa_sinks, cd_w_out, moe_norm, moe_router,
           moe_w_gate, moe_w_up, moe_w_down, final_norm):
    B, S, D = x.shape
    T = B * S
    h = x.reshape(T, D)
    row = lambda g: g.reshape(1, -1)

    w = ab_w_in[0]
    win0 = jnp.concatenate([w[:, :640], jnp.tile(w[:, 640:672], (1, 4)), w[:, 672:]], axis=1).astype(BF16)
    wuq = mla_w_uq[0].reshape(MLA_Q_RANK, MLA_HEADS, MLA_NOPE + MLA_ROPE)
    wuq = jnp.concatenate([wuq[:, :, :MLA_NOPE].reshape(MLA_Q_RANK, -1),
                           wuq[:, :, MLA_NOPE:].reshape(MLA_Q_RANK, -1)], axis=1).astype(BF16)
    wukv = mla_w_ukv[0].reshape(MLA_KV_RANK, MLA_HEADS, MLA_NOPE + MLA_V)
    wukv = jnp.concatenate([wukv[:, :, :MLA_NOPE].reshape(MLA_KV_RANK, -1),
                            wukv[:, :, MLA_NOPE:].reshape(MLA_KV_RANK, -1)], axis=1).astype(BF16)
    cm, sm = _rope_tables(S, MLA_ROPE // 2)
    cr, sr = _rope_tables(S, RET_DK // 2)
    qnope, qrope, knope, krope, v, rq, rk, rv, rg = _proj0(
        h, row(ab_norm[0]), win0, row(mla_q_norm[0]), wuq, row(mla_kv_norm[0]), wukv, cm, sm, cr, sr, S)
    mla = _mla_attention(qnope, qrope, knope, krope, v, B, S)
    ret = _retention(rq, rk, rv, rg, row(ret_gn[0]), *_retention_tables(), B, S)
    wo = ab_w_out[0].astype(BF16)
    nm = MLA_HEADS * MLA_V
    h = _mixer_out_ffn(h, mla, ret, wo[:nm], wo[nm:], row(ffn_norm[0]), _to_bf16(ffn_w_gate)[0],
                       _to_bf16(ffn_w_up)[0], _to_bf16(ffn_w_down)[0])

    w = cd_w_in[0]
    nq = SWA_HEADS * SWA_DIM
    pair_order = np.stack([np.arange(SWA_HEADS // 2), np.arange(SWA_HEADS // 2) + SWA_HEADS // 2], 1).reshape(-1)
    cols = (pair_order[:, None] * SWA_DIM + np.arange(SWA_DIM)[None, :]).reshape(-1)
    win1 = jnp.concatenate([w[:, :nq][:, cols], w[:, nq:]], axis=1).astype(BF16)
    sq, sk, sv, bq, bk, bv = _proj1(h, row(cd_norm[0]), win1)
    slopes = 2.0 ** (-8.0 * jnp.arange(1, SWA_HEADS + 1, dtype=F32) / SWA_HEADS)
    swa = _swa_attention(jnp.concatenate([swa_sinks[0].astype(F32), slopes]), sq, sk, sv, B, S)
    sb = _sb_attention(bq, bk, bv, B, S)
    wo = cd_w_out[0]
    wo_swa = wo[:nq][cols].astype(BF16)
    router = jnp.pad(moe_router[0], ((0, 0), (0, LANES - N_EXPERTS)))
    rhi = router.astype(BF16)
    rlo = (router - rhi.astype(F32)).astype(BF16)
    h, route = _mixer_out_router(h, swa, sb, wo_swa, wo[nq:].astype(BF16), row(moe_norm[0]), rhi, rlo)

    R = MOE_ROWS
    TK = T * TOP_K
    expert = route[:, :TOP_K].astype(jnp.int32).reshape(-1)
    onehot = (expert[:, None] == jnp.arange(N_EXPERTS, dtype=jnp.int32)[None, :]).astype(jnp.int32)
    csum = jnp.cumsum(onehot, axis=0)
    counts = csum[-1]
    rank = jnp.sum((csum - 1) * onehot, axis=1)
    padded = ((counts + R - 1) // R) * R
    pad_end = jnp.cumsum(padded)
    pad_start = pad_end - padded
    dest = pad_start[expert] + rank
    cap = -(-TK // R) * R + N_EXPERTS * R
    nblk = cap // R
    token = jnp.repeat(jnp.arange(T, dtype=jnp.int32), TOP_K)
    buf_tok = jnp.zeros((cap,), jnp.int32).at[dest].set(token)
    blk_exp = jnp.minimum(jnp.searchsorted(pad_end, jnp.arange(nblk, dtype=jnp.int32) * R, side='right'),
                          N_EXPERTS - 1).astype(jnp.int32)
    sched = jnp.concatenate([blk_exp, (pad_end[-1:] // R).astype(jnp.int32)])
    yg = _moe_experts(sched, buf_tok.reshape(nblk, 1, R), h, row(moe_norm[0]), _to_bf16(moe_w_gate[0]),
                      _to_bf16(moe_w_up[0]), _to_bf16(moe_w_down[0]))
    tm = COMBINE_ROWS
    pos = dest.reshape(T // tm, tm, TOP_K).transpose(0, 2, 1).reshape(T // tm, 1, TOP_K * tm)
    out = _combine_norm(pos, h, route, row(final_norm), yg)
    return out.reshape(B, S, D)
```

```python
import functools

import jax
import jax.numpy as jnp
import numpy as np
from jax import lax
from jax.experimental import pallas as pl
from jax.experimental.pallas import tpu as pltpu

F32 = jnp.float32
BF16 = jnp.bfloat16

LANES = 128
EPS = 1e-6
LOG2E = 1.4426950408889634
ROPE_BASE = 10000.0
CHUNK = 128
MLA_HEADS, MLA_NOPE, MLA_ROPE, MLA_V = 8, 64, 32, 64
MLA_Q_RANK, MLA_KV_RANK = 384, 256
RET_HEADS, RET_DK, RET_DV = 8, 64, 128
SWA_HEADS, SWA_KV_HEADS, SWA_DIM, WINDOW = 16, 2, 64, 128
SB_HEADS, SB_DIM = 8, 64
N_EXPERTS, TOP_K = 8, 2

VMEM_LIMIT = 56 * 1024 * 1024
ROW_TILE = 512
FF_STEPS = 2
MOE_FF_STEPS = 2
CAST_ROWS = 512
COMBINE_ROWS = 256
DISPATCH_STEPS = 32
DISPATCH_UNROLL = 16
MOE_ROWS = 512
MLA_TQ = 512
SB_T = 256


def _cparams(sem):
    return pltpu.CompilerParams(dimension_semantics=sem, vmem_limit_bytes=VMEM_LIMIT)


def _rms(x, g):
    return x * lax.rsqrt(jnp.mean(x * x, axis=-1, keepdims=True) + EPS) * g


def _rope_slab(x, cos, sin_signed, half):
    lane = lax.broadcasted_iota(jnp.int32, x.shape, 1)
    first = (lane % (2 * half)) < half
    partner = jnp.where(first, pltpu.roll(x, LANES - half, 1), pltpu.roll(x, half, 1))
    return x * cos + partner * sin_signed


def _nt_dot(a, b):
    return lax.dot_general(a, b, (((1,), (1,)), ((), ())), preferred_element_type=F32)


def _dot(a, b):
    return jnp.dot(a, b, preferred_element_type=F32)


def _proj0_kernel(h_ref, g_ref, win_ref, qn_ref, wuq_ref, kvn_ref, wukv_ref, cm_ref, sm_ref, cr_ref, sr_ref,
                  qnope_ref, qrope_ref, knope_ref, krope_ref, v_ref, rq_ref, rk_ref, rv_ref, rg_ref):
    xn = _rms(h_ref[...], g_ref[...]).astype(BF16)

    def mm(lo, hi):
        return _dot(xn, win_ref[:, lo:hi])

    cm, sm, cr, sr = cm_ref[...], sm_ref[...], cr_ref[...], sr_ref[...]
    q_scale = (MLA_NOPE + MLA_ROPE) ** -0.5
    qh = _dot(_rms(mm(0, 384), qn_ref[...]).astype(BF16), wuq_ref[...])
    qnope_ref[...] = (qh[:, :512] * q_scale).astype(BF16)
    for s in range(2):
        slab = qh[:, 512 + LANES * s:512 + LANES * (s + 1)]
        qrope_ref[:, LANES * s:LANES * (s + 1)] = (_rope_slab(slab, cm, sm, MLA_ROPE // 2) * q_scale).astype(BF16)
    kvh = _dot(_rms(mm(384, 640), kvn_ref[...]).astype(BF16), wukv_ref[...])
    knope_ref[...] = kvh[:, :512].astype(BF16)
    v_ref[...] = kvh[:, 512:].astype(BF16)
    krope_ref[...] = _rope_slab(mm(640, 768), cm, sm, MLA_ROPE // 2).astype(BF16)
    rq = mm(768, 1280)
    rk = mm(1280, 1792)
    for s in range(4):
        sl = slice(LANES * s, LANES * (s + 1))
        rq_ref[:, sl] = _rope_slab(rq[:, sl], cr, sr, RET_DK // 2).astype(BF16)
        rk_ref[:, sl] = (_rope_slab(rk[:, sl], cr, sr, RET_DK // 2) * (RET_DK ** -0.5)).astype(BF16)
    rv_ref[...] = mm(1792, 2816).astype(BF16)
    rg = mm(2816, 3840)
    rg_ref[...] = (rg * (1.0 / (1.0 + jnp.exp(-rg)))).astype(BF16)


def _proj0(h, g, win, qn, wuq, kvn, wukv, cm, sm, cr, sr, seq):
    T, D = h.shape
    tm = ROW_TILE
    nseq = seq // tm
    row = lambda w: pl.BlockSpec((tm, w), lambda i: (i, 0))
    full = lambda a: pl.BlockSpec(a.shape, lambda i: (0, 0))
    pos = pl.BlockSpec((tm, LANES), lambda i: (i % nseq, 0))
    widths = (512, 256, 512, 128, 512, 512, 512, 1024, 1024)
    return pl.pallas_call(
        _proj0_kernel,
        grid=(T // tm,),
        in_specs=[row(D), full(g), full(win), full(qn), full(wuq), full(kvn), full(wukv), pos, pos, pos, pos],
        out_specs=[row(w) for w in widths],
        out_shape=[jax.ShapeDtypeStruct((T, w), BF16) for w in widths],
        compiler_params=_cparams(("parallel",)),
        name="proj0",
    )(h, g, win, qn, wuq, kvn, wukv, cm, sm, cr, sr)


def _mla_kernel(qn_ref, qr_ref, kn_ref, kr_ref, v_ref, o_ref, kk_ref, *, tq):
    S = qn_ref.shape[0]
    p = pl.program_id(1)
    kk_ref[:, :LANES] = kn_ref[...]
    kk_ref[:, LANES:] = kr_ref[...]
    lane = lax.broadcasted_iota(jnp.int32, (1, LANES), 1)
    quarter = (p % 2) * 2
    row = lax.broadcasted_iota(jnp.int32, (tq, tq), 0)
    col = lax.broadcasted_iota(jnp.int32, (tq, tq), 1)
    causal = col <= row
    for i in range(S // tq):
        lo, hi = i * tq, (i + 1) * tq
        qn = qn_ref[lo:hi, :]
        qr = qr_ref[lo:hi, :]
        outs = []
        for hh in range(2):
            mn = (lane // 64 == hh).astype(BF16)
            mr = (lane // 32 == quarter + hh).astype(BF16)
            qa = jnp.concatenate([qn * mn, qr * mr], axis=1)
            sd = jnp.where(causal, _nt_dot(qa, kk_ref[lo:hi, :]), -jnp.inf)
            m = jnp.max(sd, axis=-1, keepdims=True)
            if i > 0:
                so = _nt_dot(qa, kk_ref[:lo, :])
                m = jnp.maximum(m, jnp.max(so, axis=-1, keepdims=True))
            pd = jnp.exp(sd - m)
            l = jnp.sum(pd, axis=-1, keepdims=True)
            acc = _dot(pd.astype(BF16), v_ref[lo:hi, :])
            if i > 0:
                po = jnp.exp(so - m)
                l = l + jnp.sum(po, axis=-1, keepdims=True)
                acc = acc + _dot(po.astype(BF16), v_ref[:lo, :])
            outs.append(acc / l)
        o_ref[lo:hi, :] = jnp.where(lane < 64, outs[0], outs[1]).astype(BF16)


def _mla_attention(qnope, qrope, knope, krope, v, batch, seq):
    T = qnope.shape[0]
    blk = lambda f: pl.BlockSpec((seq, LANES), f)
    return pl.pallas_call(
        functools.partial(_mla_kernel, tq=MLA_TQ),
        grid=(batch, MLA_HEADS // 2),
        in_specs=[blk(lambda b, p: (b, p)), blk(lambda b, p: (b, p // 2)), blk(lambda b, p: (b, p)),
                  blk(lambda b, p: (b, 0)), blk(lambda b, p: (b, p))],
        out_specs=blk(lambda b, p: (b, p)),
        out_shape=jax.ShapeDtypeStruct((T, MLA_HEADS * MLA_V), BF16),
        scratch_shapes=[pltpu.VMEM((seq, 2 * LANES), BF16)],
        compiler_params=_cparams(("parallel", "parallel")),
        name="mla_attention",
    )(qnope, qrope, knope, krope, v)


def _ret_kernel(q_ref, k_ref, v_ref, g_ref, gn_ref, din_ref, qd_ref, kd_ref, cd_ref, o_ref):
    S = q_ref.shape[0]
    h = pl.program_id(1)
    lane = lax.broadcasted_iota(jnp.int32, (1, LANES), 1)
    qmask = (lane // RET_DK == h % 2).astype(F32)
    din = din_ref[0]
    qd = qd_ref[0] * qmask
    kd = kd_ref[0]
    cd = cd_ref[0]
    gain = gn_ref[...]
    rows = [slice(c * CHUNK, (c + 1) * CHUNK) for c in range(S // CHUNK)]
    kvs = [_dot((k_ref[r, :].astype(F32) * kd).T.astype(BF16), v_ref[r, :]) for r in rows[:-1]]
    states = [None]
    for kv in kvs:
        states.append(kv if states[-1] is None else states[-1] * cd + kv)
    for r, state in zip(rows, states):
        q = q_ref[r, :].astype(F32)
        inner = _nt_dot((q * qmask).astype(BF16), k_ref[r, :]) * din
        y = _dot(inner.astype(BF16), v_ref[r, :])
        if state is not None:
            y = y + _dot((q * qd).astype(BF16), state.astype(BF16))
        mu = jnp.mean(y, axis=-1, keepdims=True)
        var = jnp.mean(jnp.square(y - mu), axis=-1, keepdims=True)
        yn = (y - mu) * lax.rsqrt(var + EPS) * gain
        o_ref[r, :] = (g_ref[r, :].astype(F32) * yn).astype(BF16)


def _retention(rq, rk, rv, rg, gn, din, qd, kd, cd, batch, seq):
    T = rq.shape[0]
    blk = lambda f: pl.BlockSpec((seq, LANES), f)
    tab = pl.BlockSpec((1, CHUNK, LANES), lambda b, h: (h, 0, 0))
    return pl.pallas_call(
        _ret_kernel,
        grid=(batch, RET_HEADS),
        in_specs=[blk(lambda b, h: (b, h // 2)), blk(lambda b, h: (b, h // 2)), blk(lambda b, h: (b, h)),
                  blk(lambda b, h: (b, h)), pl.BlockSpec((1, LANES), lambda b, h: (0, h)), tab, tab, tab,
                  pl.BlockSpec((1, 1, LANES), lambda b, h: (h, 0, 0))],
        out_specs=blk(lambda b, h: (b, h)),
        out_shape=jax.ShapeDtypeStruct((T, RET_HEADS * RET_DV), BF16),
        compiler_params=_cparams(("parallel", "parallel")),
        name="retention",
    )(rq, rk, rv, rg, gn, din, qd, kd, cd)


def _ffn_kernel(h_ref, a_ref, b_ref, wa_ref, wb_ref, g_ref, wg_ref, wu_ref, wd_ref, o_ref, xn_ref, acc_ref):
    j = pl.program_id(1)

    @pl.when(j == 0)
    def _():
        hn = h_ref[...] + _dot(a_ref[...], wa_ref[...]) + _dot(b_ref[...], wb_ref[...])
        acc_ref[...] = hn
        xn_ref[...] = _rms(hn, g_ref[...]).astype(BF16)

    xn = xn_ref[...]
    gate = _dot(xn, wg_ref[...])
    act = gate * (1.0 / (1.0 + jnp.exp(-gate))) * _dot(xn, wu_ref[...])
    acc_ref[...] += _dot(act.astype(BF16), wd_ref[...])

    @pl.when(j == pl.num_programs(1) - 1)
    def _():
        o_ref[...] = acc_ref[...]


def _mixer_out_ffn(h, a, b, wa, wb, g, wg, wu, wd):
    T, D = h.shape
    F = wg.shape[1]
    tm, nf = ROW_TILE, FF_STEPS
    tf = F // nf
    row = lambda w: pl.BlockSpec((tm, w), lambda i, j: (i, 0))
    full = lambda x: pl.BlockSpec(x.shape, lambda i, j: (0, 0))
    return pl.pallas_call(
        _ffn_kernel,
        grid=(T // tm, nf),
        in_specs=[row(D), row(a.shape[1]), row(b.shape[1]), full(wa), full(wb), full(g),
                  pl.BlockSpec((D, tf), lambda i, j: (0, j)), pl.BlockSpec((D, tf), lambda i, j: (0, j)),
                  pl.BlockSpec((tf, D), lambda i, j: (j, 0))],
        out_specs=row(D),
        out_shape=jax.ShapeDtypeStruct((T, D), F32),
        scratch_shapes=[pltpu.VMEM((tm, D), BF16), pltpu.VMEM((tm, D), F32)],
        compiler_params=_cparams(("parallel", "arbitrary")),
        name="mixer_out_ffn",
    )(h, a, b, wa, wb, g, wg, wu, wd)


def _proj1_kernel(h_ref, g_ref, win_ref, sq_ref, sk_ref, sv_ref, bq_ref, bk_ref, bv_ref):
    xn = _rms(h_ref[...], g_ref[...]).astype(BF16)

    def mm(lo, hi):
        return _dot(xn, win_ref[:, lo:hi])

    sq_ref[...] = (mm(0, 1024) * (SWA_DIM ** -0.5)).astype(BF16)
    sk_ref[...] = mm(1024, 1152).astype(BF16)
    sv_ref[...] = mm(1152, 1280).astype(BF16)
    bq_ref[...] = (mm(1280, 1792) * (SB_DIM ** -0.5)).astype(BF16)
    bk_ref[...] = mm(1792, 2304).astype(BF16)
    bv_ref[...] = mm(2304, 2816).astype(BF16)


def _proj1(h, g, win):
    T, D = h.shape
    tm = ROW_TILE
    row = lambda w: pl.BlockSpec((tm, w), lambda i: (i, 0))
    full = lambda a: pl.BlockSpec(a.shape, lambda i: (0, 0))
    widths = (1024, 128, 128, 512, 512, 512)
    return pl.pallas_call(
        _proj1_kernel,
        grid=(T // tm,),
        in_specs=[row(D), full(g), full(win)],
        out_specs=[row(w) for w in widths],
        out_shape=[jax.ShapeDtypeStruct((T, w), BF16) for w in widths],
        compiler_params=_cparams(("parallel",)),
        name="proj1",
    )(h, g, win)


def _swa_kernel(tab_ref, q_ref, k_ref, v_ref, o_ref):
    S = q_ref.shape[0]
    W = WINDOW
    j = pl.program_id(1)
    lane = lax.broadcasted_iota(jnp.int32, (1, LANES), 1)
    r2 = lax.broadcasted_iota(jnp.int32, (W, 2 * W), 0)
    c2 = lax.broadcasted_iota(jnp.int32, (W, 2 * W), 1)
    dist2 = r2 + W - c2
    r1 = lax.broadcasted_iota(jnp.int32, (W, W), 0)
    c1 = lax.broadcasted_iota(jnp.int32, (W, W), 1)
    dist1 = r1 - c1
    for i in range(S // W):
        q = q_ref[i * W:(i + 1) * W, :]
        if i == 0:
            k, v, dist = k_ref[0:W, :], v_ref[0:W, :], dist1
        else:
            k, v, dist = k_ref[(i - 1) * W:(i + 1) * W, :], v_ref[(i - 1) * W:(i + 1) * W, :], dist2
        valid = (dist >= 0) & (dist < W)
        distf = dist.astype(F32)
        outs = []
        for hh in range(2):
            head = j + (SWA_HEADS // 2) * hh
            sink = tab_ref[head]
            slope = tab_ref[SWA_HEADS + head]
            s = _nt_dot(q * (lane // 64 == hh).astype(BF16), k) - slope * distf
            s = jnp.where(valid, s, -jnp.inf)
            m = jnp.maximum(jnp.max(s, axis=-1, keepdims=True), sink)
            p = jnp.exp(s - m)
            den = jnp.sum(p, axis=-1, keepdims=True) + jnp.exp(sink - m)
            outs.append(_dot(p.astype(BF16), v) / den)
        o_ref[i * W:(i + 1) * W, :] = jnp.where(lane < 64, outs[0], outs[1]).astype(BF16)


def _swa_attention(tab, sq, sk, sv, batch, seq):
    T = sq.shape[0]
    blk = lambda f: pl.BlockSpec((seq, LANES), f)
    return pl.pallas_call(
        _swa_kernel,
        grid_spec=pltpu.PrefetchScalarGridSpec(
            num_scalar_prefetch=1,
            grid=(batch, SWA_HEADS // 2),
            in_specs=[blk(lambda b, j, t: (b, j)), blk(lambda b, j, t: (b, 0)), blk(lambda b, j, t: (b, 0))],
            out_specs=blk(lambda b, j, t: (b, j)),
        ),
        out_shape=jax.ShapeDtypeStruct((T, SWA_HEADS * SWA_DIM), BF16),
        compiler_params=_cparams(("parallel", "parallel")),
        name="swa_attention",
    )(tab, sq, sk, sv)


def _sb_kernel(q_ref, k_ref, v_ref, o_ref, *, t):
    S = q_ref.shape[0]
    lane = lax.broadcasted_iota(jnp.int32, (1, LANES), 1)
    row = lax.broadcasted_iota(jnp.int32, (t, t), 0)
    col = lax.broadcasted_iota(jnp.int32, (t, t), 1)
    before = col < row
    neg_later = jnp.where(row > col, -1.0, 0.0).astype(BF16)

    def softplus(z):
        return jnp.maximum(z, 0.0) + jnp.log(1.0 + jnp.exp2(jnp.abs(z) * (-LOG2E)))

    for i in range(S // t):
        rows = slice(i * t, (i + 1) * t)
        q = q_ref[rows, :]
        outs = []
        for hh in range(2):
            qm = q * (lane // 64 == hh).astype(BF16)
            z = _nt_dot(qm, k_ref[rows, :])
            sp = jnp.where(before, softplus(z), 0.0)
            a = jnp.where(before, jnp.exp((z - sp) + _dot(sp.astype(BF16), neg_later)), 0.0)
            acc = _dot(a.astype(BF16), v_ref[rows, :])
            run = -jnp.sum(sp, axis=-1, keepdims=True)
            for j in range(i - 1, -1, -1):
                krows = slice(j * t, (j + 1) * t)
                z = _nt_dot(qm, k_ref[krows, :])
                sp = softplus(z)
                a = jnp.exp((z - sp) + (_dot(sp.astype(BF16), neg_later) + run))
                acc = acc + _dot(a.astype(BF16), v_ref[krows, :])
                run = run - jnp.sum(sp, axis=-1, keepdims=True)
            outs.append(acc)
        o_ref[rows, :] = jnp.where(lane < 64, outs[0], outs[1]).astype(BF16)


def _sb_attention(bq, bk, bv, batch, seq):
    T = bq.shape[0]
    blk = pl.BlockSpec((seq, LANES), lambda b, p: (b, p))
    return pl.pallas_call(
        functools.partial(_sb_kernel, t=SB_T),
        grid=(batch, SB_HEADS // 2),
        in_specs=[blk, blk, blk],
        out_specs=blk,
        out_shape=jax.ShapeDtypeStruct((T, SB_HEADS * SB_DIM), BF16),
        compiler_params=_cparams(("parallel", "parallel")),
        name="sb_attention",
    )(bq, bk, bv)


def _split_bf16(x):
    hi = x.astype(BF16)
    return hi, (x - hi.astype(F32)).astype(BF16)


def _out_router_kernel(h_ref, a_ref, b_ref, wa_ref, wb_ref, g_ref, rhi_ref, rlo_ref, o_ref, route_ref):
    hn = h_ref[...] + _dot(a_ref[...], wa_ref[...]) + _dot(b_ref[...], wb_ref[...])
    o_ref[...] = hn
    xhi, xlo = _split_bf16(_rms(hn, g_ref[...]))
    logits = _dot(xhi, rhi_ref[...]) + (_dot(xhi, rlo_ref[...]) + _dot(xlo, rhi_ref[...]))
    lane = lax.broadcasted_iota(jnp.int32, logits.shape, 1)
    logits = jnp.where(lane < N_EXPERTS, logits, -jnp.inf)
    m1 = jnp.max(logits, axis=-1, keepdims=True)
    i1 = jnp.min(jnp.where(logits == m1, lane, LANES), axis=-1, keepdims=True)
    rest = jnp.where(lane == i1, -jnp.inf, logits)
    m2 = jnp.max(rest, axis=-1, keepdims=True)
    i2 = jnp.min(jnp.where(rest == m2, lane, LANES), axis=-1, keepdims=True)
    e2 = jnp.exp(m2 - m1)
    den = 1.0 + e2
    route = jnp.where(lane == 0, i1.astype(F32), 0.0)
    route = jnp.where(lane == 1, i2.astype(F32), route)
    route = jnp.where(lane == 2, 1.0 / den, route)
    route = jnp.where(lane == 3, e2 / den, route)
    route_ref[...] = route


def _mixer_out_router(h, a, b, wa, wb, g, rhi, rlo):
    T, D = h.shape
    tm = ROW_TILE
    row = lambda w: pl.BlockSpec((tm, w), lambda i: (i, 0))
    full = lambda x: pl.BlockSpec(x.shape, lambda i: (0, 0))
    return pl.pallas_call(
        _out_router_kernel,
        grid=(T // tm,),
        in_specs=[row(D), row(a.shape[1]), row(b.shape[1]), full(wa), full(wb), full(g), full(rhi), full(rlo)],
        out_specs=[row(D), row(LANES)],
        out_shape=[jax.ShapeDtypeStruct((T, D), F32), jax.ShapeDtypeStruct((T, LANES), F32)],
        compiler_params=_cparams(("parallel",)),
        name="mixer_out_router",
    )(h, a, b, wa, wb, g, rhi, rlo)


def _gather_rows(idx_ref, src_hbm, dst_ref, sem, base, n):
    for r in range(n):
        pltpu.make_async_copy(src_hbm.at[pl.ds(idx_ref[0, 0, base + r], 1)], dst_ref.at[pl.ds(base + r, 1)], sem).start()


def _wait_rows(src_hbm, dst_ref, sem):
    pltpu.make_async_copy(src_hbm.at[pl.ds(0, dst_ref.shape[0])], dst_ref, sem).wait()


def _cast_kernel(w_ref, o_ref):
    o_ref[...] = w_ref[...].astype(BF16)


def _to_bf16(w):
    E, K, N = w.shape
    blk = pl.BlockSpec((1, CAST_ROWS, N), lambda e, k: (e, k, 0))
    return pl.pallas_call(
        _cast_kernel,
        grid=(E, K // CAST_ROWS),
        in_specs=[blk],
        out_specs=blk,
        out_shape=jax.ShapeDtypeStruct(w.shape, BF16),
        compiler_params=_cparams(("parallel", "parallel")),
        name="cast_bf16",
    )(w)


def _dispatch_kernel(meta_ref, dest_ref, wg_ref, wu_ref, wd_ref, h_ref, og_ref, ou_ref, od_ref, xg_hbm,
                     zero_ref, sem, zsem):
    step = pl.program_id(0) * pl.num_programs(1) + pl.program_id(1)
    tpb = h_ref.shape[0]
    R = zero_ref.shape[0]

    def zero_block(start):
        return pltpu.make_async_copy(zero_ref, xg_hbm.at[pl.ds(pl.multiple_of(start, R), R)], zsem)

    cap = xg_hbm.shape[0]
    fills = [(meta_ref[N_EXPERTS + e] > 0, meta_ref[e] - R) for e in range(N_EXPERTS)]
    fills += [(meta_ref[N_EXPERTS - 1] + b * R < cap, meta_ref[N_EXPERTS - 1] + b * R) for b in range(N_EXPERTS)]

    @pl.when(step == 0)
    def _():
        zero_ref[...] = jnp.zeros_like(zero_ref)
        for do_start in (True, False):
            for cond, start in fills:
                @pl.when(cond)
                def _():
                    zero_block(start).start() if do_start else zero_block(start).wait()

    def scatter_group(gi, carry):
        base = gi * DISPATCH_UNROLL
        rows = [[dest_ref[0, 0, k * tpb + base + u] for k in range(TOP_K)]
                for u in range(DISPATCH_UNROLL)]
        for u in range(DISPATCH_UNROLL):
            for k in range(TOP_K):
                pltpu.make_async_copy(h_ref.at[pl.ds(base + u, 1)], xg_hbm.at[pl.ds(rows[u][k], 1)], sem).start()
        return carry

    lax.fori_loop(0, tpb // DISPATCH_UNROLL, scatter_group, 0)
    og_ref[...] = wg_ref[...].astype(BF16)
    ou_ref[...] = wu_ref[...].astype(BF16)
    od_ref[...] = wd_ref[...].astype(BF16)
    for k in range(TOP_K):
        pltpu.make_async_copy(h_ref, xg_hbm.at[pl.ds(0, tpb)], sem).wait()


def _dispatch_cast(meta, dest, wg, wu, wd, h, cap):
    E, D, F = wg.shape
    nsteps, _, n = dest.shape
    nk = nsteps // E
    spec = lambda w: pl.BlockSpec((1, w.shape[1] // nk, w.shape[2]), lambda e, k, m: (e, k, 0))
    return pl.pallas_call(
        _dispatch_kernel,
        grid_spec=pltpu.PrefetchScalarGridSpec(
            num_scalar_prefetch=1,
            grid=(E, nk),
            in_specs=[pl.BlockSpec((1, 1, n), lambda e, k, m: (e * nk + k, 0, 0), memory_space=pltpu.SMEM),
                      spec(wg), spec(wu), spec(wd),
                      pl.BlockSpec((n // TOP_K, D), lambda e, k, m: (e * nk + k, 0))],
            out_specs=[spec(wg), spec(wu), spec(wd), pl.BlockSpec(memory_space=pl.ANY)],
            scratch_shapes=[pltpu.VMEM((MOE_ROWS, D), F32), pltpu.SemaphoreType.DMA, pltpu.SemaphoreType.DMA],
        ),
        out_shape=[jax.ShapeDtypeStruct(wg.shape, BF16), jax.ShapeDtypeStruct(wu.shape, BF16),
                   jax.ShapeDtypeStruct(wd.shape, BF16), jax.ShapeDtypeStruct((cap, D), F32)],
        compiler_params=_cparams(("arbitrary", "arbitrary")),
        name="dispatch_cast",
    )(meta, dest, wg, wu, wd, h)


def _moe_kernel(sched_ref, x_ref, g_ref, wg_ref, wu_ref, wd_ref, o_ref, xn_ref, acc_ref):
    i, j = pl.program_id(0), pl.program_id(1)
    nblk = pl.num_programs(0)
    used = i < sched_ref[nblk]

    @pl.when(j == 0)
    def _():
        xn_ref[...] = _rms(x_ref[...], g_ref[...]).astype(BF16)
        acc_ref[...] = jnp.zeros_like(acc_ref)

    @pl.when(used)
    def _():
        xn = xn_ref[...]
        gate = _dot(xn, wg_ref[0])
        act = gate * (1.0 / (1.0 + jnp.exp(-gate))) * _dot(xn, wu_ref[0])
        acc_ref[...] += _dot(act.astype(BF16), wd_ref[0])

    @pl.when(j == pl.num_programs(1) - 1)
    def _():
        o_ref[...] = acc_ref[...]


def _moe_experts(sched, xg, g, wg, wu, wd):
    cap, D = xg.shape
    F = wg.shape[2]
    R, nf = MOE_ROWS, MOE_FF_STEPS
    nblk, tf = cap // R, F // nf

    def blk_i(i, s):
        return jnp.minimum(i, s[nblk] - 1)

    def col_j(i, j, s):
        return jnp.where(i < s[nblk], j, nf - 1)

    return pl.pallas_call(
        _moe_kernel,
        grid_spec=pltpu.PrefetchScalarGridSpec(
            num_scalar_prefetch=1,
            grid=(nblk, nf),
            in_specs=[pl.BlockSpec((R, D), lambda i, j, s: (blk_i(i, s), 0)),
                      pl.BlockSpec(g.shape, lambda i, j, s: (0, 0)),
                      pl.BlockSpec((1, D, tf), lambda i, j, s: (s[blk_i(i, s)], 0, col_j(i, j, s))),
                      pl.BlockSpec((1, D, tf), lambda i, j, s: (s[blk_i(i, s)], 0, col_j(i, j, s))),
                      pl.BlockSpec((1, tf, D), lambda i, j, s: (s[blk_i(i, s)], col_j(i, j, s), 0))],
            out_specs=pl.BlockSpec((R, D), lambda i, j, s: (i, 0)),
            scratch_shapes=[pltpu.VMEM((R, D), BF16), pltpu.VMEM((R, D), F32)],
        ),
        out_shape=jax.ShapeDtypeStruct((cap, D), F32),
        compiler_params=_cparams(("arbitrary", "arbitrary")),
        name="moe_experts",
    )(sched, xg, g, wg, wu, wd)


def _combine_kernel(pos_ref, nxt_ref, h_ref, route_ref, g_ref, y_hbm, o_ref, ybuf, sem):
    i = pl.program_id(0)
    n = ybuf.shape[1]
    slot = i % 2

    @pl.when(i == 0)
    def _():
        _gather_rows(pos_ref, y_hbm, ybuf.at[0], sem.at[0], 0, n)

    _wait_rows(y_hbm, ybuf.at[slot], sem.at[slot])
    _gather_rows(nxt_ref, y_hbm, ybuf.at[1 - slot], sem.at[1 - slot], 0, n)
    tm = n // TOP_K
    route = route_ref[...]
    out = h_ref[...] + (ybuf[slot, :tm, :] * route[:, 2:3] + ybuf[slot, tm:, :] * route[:, 3:4])
    o_ref[...] = _rms(out, g_ref[...])

    @pl.when(i == pl.num_programs(0) - 1)
    def _():
        _wait_rows(y_hbm, ybuf.at[1 - slot], sem.at[1 - slot])


def _combine_norm(pos, h, route, g, yg):
    T, D = h.shape
    ntile, _, n = pos.shape
    tm = n // TOP_K
    row = lambda w: pl.BlockSpec((tm, w), lambda i: (i, 0))
    smem = lambda f: pl.BlockSpec((1, 1, n), f, memory_space=pltpu.SMEM)
    return pl.pallas_call(
        _combine_kernel,
        grid=(ntile,),
        in_specs=[smem(lambda i: (i, 0, 0)), smem(lambda i: (jnp.minimum(i + 1, ntile - 1), 0, 0)),
                  row(D), row(LANES), pl.BlockSpec(g.shape, lambda i: (0, 0)), pl.BlockSpec(memory_space=pl.ANY)],
        out_specs=row(D),
        out_shape=jax.ShapeDtypeStruct((T, D), F32),
        scratch_shapes=[pltpu.VMEM((2, n, D), F32), pltpu.SemaphoreType.DMA((2,))],
        compiler_params=_cparams(("arbitrary",)),
        name="combine_norm",
    )(pos, pos, h, route, g, yg)


def _rope_tables(seq, half):
    inv_freq = ROPE_BASE ** (-jnp.arange(half, dtype=F32) / half)
    ang = jnp.arange(seq).astype(F32)[:, None] * inv_freq[None, :]
    cos, sin = jnp.cos(ang), jnp.sin(ang)
    reps = LANES // (2 * half)
    return jnp.tile(jnp.concatenate([cos, cos], 1), (1, reps)), jnp.tile(jnp.concatenate([-sin, sin], 1), (1, reps))


def _retention_tables():
    log_gamma = jnp.log1p(-(2.0 ** (-5.0 - jnp.arange(RET_HEADS, dtype=F32))))
    idx = jnp.arange(CHUNK, dtype=F32)
    rel = idx[:, None] - idx[None, :]
    din = jnp.where(rel >= 0, jnp.exp(log_gamma[:, None, None] * jnp.maximum(rel, 0.0)), 0.0)
    qd = jnp.exp(log_gamma[:, None] * (idx + 1.0))
    kd = jnp.exp(log_gamma[:, None] * (CHUNK - 1.0 - idx))
    cd = jnp.exp(log_gamma * CHUNK)
    bc = lambda a: jnp.broadcast_to(a[:, :, None], (RET_HEADS, CHUNK, LANES))
    return din, bc(qd), bc(kd), jnp.broadcast_to(cd[:, None, None], (RET_HEADS, 1, LANES))


def kernel(x, ab_norm, ab_w_in, mla_q_norm, mla_w_uq, mla_kv_norm, mla_w_ukv, ret_gn, ab_w_out, ffn_norm,
           ffn_w_gate, ffn_w_up, ffn_w_down, cd_norm, cd_w_in, swa_sinks, cd_w_out, moe_norm, moe_router,
           moe_w_gate, moe_w_up, moe_w_down, final_norm):
    B, S, D = x.shape
    T = B * S
    h = x.reshape(T, D)
    row = lambda g: g.reshape(1, -1)

    w = ab_w_in[0]
    win0 = jnp.concatenate([w[:, :640], jnp.tile(w[:, 640:672], (1, 4)), w[:, 672:]], axis=1).astype(BF16)
    wuq = mla_w_uq[0].reshape(MLA_Q_RANK, MLA_HEADS, MLA_NOPE + MLA_ROPE)
    wuq = jnp.concatenate([wuq[:, :, :MLA_NOPE].reshape(MLA_Q_RANK, -1),
                           wuq[:, :, MLA_NOPE:].reshape(MLA_Q_RANK, -1)], axis=1).astype(BF16)
    wukv = mla_w_ukv[0].reshape(MLA_KV_RANK, MLA_HEADS, MLA_NOPE + MLA_V)
    wukv = jnp.concatenate([wukv[:, :, :MLA_NOPE].reshape(MLA_KV_RANK, -1),
                            wukv[:, :, MLA_NOPE:].reshape(MLA_KV_RANK, -1)], axis=1).astype(BF16)
    cm, sm = _rope_tables(S, MLA_ROPE // 2)
    cr, sr = _rope_tables(S, RET_DK // 2)
    qnope, qrope, knope, krope, v, rq, rk, rv, rg = _proj0(
        h, row(ab_norm[0]), win0, row(mla_q_norm[0]), wuq, row(mla_kv_norm[0]), wukv, cm, sm, cr, sr, S)
    mla = _mla_attention(qnope, qrope, knope, krope, v, B, S)
    ret = _retention(rq, rk, rv, rg, row(ret_gn[0]), *_retention_tables(), B, S)
    wo = ab_w_out[0].astype(BF16)
    nm = MLA_HEADS * MLA_V
    h = _mixer_out_ffn(h, mla, ret, wo[:nm], wo[nm:], row(ffn_norm[0]), _to_bf16(ffn_w_gate)[0],
                       _to_bf16(ffn_w_up)[0], _to_bf16(ffn_w_down)[0])

    w = cd_w_in[0]
    nq = SWA_HEADS * SWA_DIM
    pair_order = np.stack([np.arange(SWA_HEADS // 2), np.arange(SWA_HEADS // 2) + SWA_HEADS // 2], 1).reshape(-1)
    cols = (pair_order[:, None] * SWA_DIM + np.arange(SWA_DIM)[None, :]).reshape(-1)
    win1 = jnp.concatenate([w[:, :nq][:, cols], w[:, nq:]], axis=1).astype(BF16)
    sq, sk, sv, bq, bk, bv = _proj1(h, row(cd_norm[0]), win1)
    slopes = 2.0 ** (-8.0 * jnp.arange(1, SWA_HEADS + 1, dtype=F32) / SWA_HEADS)
    swa = _swa_attention(jnp.concatenate([swa_sinks[0].astype(F32), slopes]), sq, sk, sv, B, S)
    sb = _sb_attention(bq, bk, bv, B, S)
    wo = cd_w_out[0]
    wo_swa = wo[:nq][cols].astype(BF16)
    router = jnp.pad(moe_router[0], ((0, 0), (0, LANES - N_EXPERTS)))
    rhi = router.astype(BF16)
    rlo = (router - rhi.astype(F32)).astype(BF16)
    h, route = _mixer_out_router(h, swa, sb, wo_swa, wo[nq:].astype(BF16), row(moe_norm[0]), rhi, rlo)

    R = MOE_ROWS
    TK = T * TOP_K
    expert = route[:, :TOP_K].astype(jnp.int32).reshape(-1)
    onehot = (expert[:, None] == jnp.arange(N_EXPERTS, dtype=jnp.int32)[None, :]).astype(jnp.int32)
    csum = jnp.cumsum(onehot, axis=0)
    counts = csum[-1]
    rank = jnp.sum((csum - 1) * onehot, axis=1)
    padded = ((counts + R - 1) // R) * R
    pad_end = jnp.cumsum(padded)
    pad_start = pad_end - padded
    dest = pad_start[expert] + rank
    cap = -(-TK // R) * R + N_EXPERTS * R
    nblk = cap // R
    blk_exp = jnp.minimum(jnp.searchsorted(pad_end, jnp.arange(nblk, dtype=jnp.int32) * R, side='right'),
                          N_EXPERTS - 1).astype(jnp.int32)
    sched = jnp.concatenate([blk_exp, (pad_end[-1:] // R).astype(jnp.int32)])

    def tiles(tm):
        return dest.reshape(T // tm, tm, TOP_K).transpose(0, 2, 1).reshape(T // tm, 1, TOP_K * tm)

    meta = jnp.concatenate([pad_end, padded]).astype(jnp.int32)
    wg, wu, wd, xg = _dispatch_cast(meta, tiles(T // DISPATCH_STEPS), moe_w_gate[0], moe_w_up[0], moe_w_down[0],
                                    h, cap)
    yg = _moe_experts(sched, xg, row(moe_norm[0]), wg, wu, wd)
    pos = tiles(COMBINE_ROWS)
    out = _combine_norm(pos, h, route, row(final_norm), yg)
    return out.reshape(B, S, D)
```

```python
import functools

import jax
import jax.numpy as jnp
import numpy as np
from jax import lax
from jax.experimental import pallas as pl
from jax.experimental.pallas import tpu as pltpu

F32 = jnp.float32
BF16 = jnp.bfloat16

LANES = 128
EPS = 1e-6
LOG2E = 1.4426950408889634
ROPE_BASE = 10000.0
CHUNK = 128
MLA_HEADS, MLA_NOPE, MLA_ROPE, MLA_V = 8, 64, 32, 64
MLA_Q_RANK, MLA_KV_RANK = 384, 256
RET_HEADS, RET_DK, RET_DV = 8, 64, 128
SWA_HEADS, SWA_KV_HEADS, SWA_DIM, WINDOW = 16, 2, 64, 128
SB_HEADS, SB_DIM = 8, 64
N_EXPERTS, TOP_K = 8, 2

VMEM_LIMIT = 56 * 1024 * 1024
ROW_TILE = 512
FF_STEPS = 2
MOE_FF_STEPS = 2
CAST_ROWS = 512
COMBINE_ROWS = 256
DISPATCH_STEPS = 32
DISPATCH_UNROLL = 16
MOE_ROWS = 512
MLA_TQ = 512
SB_T = 256


def _cparams(sem):
    return pltpu.CompilerParams(dimension_semantics=sem, vmem_limit_bytes=VMEM_LIMIT)


def _rms(x, g):
    return x * lax.rsqrt(jnp.mean(x * x, axis=-1, keepdims=True) + EPS) * g


def _rope_slab(x, cos, sin_signed, half):
    lane = lax.broadcasted_iota(jnp.int32, x.shape, 1)
    first = (lane % (2 * half)) < half
    partner = jnp.where(first, pltpu.roll(x, LANES - half, 1), pltpu.roll(x, half, 1))
    return x * cos + partner * sin_signed


def _nt_dot(a, b):
    return lax.dot_general(a, b, (((1,), (1,)), ((), ())), preferred_element_type=F32)


def _dot(a, b):
    return jnp.dot(a, b, preferred_element_type=F32)


def _proj0_kernel(h_ref, g_ref, win_ref, qn_ref, wuq_ref, kvn_ref, wukv_ref, cm_ref, sm_ref, cr_ref, sr_ref,
                  qnope_ref, qrope_ref, knope_ref, krope_ref, v_ref, rq_ref, rk_ref, rv_ref, rg_ref):
    xn = _rms(h_ref[...], g_ref[...]).astype(BF16)

    def mm(lo, hi):
        return _dot(xn, win_ref[:, lo:hi])

    cm, sm, cr, sr = cm_ref[...], sm_ref[...], cr_ref[...], sr_ref[...]
    q_scale = (MLA_NOPE + MLA_ROPE) ** -0.5
    qh = _dot(_rms(mm(0, 384), qn_ref[...]).astype(BF16), wuq_ref[...])
    qnope_ref[...] = (qh[:, :512] * q_scale).astype(BF16)
    for s in range(2):
        slab = qh[:, 512 + LANES * s:512 + LANES * (s + 1)]
        qrope_ref[:, LANES * s:LANES * (s + 1)] = (_rope_slab(slab, cm, sm, MLA_ROPE // 2) * q_scale).astype(BF16)
    kvh = _dot(_rms(mm(384, 640), kvn_ref[...]).astype(BF16), wukv_ref[...])
    knope_ref[...] = kvh[:, :512].astype(BF16)
    v_ref[...] = kvh[:, 512:].astype(BF16)
    krope_ref[...] = _rope_slab(mm(640, 768), cm, sm, MLA_ROPE // 2).astype(BF16)
    rq = mm(768, 1280)
    rk = mm(1280, 1792)
    for s in range(4):
        sl = slice(LANES * s, LANES * (s + 1))
        rq_ref[:, sl] = _rope_slab(rq[:, sl], cr, sr, RET_DK // 2).astype(BF16)
        rk_ref[:, sl] = (_rope_slab(rk[:, sl], cr, sr, RET_DK // 2) * (RET_DK ** -0.5)).astype(BF16)
    rv_ref[...] = mm(1792, 2816).astype(BF16)
    rg = mm(2816, 3840)
    rg_ref[...] = (rg * (1.0 / (1.0 + jnp.exp(-rg)))).astype(BF16)


def _proj0(h, g, win, qn, wuq, kvn, wukv, cm, sm, cr, sr, seq):
    T, D = h.shape
    tm = ROW_TILE
    nseq = seq // tm
    row = lambda w: pl.BlockSpec((tm, w), lambda i: (i, 0))
    full = lambda a: pl.BlockSpec(a.shape, lambda i: (0, 0))
    pos = pl.BlockSpec((tm, LANES), lambda i: (i % nseq, 0))
    widths = (512, 256, 512, 128, 512, 512, 512, 1024, 1024)
    return pl.pallas_call(
        _proj0_kernel,
        grid=(T // tm,),
        in_specs=[row(D), full(g), full(win), full(qn), full(wuq), full(kvn), full(wukv), pos, pos, pos, pos],
        out_specs=[row(w) for w in widths],
        out_shape=[jax.ShapeDtypeStruct((T, w), BF16) for w in widths],
        compiler_params=_cparams(("parallel",)),
        name="proj0",
    )(h, g, win, qn, wuq, kvn, wukv, cm, sm, cr, sr)


def _mla_kernel(qn_ref, qr_ref, kn_ref, kr_ref, v_ref, o_ref, kk_ref, *, tq):
    S = qn_ref.shape[0]
    p = pl.program_id(1)
    kk_ref[:, :LANES] = kn_ref[...]
    kk_ref[:, LANES:] = kr_ref[...]
    lane = lax.broadcasted_iota(jnp.int32, (1, LANES), 1)
    quarter = (p % 2) * 2
    row = lax.broadcasted_iota(jnp.int32, (tq, tq), 0)
    col = lax.broadcasted_iota(jnp.int32, (tq, tq), 1)
    causal = col <= row
    for i in range(S // tq):
        lo, hi = i * tq, (i + 1) * tq
        qn = qn_ref[lo:hi, :]
        qr = qr_ref[lo:hi, :]
        outs = []
        for hh in range(2):
            mn = (lane // 64 == hh).astype(BF16)
            mr = (lane // 32 == quarter + hh).astype(BF16)
            qa = jnp.concatenate([qn * mn, qr * mr], axis=1)
            sd = jnp.where(causal, _nt_dot(qa, kk_ref[lo:hi, :]), -jnp.inf)
            m = jnp.max(sd, axis=-1, keepdims=True)
            if i > 0:
                so = _nt_dot(qa, kk_ref[:lo, :])
                m = jnp.maximum(m, jnp.max(so, axis=-1, keepdims=True))
            pd = jnp.exp(sd - m)
            l = jnp.sum(pd, axis=-1, keepdims=True)
            acc = _dot(pd.astype(BF16), v_ref[lo:hi, :])
            if i > 0:
                po = jnp.exp(so - m)
                l = l + jnp.sum(po, axis=-1, keepdims=True)
                acc = acc + _dot(po.astype(BF16), v_ref[:lo, :])
            outs.append(acc / l)
        o_ref[lo:hi, :] = jnp.where(lane < 64, outs[0], outs[1]).astype(BF16)


def _mla_attention(qnope, qrope, knope, krope, v, batch, seq):
    T = qnope.shape[0]
    blk = lambda f: pl.BlockSpec((seq, LANES), f)
    return pl.pallas_call(
        functools.partial(_mla_kernel, tq=MLA_TQ),
        grid=(batch, MLA_HEADS // 2),
        in_specs=[blk(lambda b, p: (b, p)), blk(lambda b, p: (b, p // 2)), blk(lambda b, p: (b, p)),
                  blk(lambda b, p: (b, 0)), blk(lambda b, p: (b, p))],
        out_specs=blk(lambda b, p: (b, p)),
        out_shape=jax.ShapeDtypeStruct((T, MLA_HEADS * MLA_V), BF16),
        scratch_shapes=[pltpu.VMEM((seq, 2 * LANES), BF16)],
        compiler_params=_cparams(("parallel", "parallel")),
        name="mla_attention",
    )(qnope, qrope, knope, krope, v)


def _ret_kernel(q_ref, k_ref, v_ref, g_ref, gn_ref, din_ref, qd_ref, kd_ref, cd_ref, o_ref):
    S = q_ref.shape[0]
    h = pl.program_id(1)
    lane = lax.broadcasted_iota(jnp.int32, (1, LANES), 1)
    qmask = (lane // RET_DK == h % 2).astype(F32)
    din = din_ref[0]
    qd = qd_ref[0] * qmask
    kd = kd_ref[0]
    cd = cd_ref[0]
    gain = gn_ref[...]
    rows = [slice(c * CHUNK, (c + 1) * CHUNK) for c in range(S // CHUNK)]
    kvs = [_dot((k_ref[r, :].astype(F32) * kd).T.astype(BF16), v_ref[r, :]) for r in rows[:-1]]
    states = [None]
    for kv in kvs:
        states.append(kv if states[-1] is None else states[-1] * cd + kv)
    for r, state in zip(rows, states):
        q = q_ref[r, :].astype(F32)
        inner = _nt_dot((q * qmask).astype(BF16), k_ref[r, :]) * din
        y = _dot(inner.astype(BF16), v_ref[r, :])
        if state is not None:
            y = y + _dot((q * qd).astype(BF16), state.astype(BF16))
        mu = jnp.mean(y, axis=-1, keepdims=True)
        var = jnp.mean(jnp.square(y - mu), axis=-1, keepdims=True)
        yn = (y - mu) * lax.rsqrt(var + EPS) * gain
        o_ref[r, :] = (g_ref[r, :].astype(F32) * yn).astype(BF16)


def _retention(rq, rk, rv, rg, gn, din, qd, kd, cd, batch, seq):
    T = rq.shape[0]
    blk = lambda f: pl.BlockSpec((seq, LANES), f)
    tab = pl.BlockSpec((1, CHUNK, LANES), lambda b, h: (h, 0, 0))
    return pl.pallas_call(
        _ret_kernel,
        grid=(batch, RET_HEADS),
        in_specs=[blk(lambda b, h: (b, h // 2)), blk(lambda b, h: (b, h // 2)), blk(lambda b, h: (b, h)),
                  blk(lambda b, h: (b, h)), pl.BlockSpec((1, LANES), lambda b, h: (0, h)), tab, tab, tab,
                  pl.BlockSpec((1, 1, LANES), lambda b, h: (h, 0, 0))],
        out_specs=blk(lambda b, h: (b, h)),
        out_shape=jax.ShapeDtypeStruct((T, RET_HEADS * RET_DV), BF16),
        compiler_params=_cparams(("parallel", "parallel")),
        name="retention",
    )(rq, rk, rv, rg, gn, din, qd, kd, cd)


def _ffn_kernel(h_ref, a_ref, b_ref, wa_ref, wb_ref, g_ref, wg_ref, wu_ref, wd_ref, o_ref, xn_ref, acc_ref):
    j = pl.program_id(1)

    @pl.when(j == 0)
    def _():
        tm = h_ref.shape[0]
        for rows in (slice(0, tm // 2), slice(tm // 2, tm)):
            hn = h_ref[rows, :] + _dot(a_ref[rows, :], wa_ref[...]) + _dot(b_ref[rows, :], wb_ref[...])
            acc_ref[rows, :] = hn
            xn_ref[rows, :] = _rms(hn, g_ref[...]).astype(BF16)

    xn = xn_ref[...]
    gate = _dot(xn, wg_ref[...])
    act = gate * (1.0 / (1.0 + jnp.exp(-gate))) * _dot(xn, wu_ref[...])
    acc_ref[...] += _dot(act.astype(BF16), wd_ref[...])

    @pl.when(j == pl.num_programs(1) - 1)
    def _():
        o_ref[...] = acc_ref[...]


def _mixer_out_ffn(h, a, b, wa, wb, g, wg, wu, wd):
    T, D = h.shape
    F = wg.shape[1]
    tm, nf = ROW_TILE, FF_STEPS
    tf = F // nf
    row = lambda w: pl.BlockSpec((tm, w), lambda i, j: (i, 0))
    full = lambda x: pl.BlockSpec(x.shape, lambda i, j: (0, 0))
    return pl.pallas_call(
        _ffn_kernel,
        grid=(T // tm, nf),
        in_specs=[row(D), row(a.shape[1]), row(b.shape[1]), full(wa), full(wb), full(g),
                  pl.BlockSpec((D, tf), lambda i, j: (0, j)), pl.BlockSpec((D, tf), lambda i, j: (0, j)),
                  pl.BlockSpec((tf, D), lambda i, j: (j, 0))],
        out_specs=row(D),
        out_shape=jax.ShapeDtypeStruct((T, D), F32),
        scratch_shapes=[pltpu.VMEM((tm, D), BF16), pltpu.VMEM((tm, D), F32)],
        compiler_params=_cparams(("parallel", "arbitrary")),
        name="mixer_out_ffn",
    )(h, a, b, wa, wb, g, wg, wu, wd)


def _proj1_kernel(h_ref, g_ref, win_ref, sq_ref, sk_ref, sv_ref, bq_ref, bk_ref, bv_ref):
    xn = _rms(h_ref[...], g_ref[...]).astype(BF16)

    def mm(lo, hi):
        return _dot(xn, win_ref[:, lo:hi])

    sq_ref[...] = (mm(0, 1024) * (SWA_DIM ** -0.5)).astype(BF16)
    sk_ref[...] = mm(1024, 1152).astype(BF16)
    sv_ref[...] = mm(1152, 1280).astype(BF16)
    bq_ref[...] = (mm(1280, 1792) * (SB_DIM ** -0.5)).astype(BF16)
    bk_ref[...] = mm(1792, 2304).astype(BF16)
    bv_ref[...] = mm(2304, 2816).astype(BF16)


def _proj1(h, g, win):
    T, D = h.shape
    tm = ROW_TILE
    row = lambda w: pl.BlockSpec((tm, w), lambda i: (i, 0))
    full = lambda a: pl.BlockSpec(a.shape, lambda i: (0, 0))
    widths = (1024, 128, 128, 512, 512, 512)
    return pl.pallas_call(
        _proj1_kernel,
        grid=(T // tm,),
        in_specs=[row(D), full(g), full(win)],
        out_specs=[row(w) for w in widths],
        out_shape=[jax.ShapeDtypeStruct((T, w), BF16) for w in widths],
        compiler_params=_cparams(("parallel",)),
        name="proj1",
    )(h, g, win)


def _swa_kernel(tab_ref, q_ref, k_ref, v_ref, o_ref):
    S = q_ref.shape[0]
    W = WINDOW
    j = pl.program_id(1)
    lane = lax.broadcasted_iota(jnp.int32, (1, LANES), 1)
    dist2 = (lax.broadcasted_iota(jnp.int32, (W, 2 * W), 0) + W) - lax.broadcasted_iota(jnp.int32, (W, 2 * W), 1)
    dist1 = dist2[:, W:]
    heads = []
    for hh in range(2):
        head = j + (SWA_HEADS // 2) * hh
        sink, slope = tab_ref[head], tab_ref[SWA_HEADS + head]
        bias = [jnp.where((d >= 0) & (d < W), slope * d.astype(F32), jnp.inf) for d in (dist1, dist2)]
        heads.append((sink, bias, (lane // 64 == hh).astype(BF16)))
    for i in range(S // W):
        q = q_ref[i * W:(i + 1) * W, :]
        lo = max(i - 1, 0) * W
        k, v = k_ref[lo:(i + 1) * W, :], v_ref[lo:(i + 1) * W, :]
        outs = []
        for sink, bias, mask in heads:
            s = _nt_dot(q * mask, k) - bias[min(i, 1)]
            m = jnp.maximum(jnp.max(s, axis=-1, keepdims=True), sink)
            p = jnp.exp(s - m)
            den = jnp.sum(p, axis=-1, keepdims=True) + jnp.exp(sink - m)
            outs.append(_dot(p.astype(BF16), v) / den)
        o_ref[i * W:(i + 1) * W, :] = jnp.where(lane < 64, outs[0], outs[1]).astype(BF16)


def _swa_attention(tab, sq, sk, sv, batch, seq):
    T = sq.shape[0]
    blk = lambda f: pl.BlockSpec((seq, LANES), f)
    return pl.pallas_call(
        _swa_kernel,
        grid_spec=pltpu.PrefetchScalarGridSpec(
            num_scalar_prefetch=1,
            grid=(batch, SWA_HEADS // 2),
            in_specs=[blk(lambda b, j, t: (b, j)), blk(lambda b, j, t: (b, 0)), blk(lambda b, j, t: (b, 0))],
            out_specs=blk(lambda b, j, t: (b, j)),
        ),
        out_shape=jax.ShapeDtypeStruct((T, SWA_HEADS * SWA_DIM), BF16),
        compiler_params=_cparams(("parallel", "parallel")),
        name="swa_attention",
    )(tab, sq, sk, sv)


def _sb_kernel(q_ref, k_ref, v_ref, o_ref, *, t):
    S = q_ref.shape[0]
    lane = lax.broadcasted_iota(jnp.int32, (1, LANES), 1)
    row = lax.broadcasted_iota(jnp.int32, (t, t), 0)
    col = lax.broadcasted_iota(jnp.int32, (t, t), 1)
    before = col < row
    neg_later = jnp.where(row > col, -1.0, 0.0).astype(BF16)

    def softplus(z):
        return jnp.maximum(z, 0.0) + jnp.log(1.0 + jnp.exp2(jnp.abs(z) * (-LOG2E)))

    for i in range(S // t):
        rows = slice(i * t, (i + 1) * t)
        q = q_ref[rows, :]
        outs = []
        for hh in range(2):
            qm = q * (lane // 64 == hh).astype(BF16)
            z = _nt_dot(qm, k_ref[rows, :])
            sp = jnp.where(before, softplus(z), 0.0)
            a = jnp.where(before, jnp.exp((z - sp) + _dot(sp.astype(BF16), neg_later)), 0.0)
            acc = _dot(a.astype(BF16), v_ref[rows, :])
            run = -jnp.sum(sp, axis=-1, keepdims=True)
            for j in range(i - 1, -1, -1):
                krows = slice(j * t, (j + 1) * t)
                z = _nt_dot(qm, k_ref[krows, :])
                sp = softplus(z)
                a = jnp.exp((z - sp) + (_dot(sp.astype(BF16), neg_later) + run))
                acc = acc + _dot(a.astype(BF16), v_ref[krows, :])
                run = run - jnp.sum(sp, axis=-1, keepdims=True)
            outs.append(acc)
        o_ref[rows, :] = jnp.where(lane < 64, outs[0], outs[1]).astype(BF16)


def _sb_attention(bq, bk, bv, batch, seq):
    T = bq.shape[0]
    blk = pl.BlockSpec((seq, LANES), lambda b, p: (b, p))
    return pl.pallas_call(
        functools.partial(_sb_kernel, t=SB_T),
        grid=(batch, SB_HEADS // 2),
        in_specs=[blk, blk, blk],
        out_specs=blk,
        out_shape=jax.ShapeDtypeStruct((T, SB_HEADS * SB_DIM), BF16),
        compiler_params=_cparams(("parallel", "parallel")),
        name="sb_attention",
    )(bq, bk, bv)


def _split_bf16(x):
    hi = x.astype(BF16)
    return hi, (x - hi.astype(F32)).astype(BF16)


def _out_router_kernel(h_ref, a_ref, b_ref, wa_ref, wb_ref, g_ref, rcat_ref, o_ref, route_ref):
    tm = h_ref.shape[0]
    for rows in (slice(0, tm // 2), slice(tm // 2, tm)):
        hn = h_ref[rows, :] + _dot(a_ref[rows, :], wa_ref[...]) + _dot(b_ref[rows, :], wb_ref[...])
        o_ref[rows, :] = hn
        xhi, xlo = _split_bf16(_rms(hn, g_ref[...]))
        both = _dot(xhi, rcat_ref[...])
        logits = both[:, :LANES] + (both[:, LANES:] + _dot(xlo, rcat_ref[:, :LANES]))
        lane = lax.broadcasted_iota(jnp.int32, logits.shape, 1)
        logits = jnp.where(lane < N_EXPERTS, logits, -jnp.inf)
        m1 = jnp.max(logits, axis=-1, keepdims=True)
        i1 = jnp.min(jnp.where(logits == m1, lane, LANES), axis=-1, keepdims=True)
        rest = jnp.where(lane == i1, -jnp.inf, logits)
        m2 = jnp.max(rest, axis=-1, keepdims=True)
        i2 = jnp.min(jnp.where(rest == m2, lane, LANES), axis=-1, keepdims=True)
        e2 = jnp.exp(m2 - m1)
        den = 1.0 + e2
        route = jnp.where(lane == 0, i1.astype(F32), 0.0)
        route = jnp.where(lane == 1, i2.astype(F32), route)
        route = jnp.where(lane == 2, 1.0 / den, route)
        route = jnp.where(lane == 3, e2 / den, route)
        route_ref[rows, :] = route


def _mixer_out_router(h, a, b, wa, wb, g, rcat):
    T, D = h.shape
    tm = ROW_TILE
    row = lambda w: pl.BlockSpec((tm, w), lambda i: (i, 0))
    full = lambda x: pl.BlockSpec(x.shape, lambda i: (0, 0))
    return pl.pallas_call(
        _out_router_kernel,
        grid=(T // tm,),
        in_specs=[row(D), row(a.shape[1]), row(b.shape[1]), full(wa), full(wb), full(g), full(rcat)],
        out_specs=[row(D), row(LANES)],
        out_shape=[jax.ShapeDtypeStruct((T, D), F32), jax.ShapeDtypeStruct((T, LANES), F32)],
        compiler_params=_cparams(("parallel",)),
        name="mixer_out_router",
    )(h, a, b, wa, wb, g, rcat)


def _gather_rows(idx_ref, src_hbm, dst_ref, sem, base, n):
    for r in range(n):
        pltpu.make_async_copy(src_hbm.at[pl.ds(idx_ref[0, 0, base + r], 1)], dst_ref.at[pl.ds(base + r, 1)], sem).start()


def _wait_rows(src_hbm, dst_ref, sem):
    pltpu.make_async_copy(src_hbm.at[pl.ds(0, dst_ref.shape[0])], dst_ref, sem).wait()


def _cast_kernel(w_ref, o_ref):
    o_ref[...] = w_ref[...].astype(BF16)


def _to_bf16(w):
    E, K, N = w.shape
    blk = pl.BlockSpec((1, CAST_ROWS, N), lambda e, k: (e, k, 0))
    return pl.pallas_call(
        _cast_kernel,
        grid=(E, K // CAST_ROWS),
        in_specs=[blk],
        out_specs=blk,
        out_shape=jax.ShapeDtypeStruct(w.shape, BF16),
        compiler_params=_cparams(("parallel", "parallel")),
        name="cast_bf16",
    )(w)


def _dispatch_kernel(meta_ref, dest_ref, wg_ref, wu_ref, wd_ref, h_ref, og_ref, ou_ref, od_ref, xg_hbm,
                     zero_ref, sem, zsem):
    step = pl.program_id(0) * pl.num_programs(1) + pl.program_id(1)
    tpb = h_ref.shape[0]
    R = zero_ref.shape[0]

    def zero_block(start):
        return pltpu.make_async_copy(zero_ref, xg_hbm.at[pl.ds(pl.multiple_of(start, R), R)], zsem)

    cap = xg_hbm.shape[0]
    fills = [(meta_ref[N_EXPERTS + e] > 0, meta_ref[e] - R) for e in range(N_EXPERTS)]
    fills += [(meta_ref[N_EXPERTS - 1] + b * R < cap, meta_ref[N_EXPERTS - 1] + b * R) for b in range(N_EXPERTS)]

    @pl.when(step == 0)
    def _():
        zero_ref[...] = jnp.zeros_like(zero_ref)
        for do_start in (True, False):
            for cond, start in fills:
                @pl.when(cond)
                def _():
                    zero_block(start).start() if do_start else zero_block(start).wait()

    def scatter_group(gi, carry):
        base = gi * DISPATCH_UNROLL
        rows = [[dest_ref[0, 0, k * tpb + base + u] for k in range(TOP_K)]
                for u in range(DISPATCH_UNROLL)]
        for u in range(DISPATCH_UNROLL):
            for k in range(TOP_K):
                pltpu.make_async_copy(h_ref.at[pl.ds(base + u, 1)], xg_hbm.at[pl.ds(rows[u][k], 1)], sem).start()
        return carry

    lax.fori_loop(0, tpb // DISPATCH_UNROLL, scatter_group, 0)
    og_ref[...] = wg_ref[...].astype(BF16)
    ou_ref[...] = wu_ref[...].astype(BF16)
    od_ref[...] = wd_ref[...].astype(BF16)
    for k in range(TOP_K):
        pltpu.make_async_copy(h_ref, xg_hbm.at[pl.ds(0, tpb)], sem).wait()


def _dispatch_cast(meta, dest, wg, wu, wd, h, cap):
    E, D, F = wg.shape
    nsteps, _, n = dest.shape
    nk = nsteps // E
    spec = lambda w: pl.BlockSpec((1, w.shape[1] // nk, w.shape[2]), lambda e, k, m: (e, k, 0))
    return pl.pallas_call(
        _dispatch_kernel,
        grid_spec=pltpu.PrefetchScalarGridSpec(
            num_scalar_prefetch=1,
            grid=(E, nk),
            in_specs=[pl.BlockSpec((1, 1, n), lambda e, k, m: (e * nk + k, 0, 0), memory_space=pltpu.SMEM),
                      spec(wg), spec(wu), spec(wd),
                      pl.BlockSpec((n // TOP_K, D), lambda e, k, m: (e * nk + k, 0))],
            out_specs=[spec(wg), spec(wu), spec(wd), pl.BlockSpec(memory_space=pl.ANY)],
            scratch_shapes=[pltpu.VMEM((MOE_ROWS, D), F32), pltpu.SemaphoreType.DMA, pltpu.SemaphoreType.DMA],
        ),
        out_shape=[jax.ShapeDtypeStruct(wg.shape, BF16), jax.ShapeDtypeStruct(wu.shape, BF16),
                   jax.ShapeDtypeStruct(wd.shape, BF16), jax.ShapeDtypeStruct((cap, D), F32)],
        compiler_params=_cparams(("arbitrary", "arbitrary")),
        name="dispatch_cast",
    )(meta, dest, wg, wu, wd, h)


def _moe_kernel(sched_ref, x_ref, g_ref, wg_ref, wu_ref, wd_ref, o_ref, xn_ref, acc_ref):
    i, j = pl.program_id(0), pl.program_id(1)
    nblk = pl.num_programs(0)
    used = i < sched_ref[nblk]

    @pl.when(j == 0)
    def _():
        xn_ref[...] = _rms(x_ref[...], g_ref[...]).astype(BF16)
        acc_ref[...] = jnp.zeros_like(acc_ref)

    @pl.when(used)
    def _():
        xn = xn_ref[...]
        gate = _dot(xn, wg_ref[0])
        act = gate * (1.0 / (1.0 + jnp.exp(-gate))) * _dot(xn, wu_ref[0])
        acc_ref[...] += _dot(act.astype(BF16), wd_ref[0])

    @pl.when(j == pl.num_programs(1) - 1)
    def _():
        o_ref[...] = acc_ref[...]


def _moe_experts(sched, xg, g, wg, wu, wd):
    cap, D = xg.shape
    F = wg.shape[2]
    R, nf = MOE_ROWS, MOE_FF_STEPS
    nblk, tf = cap // R, F // nf

    def blk_i(i, s):
        return jnp.minimum(i, s[nblk] - 1)

    def col_j(i, j, s):
        return jnp.where(i < s[nblk], j, nf - 1)

    return pl.pallas_call(
        _moe_kernel,
        grid_spec=pltpu.PrefetchScalarGridSpec(
            num_scalar_prefetch=1,
            grid=(nblk, nf),
            in_specs=[pl.BlockSpec((R, D), lambda i, j, s: (blk_i(i, s), 0)),
                      pl.BlockSpec(g.shape, lambda i, j, s: (0, 0)),
                      pl.BlockSpec((1, D, tf), lambda i, j, s: (s[blk_i(i, s)], 0, col_j(i, j, s))),
                      pl.BlockSpec((1, D, tf), lambda i, j, s: (s[blk_i(i, s)], 0, col_j(i, j, s))),
                      pl.BlockSpec((1, tf, D), lambda i, j, s: (s[blk_i(i, s)], col_j(i, j, s), 0))],
            out_specs=pl.BlockSpec((R, D), lambda i, j, s: (i, 0)),
            scratch_shapes=[pltpu.VMEM((R, D), BF16), pltpu.VMEM((R, D), F32)],
        ),
        out_shape=jax.ShapeDtypeStruct((cap, D), F32),
        compiler_params=_cparams(("arbitrary", "arbitrary")),
        name="moe_experts",
    )(sched, xg, g, wg, wu, wd)


def _combine_kernel(pos_ref, nxt_ref, h_ref, route_ref, g_ref, y_hbm, o_ref, ybuf, sem):
    i = pl.program_id(0)
    n = ybuf.shape[1]
    slot = i % 2

    @pl.when(i == 0)
    def _():
        _gather_rows(pos_ref, y_hbm, ybuf.at[0], sem.at[0], 0, n)

    _wait_rows(y_hbm, ybuf.at[slot], sem.at[slot])
    _gather_rows(nxt_ref, y_hbm, ybuf.at[1 - slot], sem.at[1 - slot], 0, n)
    tm = n // TOP_K
    route = route_ref[...]
    out = h_ref[...] + (ybuf[slot, :tm, :] * route[:, 2:3] + ybuf[slot, tm:, :] * route[:, 3:4])
    o_ref[...] = _rms(out, g_ref[...])

    @pl.when(i == pl.num_programs(0) - 1)
    def _():
        _wait_rows(y_hbm, ybuf.at[1 - slot], sem.at[1 - slot])


def _combine_norm(pos, h, route, g, yg):
    T, D = h.shape
    ntile, _, n = pos.shape
    tm = n // TOP_K
    row = lambda w: pl.BlockSpec((tm, w), lambda i: (i, 0))
    smem = lambda f: pl.BlockSpec((1, 1, n), f, memory_space=pltpu.SMEM)
    return pl.pallas_call(
        _combine_kernel,
        grid=(ntile,),
        in_specs=[smem(lambda i: (i, 0, 0)), smem(lambda i: (jnp.minimum(i + 1, ntile - 1), 0, 0)),
                  row(D), row(LANES), pl.BlockSpec(g.shape, lambda i: (0, 0)), pl.BlockSpec(memory_space=pl.ANY)],
        out_specs=row(D),
        out_shape=jax.ShapeDtypeStruct((T, D), F32),
        scratch_shapes=[pltpu.VMEM((2, n, D), F32), pltpu.SemaphoreType.DMA((2,))],
        compiler_params=_cparams(("arbitrary",)),
        name="combine_norm",
    )(pos, pos, h, route, g, yg)


def _rope_tables(seq, half):
    inv_freq = ROPE_BASE ** (-jnp.arange(half, dtype=F32) / half)
    ang = jnp.arange(seq).astype(F32)[:, None] * inv_freq[None, :]
    cos, sin = jnp.cos(ang), jnp.sin(ang)
    reps = LANES // (2 * half)
    return jnp.tile(jnp.concatenate([cos, cos], 1), (1, reps)), jnp.tile(jnp.concatenate([-sin, sin], 1), (1, reps))


def _retention_tables():
    log_gamma = jnp.log1p(-(2.0 ** (-5.0 - jnp.arange(RET_HEADS, dtype=F32))))
    idx = jnp.arange(CHUNK, dtype=F32)
    rel = idx[:, None] - idx[None, :]
    din = jnp.where(rel >= 0, jnp.exp(log_gamma[:, None, None] * jnp.maximum(rel, 0.0)), 0.0)
    qd = jnp.exp(log_gamma[:, None] * (idx + 1.0))
    kd = jnp.exp(log_gamma[:, None] * (CHUNK - 1.0 - idx))
    cd = jnp.exp(log_gamma * CHUNK)
    bc = lambda a: jnp.broadcast_to(a[:, :, None], (RET_HEADS, CHUNK, LANES))
    return din, bc(qd), bc(kd), jnp.broadcast_to(cd[:, None, None], (RET_HEADS, 1, LANES))


def kernel(x, ab_norm, ab_w_in, mla_q_norm, mla_w_uq, mla_kv_norm, mla_w_ukv, ret_gn, ab_w_out, ffn_norm,
           ffn_w_gate, ffn_w_up, ffn_w_down, cd_norm, cd_w_in, swa_sinks, cd_w_out, moe_norm, moe_router,
           moe_w_gate, moe_w_up, moe_w_down, final_norm):
    B, S, D = x.shape
    T = B * S
    h = x.reshape(T, D)
    row = lambda g: g.reshape(1, -1)

    w = ab_w_in[0]
    win0 = jnp.concatenate([w[:, :640], jnp.tile(w[:, 640:672], (1, 4)), w[:, 672:]], axis=1).astype(BF16)
    wuq = mla_w_uq[0].reshape(MLA_Q_RANK, MLA_HEADS, MLA_NOPE + MLA_ROPE)
    wuq = jnp.concatenate([wuq[:, :, :MLA_NOPE].reshape(MLA_Q_RANK, -1),
                           wuq[:, :, MLA_NOPE:].reshape(MLA_Q_RANK, -1)], axis=1).astype(BF16)
    wukv = mla_w_ukv[0].reshape(MLA_KV_RANK, MLA_HEADS, MLA_NOPE + MLA_V)
    wukv = jnp.concatenate([wukv[:, :, :MLA_NOPE].reshape(MLA_KV_RANK, -1),
                            wukv[:, :, MLA_NOPE:].reshape(MLA_KV_RANK, -1)], axis=1).astype(BF16)
    cm, sm = _rope_tables(S, MLA_ROPE // 2)
    cr, sr = _rope_tables(S, RET_DK // 2)
    qnope, qrope, knope, krope, v, rq, rk, rv, rg = _proj0(
        h, row(ab_norm[0]), win0, row(mla_q_norm[0]), wuq, row(mla_kv_norm[0]), wukv, cm, sm, cr, sr, S)
    mla = _mla_attention(qnope, qrope, knope, krope, v, B, S)
    ret = _retention(rq, rk, rv, rg, row(ret_gn[0]), *_retention_tables(), B, S)
    wo = ab_w_out[0].astype(BF16)
    nm = MLA_HEADS * MLA_V
    h = _mixer_out_ffn(h, mla, ret, wo[:nm], wo[nm:], row(ffn_norm[0]), _to_bf16(ffn_w_gate)[0],
                       _to_bf16(ffn_w_up)[0], _to_bf16(ffn_w_down)[0])

    w = cd_w_in[0]
    nq = SWA_HEADS * SWA_DIM
    pair_order = np.stack([np.arange(SWA_HEADS // 2), np.arange(SWA_HEADS // 2) + SWA_HEADS // 2], 1).reshape(-1)
    cols = (pair_order[:, None] * SWA_DIM + np.arange(SWA_DIM)[None, :]).reshape(-1)
    win1 = jnp.concatenate([w[:, :nq][:, cols], w[:, nq:]], axis=1).astype(BF16)
    sq, sk, sv, bq, bk, bv = _proj1(h, row(cd_norm[0]), win1)
    slopes = 2.0 ** (-8.0 * jnp.arange(1, SWA_HEADS + 1, dtype=F32) / SWA_HEADS)
    swa = _swa_attention(jnp.concatenate([swa_sinks[0].astype(F32), slopes]), sq, sk, sv, B, S)
    sb = _sb_attention(bq, bk, bv, B, S)
    wo = cd_w_out[0]
    wo_swa = wo[:nq][cols].astype(BF16)
    router = jnp.pad(moe_router[0], ((0, 0), (0, LANES - N_EXPERTS)))
    rhi = router.astype(BF16)
    rlo = (router - rhi.astype(F32)).astype(BF16)
    h, route = _mixer_out_router(h, swa, sb, wo_swa, wo[nq:].astype(BF16), row(moe_norm[0]),
                                 jnp.concatenate([rhi, rlo], axis=1))

    R = MOE_ROWS
    TK = T * TOP_K
    expert = route[:, :TOP_K].astype(jnp.int32).reshape(-1)
    onehot = (expert[:, None] == jnp.arange(N_EXPERTS, dtype=jnp.int32)[None, :]).astype(jnp.int32)
    csum = jnp.cumsum(onehot, axis=0)
    counts = csum[-1]
    rank = jnp.sum((csum - 1) * onehot, axis=1)
    padded = ((counts + R - 1) // R) * R
    pad_end = jnp.cumsum(padded)
    pad_start = pad_end - padded
    dest = pad_start[expert] + rank
    cap = -(-TK // R) * R + N_EXPERTS * R
    nblk = cap // R
    blk_exp = jnp.minimum(jnp.searchsorted(pad_end, jnp.arange(nblk, dtype=jnp.int32) * R, side='right'),
                          N_EXPERTS - 1).astype(jnp.int32)
    sched = jnp.concatenate([blk_exp, (pad_end[-1:] // R).astype(jnp.int32)])

    def tiles(tm):
        return dest.reshape(T // tm, tm, TOP_K).transpose(0, 2, 1).reshape(T // tm, 1, TOP_K * tm)

    meta = jnp.concatenate([pad_end, padded]).astype(jnp.int32)
    wg, wu, wd, xg = _dispatch_cast(meta, tiles(T // DISPATCH_STEPS), moe_w_gate[0], moe_w_up[0], moe_w_down[0],
                                    h, cap)
    yg = _moe_experts(sched, xg, row(moe_norm[0]), wg, wu, wd)
    pos = tiles(COMBINE_ROWS)
    out = _combine_norm(pos, h, route, row(final_norm), yg)
    return out.reshape(B, S, D)
```

```python
import functools

import jax
import jax.numpy as jnp
import numpy as np
from jax import lax
from jax.experimental import pallas as pl
from jax.experimental.pallas import tpu as pltpu

F32 = jnp.float32
BF16 = jnp.bfloat16

LANES = 128
EPS = 1e-6
LOG2E = 1.4426950408889634
ROPE_BASE = 10000.0
CHUNK = 128
MLA_HEADS, MLA_NOPE, MLA_ROPE, MLA_V = 8, 64, 32, 64
MLA_Q_RANK, MLA_KV_RANK = 384, 256
RET_HEADS, RET_DK, RET_DV = 8, 64, 128
SWA_HEADS, SWA_KV_HEADS, SWA_DIM, WINDOW = 16, 2, 64, 128
SB_HEADS, SB_DIM = 8, 64
N_EXPERTS, TOP_K = 8, 2

VMEM_LIMIT = 56 * 1024 * 1024
ROW_TILE = 512
FF_STEPS = 2
MOE_FF_STEPS = 2
CAST_ROWS = 512
COMBINE_ROWS = 256
COMBINE_GROUP = 32
RANK_ROWS = 512
DISPATCH_STEPS = 32
DISPATCH_UNROLL = 16
MOE_ROWS = 512
MLA_TQ = 512
SB_T = 256


def _cparams(sem):
    return pltpu.CompilerParams(dimension_semantics=sem, vmem_limit_bytes=VMEM_LIMIT)


def _rms(x, g):
    return x * lax.rsqrt(jnp.mean(x * x, axis=-1, keepdims=True) + EPS) * g


def _rope_slab(x, cos, sin_signed, half):
    lane = lax.broadcasted_iota(jnp.int32, x.shape, 1)
    first = (lane % (2 * half)) < half
    partner = jnp.where(first, pltpu.roll(x, LANES - half, 1), pltpu.roll(x, half, 1))
    return x * cos + partner * sin_signed


def _nt_dot(a, b):
    return lax.dot_general(a, b, (((1,), (1,)), ((), ())), preferred_element_type=F32)


def _dot(a, b):
    return jnp.dot(a, b, preferred_element_type=F32)


def _proj0_kernel(h_ref, g_ref, win_ref, qn_ref, wuq_ref, kvn_ref, wukv_ref, cm_ref, sm_ref, cr_ref, sr_ref,
                  qnope_ref, qrope_ref, knope_ref, krope_ref, v_ref, rq_ref, rk_ref, rv_ref, rg_ref):
    xn = _rms(h_ref[...], g_ref[...]).astype(BF16)

    def mm(lo, hi):
        return _dot(xn, win_ref[:, lo:hi])

    cm, sm, cr, sr = cm_ref[...], sm_ref[...], cr_ref[...], sr_ref[...]
    q_scale = (MLA_NOPE + MLA_ROPE) ** -0.5
    qh = _dot(_rms(mm(0, 384), qn_ref[...]).astype(BF16), wuq_ref[...])
    qnope_ref[...] = (qh[:, :512] * q_scale).astype(BF16)
    for s in range(2):
        slab = qh[:, 512 + LANES * s:512 + LANES * (s + 1)]
        qrope_ref[:, LANES * s:LANES * (s + 1)] = (_rope_slab(slab, cm, sm, MLA_ROPE // 2) * q_scale).astype(BF16)
    kvh = _dot(_rms(mm(384, 640), kvn_ref[...]).astype(BF16), wukv_ref[...])
    knope_ref[...] = kvh[:, :512].astype(BF16)
    v_ref[...] = kvh[:, 512:].astype(BF16)
    krope_ref[...] = _rope_slab(mm(640, 768), cm, sm, MLA_ROPE // 2).astype(BF16)
    rq = mm(768, 1280)
    rk = mm(1280, 1792)
    for s in range(4):
        sl = slice(LANES * s, LANES * (s + 1))
        rq_ref[:, sl] = _rope_slab(rq[:, sl], cr, sr, RET_DK // 2).astype(BF16)
        rk_ref[:, sl] = (_rope_slab(rk[:, sl], cr, sr, RET_DK // 2) * (RET_DK ** -0.5)).astype(BF16)
    rv_ref[...] = mm(1792, 2816).astype(BF16)
    rg = mm(2816, 3840)
    rg_ref[...] = (rg * (1.0 / (1.0 + jnp.exp(-rg)))).astype(BF16)


def _proj0(h, g, win, qn, wuq, kvn, wukv, cm, sm, cr, sr, seq):
    T, D = h.shape
    tm = ROW_TILE
    nseq = seq // tm
    row = lambda w: pl.BlockSpec((tm, w), lambda i: (i, 0))
    full = lambda a: pl.BlockSpec(a.shape, lambda i: (0, 0))
    pos = pl.BlockSpec((tm, LANES), lambda i: (i % nseq, 0))
    widths = (512, 256, 512, 128, 512, 512, 512, 1024, 1024)
    return pl.pallas_call(
        _proj0_kernel,
        grid=(T // tm,),
        in_specs=[row(D), full(g), full(win), full(qn), full(wuq), full(kvn), full(wukv), pos, pos, pos, pos],
        out_specs=[row(w) for w in widths],
        out_shape=[jax.ShapeDtypeStruct((T, w), BF16) for w in widths],
        compiler_params=_cparams(("parallel",)),
        name="proj0",
    )(h, g, win, qn, wuq, kvn, wukv, cm, sm, cr, sr)


def _mla_kernel(qn_ref, qr_ref, kn_ref, kr_ref, v_ref, o_ref, kk_ref, *, tq):
    S = qn_ref.shape[0]
    p = pl.program_id(1)
    kk_ref[:, :LANES] = kn_ref[...]
    kk_ref[:, LANES:] = kr_ref[...]
    lane = lax.broadcasted_iota(jnp.int32, (1, LANES), 1)
    quarter = (p % 2) * 2
    row = lax.broadcasted_iota(jnp.int32, (tq, tq), 0)
    col = lax.broadcasted_iota(jnp.int32, (tq, tq), 1)
    causal = col <= row
    for i in range(S // tq):
        lo, hi = i * tq, (i + 1) * tq
        qn = qn_ref[lo:hi, :]
        qr = qr_ref[lo:hi, :]
        outs = []
        for hh in range(2):
            mn = (lane // 64 == hh).astype(BF16)
            mr = (lane // 32 == quarter + hh).astype(BF16)
            qa = jnp.concatenate([qn * mn, qr * mr], axis=1)
            sd = jnp.where(causal, _nt_dot(qa, kk_ref[lo:hi, :]), -jnp.inf)
            m = jnp.max(sd, axis=-1, keepdims=True)
            if i > 0:
                so = _nt_dot(qa, kk_ref[:lo, :])
                m = jnp.maximum(m, jnp.max(so, axis=-1, keepdims=True))
            pd = jnp.exp(sd - m)
            l = jnp.sum(pd, axis=-1, keepdims=True)
            acc = _dot(pd.astype(BF16), v_ref[lo:hi, :])
            if i > 0:
                po = jnp.exp(so - m)
                l = l + jnp.sum(po, axis=-1, keepdims=True)
                acc = acc + _dot(po.astype(BF16), v_ref[:lo, :])
            outs.append(acc / l)
        o_ref[lo:hi, :] = jnp.where(lane < 64, outs[0], outs[1]).astype(BF16)


def _mla_attention(qnope, qrope, knope, krope, v, batch, seq):
    T = qnope.shape[0]
    blk = lambda f: pl.BlockSpec((seq, LANES), f)
    return pl.pallas_call(
        functools.partial(_mla_kernel, tq=MLA_TQ),
        grid=(batch, MLA_HEADS // 2),
        in_specs=[blk(lambda b, p: (b, p)), blk(lambda b, p: (b, p // 2)), blk(lambda b, p: (b, p)),
                  blk(lambda b, p: (b, 0)), blk(lambda b, p: (b, p))],
        out_specs=blk(lambda b, p: (b, p)),
        out_shape=jax.ShapeDtypeStruct((T, MLA_HEADS * MLA_V), BF16),
        scratch_shapes=[pltpu.VMEM((seq, 2 * LANES), BF16)],
        compiler_params=_cparams(("parallel", "parallel")),
        name="mla_attention",
    )(qnope, qrope, knope, krope, v)


def _ret_kernel(q_ref, k_ref, v_ref, g_ref, gn_ref, din_ref, qd_ref, kd_ref, cd_ref, o_ref):
    S = q_ref.shape[0]
    h = pl.program_id(1)
    lane = lax.broadcasted_iota(jnp.int32, (1, LANES), 1)
    qmask = (lane // RET_DK == h % 2).astype(F32)
    din = din_ref[0]
    qd = qd_ref[0] * qmask
    kd = kd_ref[0]
    cd = cd_ref[0]
    gain = gn_ref[...]
    rows = [slice(c * CHUNK, (c + 1) * CHUNK) for c in range(S // CHUNK)]
    kvs = [_dot((k_ref[r, :].astype(F32) * kd).T.astype(BF16), v_ref[r, :]) for r in rows[:-1]]
    states = [None]
    for kv in kvs:
        states.append(kv if states[-1] is None else states[-1] * cd + kv)
    for r, state in zip(rows, states):
        q = q_ref[r, :].astype(F32)
        inner = _nt_dot((q * qmask).astype(BF16), k_ref[r, :]) * din
        y = _dot(inner.astype(BF16), v_ref[r, :])
        if state is not None:
            y = y + _dot((q * qd).astype(BF16), state.astype(BF16))
        mu = jnp.mean(y, axis=-1, keepdims=True)
        var = jnp.mean(jnp.square(y - mu), axis=-1, keepdims=True)
        yn = (y - mu) * lax.rsqrt(var + EPS) * gain
        o_ref[r, :] = (g_ref[r, :].astype(F32) * yn).astype(BF16)


def _retention(rq, rk, rv, rg, gn, din, qd, kd, cd, batch, seq):
    T = rq.shape[0]
    blk = lambda f: pl.BlockSpec((seq, LANES), f)
    tab = pl.BlockSpec((1, CHUNK, LANES), lambda b, h: (h, 0, 0))
    return pl.pallas_call(
        _ret_kernel,
        grid=(batch, RET_HEADS),
        in_specs=[blk(lambda b, h: (b, h // 2)), blk(lambda b, h: (b, h // 2)), blk(lambda b, h: (b, h)),
                  blk(lambda b, h: (b, h)), pl.BlockSpec((1, LANES), lambda b, h: (0, h)), tab, tab, tab,
                  pl.BlockSpec((1, 1, LANES), lambda b, h: (h, 0, 0))],
        out_specs=blk(lambda b, h: (b, h)),
        out_shape=jax.ShapeDtypeStruct((T, RET_HEADS * RET_DV), BF16),
        compiler_params=_cparams(("parallel", "parallel")),
        name="retention",
    )(rq, rk, rv, rg, gn, din, qd, kd, cd)


def _ffn_kernel(h_ref, a_ref, b_ref, wa_ref, wb_ref, g_ref, wg_ref, wu_ref, wd_ref, o_ref, xn_ref, acc_ref):
    j = pl.program_id(1)

    @pl.when(j == 0)
    def _():
        tm = h_ref.shape[0]
        for rows in (slice(0, tm // 2), slice(tm // 2, tm)):
            hn = h_ref[rows, :] + _dot(a_ref[rows, :], wa_ref[...]) + _dot(b_ref[rows, :], wb_ref[...])
            acc_ref[rows, :] = hn
            xn_ref[rows, :] = _rms(hn, g_ref[...]).astype(BF16)

    xn = xn_ref[...]
    gate = _dot(xn, wg_ref[...])
    act = gate * (1.0 / (1.0 + jnp.exp(-gate))) * _dot(xn, wu_ref[...])
    acc_ref[...] += _dot(act.astype(BF16), wd_ref[...])

    @pl.when(j == pl.num_programs(1) - 1)
    def _():
        o_ref[...] = acc_ref[...]


def _mixer_out_ffn(h, a, b, wa, wb, g, wg, wu, wd):
    T, D = h.shape
    F = wg.shape[1]
    tm, nf = ROW_TILE, FF_STEPS
    tf = F // nf
    row = lambda w: pl.BlockSpec((tm, w), lambda i, j: (i, 0))
    full = lambda x: pl.BlockSpec(x.shape, lambda i, j: (0, 0))
    return pl.pallas_call(
        _ffn_kernel,
        grid=(T // tm, nf),
        in_specs=[row(D), row(a.shape[1]), row(b.shape[1]), full(wa), full(wb), full(g),
                  pl.BlockSpec((D, tf), lambda i, j: (0, j)), pl.BlockSpec((D, tf), lambda i, j: (0, j)),
                  pl.BlockSpec((tf, D), lambda i, j: (j, 0))],
        out_specs=row(D),
        out_shape=jax.ShapeDtypeStruct((T, D), F32),
        scratch_shapes=[pltpu.VMEM((tm, D), BF16), pltpu.VMEM((tm, D), F32)],
        compiler_params=_cparams(("parallel", "arbitrary")),
        name="mixer_out_ffn",
    )(h, a, b, wa, wb, g, wg, wu, wd)


def _proj1_kernel(h_ref, g_ref, win_ref, sq_ref, sk_ref, sv_ref, bq_ref, bk_ref, bv_ref):
    xn = _rms(h_ref[...], g_ref[...]).astype(BF16)

    def mm(lo, hi):
        return _dot(xn, win_ref[:, lo:hi])

    sq_ref[...] = (mm(0, 1024) * (SWA_DIM ** -0.5)).astype(BF16)
    sk_ref[...] = mm(1024, 1152).astype(BF16)
    sv_ref[...] = mm(1152, 1280).astype(BF16)
    bq_ref[...] = (mm(1280, 1792) * (SB_DIM ** -0.5)).astype(BF16)
    bk_ref[...] = mm(1792, 2304).astype(BF16)
    bv_ref[...] = mm(2304, 2816).astype(BF16)


def _proj1(h, g, win):
    T, D = h.shape
    tm = ROW_TILE
    row = lambda w: pl.BlockSpec((tm, w), lambda i: (i, 0))
    full = lambda a: pl.BlockSpec(a.shape, lambda i: (0, 0))
    widths = (1024, 128, 128, 512, 512, 512)
    return pl.pallas_call(
        _proj1_kernel,
        grid=(T // tm,),
        in_specs=[row(D), full(g), full(win)],
        out_specs=[row(w) for w in widths],
        out_shape=[jax.ShapeDtypeStruct((T, w), BF16) for w in widths],
        compiler_params=_cparams(("parallel",)),
        name="proj1",
    )(h, g, win)


def _swa_kernel(tab_ref, q_ref, k_ref, v_ref, o_ref):
    S = q_ref.shape[0]
    W = WINDOW
    j = pl.program_id(1)
    lane = lax.broadcasted_iota(jnp.int32, (1, LANES), 1)
    dist2 = (lax.broadcasted_iota(jnp.int32, (W, 2 * W), 0) + W) - lax.broadcasted_iota(jnp.int32, (W, 2 * W), 1)
    dist1 = dist2[:, W:]
    heads = []
    for hh in range(2):
        head = j + (SWA_HEADS // 2) * hh
        sink, slope = tab_ref[head], tab_ref[SWA_HEADS + head]
        bias = [jnp.where((d >= 0) & (d < W), slope * d.astype(F32), jnp.inf) for d in (dist1, dist2)]
        heads.append((sink, bias, (lane // 64 == hh).astype(BF16)))
    for i in range(S // W):
        q = q_ref[i * W:(i + 1) * W, :]
        lo = max(i - 1, 0) * W
        k, v = k_ref[lo:(i + 1) * W, :], v_ref[lo:(i + 1) * W, :]
        outs = []
        for sink, bias, mask in heads:
            s = _nt_dot(q * mask, k) - bias[min(i, 1)]
            m = jnp.maximum(jnp.max(s, axis=-1, keepdims=True), sink)
            p = jnp.exp(s - m)
            den = jnp.sum(p, axis=-1, keepdims=True) + jnp.exp(sink - m)
            outs.append(_dot(p.astype(BF16), v) / den)
        o_ref[i * W:(i + 1) * W, :] = jnp.where(lane < 64, outs[0], outs[1]).astype(BF16)


def _swa_attention(tab, sq, sk, sv, batch, seq):
    T = sq.shape[0]
    blk = lambda f: pl.BlockSpec((seq, LANES), f)
    return pl.pallas_call(
        _swa_kernel,
        grid_spec=pltpu.PrefetchScalarGridSpec(
            num_scalar_prefetch=1,
            grid=(batch, SWA_HEADS // 2),
            in_specs=[blk(lambda b, j, t: (b, j)), blk(lambda b, j, t: (b, 0)), blk(lambda b, j, t: (b, 0))],
            out_specs=blk(lambda b, j, t: (b, j)),
        ),
        out_shape=jax.ShapeDtypeStruct((T, SWA_HEADS * SWA_DIM), BF16),
        compiler_params=_cparams(("parallel", "parallel")),
        name="swa_attention",
    )(tab, sq, sk, sv)


def _sb_kernel(q_ref, k_ref, v_ref, o_ref, *, t):
    S = q_ref.shape[0]
    lane = lax.broadcasted_iota(jnp.int32, (1, LANES), 1)
    row = lax.broadcasted_iota(jnp.int32, (t, t), 0)
    col = lax.broadcasted_iota(jnp.int32, (t, t), 1)
    before = col < row
    neg_later = jnp.where(row > col, -1.0, 0.0).astype(BF16)

    def softplus(z):
        return jnp.maximum(z, 0.0) + jnp.log(1.0 + jnp.exp2(jnp.abs(z) * (-LOG2E)))

    for i in range(S // t):
        rows = slice(i * t, (i + 1) * t)
        q = q_ref[rows, :]
        outs = []
        for hh in range(2):
            qm = q * (lane // 64 == hh).astype(BF16)
            z = _nt_dot(qm, k_ref[rows, :])
            sp = jnp.where(before, softplus(z), 0.0)
            a = jnp.where(before, jnp.exp((z - sp) + _dot(sp.astype(BF16), neg_later)), 0.0)
            acc = _dot(a.astype(BF16), v_ref[rows, :])
            run = -jnp.sum(sp, axis=-1, keepdims=True)
            for j in range(i - 1, -1, -1):
                krows = slice(j * t, (j + 1) * t)
                z = _nt_dot(qm, k_ref[krows, :])
                sp = softplus(z)
                a = jnp.exp((z - sp) + (_dot(sp.astype(BF16), neg_later) + run))
                acc = acc + _dot(a.astype(BF16), v_ref[krows, :])
                run = run - jnp.sum(sp, axis=-1, keepdims=True)
            outs.append(acc)
        o_ref[rows, :] = jnp.where(lane < 64, outs[0], outs[1]).astype(BF16)


def _sb_attention(bq, bk, bv, batch, seq):
    T = bq.shape[0]
    blk = pl.BlockSpec((seq, LANES), lambda b, p: (b, p))
    return pl.pallas_call(
        functools.partial(_sb_kernel, t=SB_T),
        grid=(batch, SB_HEADS // 2),
        in_specs=[blk, blk, blk],
        out_specs=blk,
        out_shape=jax.ShapeDtypeStruct((T, SB_HEADS * SB_DIM), BF16),
        compiler_params=_cparams(("parallel", "parallel")),
        name="sb_attention",
    )(bq, bk, bv)


def _split_bf16(x):
    hi = x.astype(BF16)
    return hi, (x - hi.astype(F32)).astype(BF16)


def _out_router_kernel(h_ref, a_ref, b_ref, wa_ref, wb_ref, g_ref, rcat_ref, o_ref, route_ref):
    tm = h_ref.shape[0]
    for rows in (slice(0, tm // 2), slice(tm // 2, tm)):
        hn = h_ref[rows, :] + _dot(a_ref[rows, :], wa_ref[...]) + _dot(b_ref[rows, :], wb_ref[...])
        o_ref[rows, :] = hn
        xhi, xlo = _split_bf16(_rms(hn, g_ref[...]))
        both = _dot(xhi, rcat_ref[...])
        logits = both[:, :LANES] + (both[:, LANES:] + _dot(xlo, rcat_ref[:, :LANES]))
        lane = lax.broadcasted_iota(jnp.int32, logits.shape, 1)
        logits = jnp.where(lane < N_EXPERTS, logits, -jnp.inf)
        m1 = jnp.max(logits, axis=-1, keepdims=True)
        i1 = jnp.min(jnp.where(logits == m1, lane, LANES), axis=-1, keepdims=True)
        rest = jnp.where(lane == i1, -jnp.inf, logits)
        m2 = jnp.max(rest, axis=-1, keepdims=True)
        i2 = jnp.min(jnp.where(rest == m2, lane, LANES), axis=-1, keepdims=True)
        e2 = jnp.exp(m2 - m1)
        den = 1.0 + e2
        route = jnp.where(lane == 0, i1.astype(F32), 0.0)
        route = jnp.where(lane == 1, i2.astype(F32), route)
        route = jnp.where(lane == 2, 1.0 / den, route)
        route = jnp.where(lane == 3, e2 / den, route)
        route_ref[rows, :] = route


def _mixer_out_router(h, a, b, wa, wb, g, rcat):
    T, D = h.shape
    tm = ROW_TILE
    row = lambda w: pl.BlockSpec((tm, w), lambda i: (i, 0))
    full = lambda x: pl.BlockSpec(x.shape, lambda i: (0, 0))
    return pl.pallas_call(
        _out_router_kernel,
        grid=(T // tm,),
        in_specs=[row(D), row(a.shape[1]), row(b.shape[1]), full(wa), full(wb), full(g), full(rcat)],
        out_specs=[row(D), row(LANES)],
        out_shape=[jax.ShapeDtypeStruct((T, D), F32), jax.ShapeDtypeStruct((T, LANES), F32)],
        compiler_params=_cparams(("parallel",)),
        name="mixer_out_router",
    )(h, a, b, wa, wb, g, rcat)


def _rank_kernel(route_ref, rank_ref, counts_ref):
    tm = route_ref.shape[0]

    @pl.when(pl.program_id(0) == 0)
    def _():
        counts_ref[...] = jnp.zeros_like(counts_ref)

    route = route_ref[...]
    lane = lax.broadcasted_iota(jnp.int32, route.shape, 1).astype(F32)
    oh0 = jnp.where(lane == route[:, 0:1], 1.0, 0.0)
    oh1 = jnp.where(lane == route[:, 1:2], 1.0, 0.0)
    both = oh0 + oh1
    earlier = (lax.broadcasted_iota(jnp.int32, (tm, tm), 1) < lax.broadcasted_iota(jnp.int32, (tm, tm), 0))
    prefix = _dot(earlier.astype(BF16), both.astype(BF16)) + counts_ref[0:1, :]
    r0 = jnp.sum(oh0 * prefix, axis=-1, keepdims=True)
    r1 = jnp.sum(oh1 * prefix, axis=-1, keepdims=True)
    rank_ref[...] = jnp.where(lane == 0.0, r0, jnp.where(lane == 1.0, r1, 0.0))
    counts_ref[...] += jnp.sum(both, axis=0, keepdims=True)


def _routing_ranks(route):
    T = route.shape[0]
    tm = RANK_ROWS
    return pl.pallas_call(
        _rank_kernel,
        grid=(T // tm,),
        in_specs=[pl.BlockSpec((tm, LANES), lambda i: (i, 0))],
        out_specs=[pl.BlockSpec((tm, LANES), lambda i: (i, 0)), pl.BlockSpec((8, LANES), lambda i: (0, 0))],
        out_shape=[jax.ShapeDtypeStruct((T, LANES), F32), jax.ShapeDtypeStruct((8, LANES), F32)],
        compiler_params=_cparams(("arbitrary",)),
        name="routing_ranks",
    )(route)


def _wait_rows(src_hbm, dst_ref, sem):
    pltpu.make_async_copy(src_hbm.at[pl.ds(0, dst_ref.shape[0])], dst_ref, sem).wait()


def _cast_kernel(w_ref, o_ref):
    o_ref[...] = w_ref[...].astype(BF16)


def _to_bf16(w):
    E, K, N = w.shape
    blk = pl.BlockSpec((1, CAST_ROWS, N), lambda e, k: (e, k, 0))
    return pl.pallas_call(
        _cast_kernel,
        grid=(E, K // CAST_ROWS),
        in_specs=[blk],
        out_specs=blk,
        out_shape=jax.ShapeDtypeStruct(w.shape, BF16),
        compiler_params=_cparams(("parallel", "parallel")),
        name="cast_bf16",
    )(w)


def _dispatch_kernel(meta_ref, dest_ref, wg_ref, wu_ref, wd_ref, h_ref, og_ref, ou_ref, od_ref, xg_hbm,
                     zero_ref, sem, zsem):
    step = pl.program_id(0) * pl.num_programs(1) + pl.program_id(1)
    tpb = h_ref.shape[0]
    R = zero_ref.shape[0]

    def zero_block(start):
        return pltpu.make_async_copy(zero_ref, xg_hbm.at[pl.ds(pl.multiple_of(start, R), R)], zsem)

    cap = xg_hbm.shape[0]
    fills = [(meta_ref[N_EXPERTS + e] > 0, meta_ref[e] - R) for e in range(N_EXPERTS)]
    fills += [(meta_ref[N_EXPERTS - 1] + b * R < cap, meta_ref[N_EXPERTS - 1] + b * R) for b in range(N_EXPERTS)]

    @pl.when(step == 0)
    def _():
        zero_ref[...] = jnp.zeros_like(zero_ref)
        for do_start in (True, False):
            for cond, start in fills:
                @pl.when(cond)
                def _():
                    zero_block(start).start() if do_start else zero_block(start).wait()

    def scatter_group(gi, carry):
        base = gi * DISPATCH_UNROLL
        rows = [[dest_ref[0, 0, TOP_K * (base + u) + k] for k in range(TOP_K)]
                for u in range(DISPATCH_UNROLL)]
        for u in range(DISPATCH_UNROLL):
            for k in range(TOP_K):
                pltpu.make_async_copy(h_ref.at[pl.ds(base + u, 1)], xg_hbm.at[pl.ds(rows[u][k], 1)], sem).start()
        return carry

    lax.fori_loop(0, tpb // DISPATCH_UNROLL, scatter_group, 0)
    og_ref[...] = wg_ref[...].astype(BF16)
    ou_ref[...] = wu_ref[...].astype(BF16)
    od_ref[...] = wd_ref[...].astype(BF16)
    for k in range(TOP_K):
        pltpu.make_async_copy(h_ref, xg_hbm.at[pl.ds(0, tpb)], sem).wait()


def _dispatch_cast(meta, dest, wg, wu, wd, h, cap):
    E, D, F = wg.shape
    nsteps, _, n = dest.shape
    nk = nsteps // E
    spec = lambda w: pl.BlockSpec((1, w.shape[1] // nk, w.shape[2]), lambda e, k, m: (e, k, 0))
    return pl.pallas_call(
        _dispatch_kernel,
        grid_spec=pltpu.PrefetchScalarGridSpec(
            num_scalar_prefetch=1,
            grid=(E, nk),
            in_specs=[pl.BlockSpec((1, 1, n), lambda e, k, m: (e * nk + k, 0, 0), memory_space=pltpu.SMEM),
                      spec(wg), spec(wu), spec(wd),
                      pl.BlockSpec((n // TOP_K, D), lambda e, k, m: (e * nk + k, 0))],
            out_specs=[spec(wg), spec(wu), spec(wd), pl.BlockSpec(memory_space=pl.ANY)],
            scratch_shapes=[pltpu.VMEM((MOE_ROWS, D), F32), pltpu.SemaphoreType.DMA, pltpu.SemaphoreType.DMA],
        ),
        out_shape=[jax.ShapeDtypeStruct(wg.shape, BF16), jax.ShapeDtypeStruct(wu.shape, BF16),
                   jax.ShapeDtypeStruct(wd.shape, BF16), jax.ShapeDtypeStruct((cap, D), F32)],
        compiler_params=_cparams(("arbitrary", "arbitrary")),
        name="dispatch_cast",
    )(meta, dest, wg, wu, wd, h)


def _moe_kernel(sched_ref, x_ref, g_ref, wg_ref, wu_ref, wd_ref, o_ref, xn_ref, acc_ref):
    i, j = pl.program_id(0), pl.program_id(1)
    nblk = pl.num_programs(0)
    used = i < sched_ref[nblk]

    @pl.when(j == 0)
    def _():
        xn_ref[...] = _rms(x_ref[...], g_ref[...]).astype(BF16)
        acc_ref[...] = jnp.zeros_like(acc_ref)

    @pl.when(used)
    def _():
        xn = xn_ref[...]
        gate = _dot(xn, wg_ref[0])
        act = gate * (1.0 / (1.0 + jnp.exp(-gate))) * _dot(xn, wu_ref[0])
        acc_ref[...] += _dot(act.astype(BF16), wd_ref[0])

    @pl.when(j == pl.num_programs(1) - 1)
    def _():
        o_ref[...] = acc_ref[...]


def _moe_experts(sched, xg, g, wg, wu, wd):
    cap, D = xg.shape
    F = wg.shape[2]
    R, nf = MOE_ROWS, MOE_FF_STEPS
    nblk, tf = cap // R, F // nf

    def blk_i(i, s):
        return jnp.minimum(i, s[nblk] - 1)

    def col_j(i, j, s):
        return jnp.where(i < s[nblk], j, nf - 1)

    return pl.pallas_call(
        _moe_kernel,
        grid_spec=pltpu.PrefetchScalarGridSpec(
            num_scalar_prefetch=1,
            grid=(nblk, nf),
            in_specs=[pl.BlockSpec((R, D), lambda i, j, s: (blk_i(i, s), 0)),
                      pl.BlockSpec(g.shape, lambda i, j, s: (0, 0)),
                      pl.BlockSpec((1, D, tf), lambda i, j, s: (s[blk_i(i, s)], 0, col_j(i, j, s))),
                      pl.BlockSpec((1, D, tf), lambda i, j, s: (s[blk_i(i, s)], 0, col_j(i, j, s))),
                      pl.BlockSpec((1, tf, D), lambda i, j, s: (s[blk_i(i, s)], col_j(i, j, s), 0))],
            out_specs=pl.BlockSpec((R, D), lambda i, j, s: (i, 0)),
            scratch_shapes=[pltpu.VMEM((R, D), BF16), pltpu.VMEM((R, D), F32)],
        ),
        out_shape=jax.ShapeDtypeStruct((cap, D), F32),
        compiler_params=_cparams(("arbitrary", "arbitrary")),
        name="moe_experts",
    )(sched, xg, g, wg, wu, wd)


def _combine_kernel(pos_ref, nxt_ref, h_ref, route_ref, g_ref, y_hbm, o_ref, ybuf0, ybuf1, sem):
    i = pl.program_id(0)
    n = ybuf0.shape[0]
    tm = n // TOP_K

    def issue(idx_ref, base, buf, s):
        for g0 in range(0, n, COMBINE_GROUP):
            rows = [idx_ref[0, 0, base + TOP_K * ((g0 + r) % tm) + (g0 + r) // tm] for r in range(COMBINE_GROUP)]
            for r in range(COMBINE_GROUP):
                pltpu.make_async_copy(y_hbm.at[pl.ds(rows[r], 1)], buf.at[pl.ds(g0 + r, 1)], sem.at[s]).start()

    def finish(buf, s, rows):
        _wait_rows(y_hbm, buf, sem.at[s])
        route = route_ref[rows, :]
        out = h_ref[rows, :] + (buf[:tm, :] * route[:, 2:3] + buf[tm:, :] * route[:, 3:4])
        o_ref[rows, :] = _rms(out, g_ref[...])

    @pl.when(i == 0)
    def _():
        issue(pos_ref, 0, ybuf0, 0)

    issue(pos_ref, n, ybuf1, 1)
    finish(ybuf0, 0, slice(0, tm))
    issue(nxt_ref, 0, ybuf0, 0)
    finish(ybuf1, 1, slice(tm, 2 * tm))

    @pl.when(i == pl.num_programs(0) - 1)
    def _():
        _wait_rows(y_hbm, ybuf0, sem.at[0])


def _combine_norm(pos, h, route, g, yg):
    T, D = h.shape
    nstep, _, n2 = pos.shape
    n = n2 // 2
    tm2 = n2 // TOP_K
    row = lambda w: pl.BlockSpec((tm2, w), lambda i: (i, 0))
    smem = lambda f: pl.BlockSpec((1, 1, n2), f, memory_space=pltpu.SMEM)
    return pl.pallas_call(
        _combine_kernel,
        grid=(nstep,),
        in_specs=[smem(lambda i: (i, 0, 0)), smem(lambda i: (jnp.minimum(i + 1, nstep - 1), 0, 0)),
                  row(D), row(LANES), pl.BlockSpec(g.shape, lambda i: (0, 0)), pl.BlockSpec(memory_space=pl.ANY)],
        out_specs=row(D),
        out_shape=jax.ShapeDtypeStruct((T, D), F32),
        scratch_shapes=[pltpu.VMEM((n, D), F32), pltpu.VMEM((n, D), F32), pltpu.SemaphoreType.DMA((2,))],
        compiler_params=_cparams(("arbitrary",)),
        name="combine_norm",
    )(pos, pos, h, route, g, yg)


def _rope_tables(seq, half):
    inv_freq = ROPE_BASE ** (-jnp.arange(half, dtype=F32) / half)
    ang = jnp.arange(seq).astype(F32)[:, None] * inv_freq[None, :]
    cos, sin = jnp.cos(ang), jnp.sin(ang)
    reps = LANES // (2 * half)
    return jnp.tile(jnp.concatenate([cos, cos], 1), (1, reps)), jnp.tile(jnp.concatenate([-sin, sin], 1), (1, reps))


def _retention_tables():
    log_gamma = jnp.log1p(-(2.0 ** (-5.0 - jnp.arange(RET_HEADS, dtype=F32))))
    idx = jnp.arange(CHUNK, dtype=F32)
    rel = idx[:, None] - idx[None, :]
    din = jnp.where(rel >= 0, jnp.exp(log_gamma[:, None, None] * jnp.maximum(rel, 0.0)), 0.0)
    qd = jnp.exp(log_gamma[:, None] * (idx + 1.0))
    kd = jnp.exp(log_gamma[:, None] * (CHUNK - 1.0 - idx))
    cd = jnp.exp(log_gamma * CHUNK)
    bc = lambda a: jnp.broadcast_to(a[:, :, None], (RET_HEADS, CHUNK, LANES))
    return din, bc(qd), bc(kd), jnp.broadcast_to(cd[:, None, None], (RET_HEADS, 1, LANES))


def kernel(x, ab_norm, ab_w_in, mla_q_norm, mla_w_uq, mla_kv_norm, mla_w_ukv, ret_gn, ab_w_out, ffn_norm,
           ffn_w_gate, ffn_w_up, ffn_w_down, cd_norm, cd_w_in, swa_sinks, cd_w_out, moe_norm, moe_router,
           moe_w_gate, moe_w_up, moe_w_down, final_norm):
    B, S, D = x.shape
    T = B * S
    h = x.reshape(T, D)
    row = lambda g: g.reshape(1, -1)

    w = ab_w_in[0]
    win0 = jnp.concatenate([w[:, :640], jnp.tile(w[:, 640:672], (1, 4)), w[:, 672:]], axis=1).astype(BF16)
    wuq = mla_w_uq[0].reshape(MLA_Q_RANK, MLA_HEADS, MLA_NOPE + MLA_ROPE)
    wuq = jnp.concatenate([wuq[:, :, :MLA_NOPE].reshape(MLA_Q_RANK, -1),
                           wuq[:, :, MLA_NOPE:].reshape(MLA_Q_RANK, -1)], axis=1).astype(BF16)
    wukv = mla_w_ukv[0].reshape(MLA_KV_RANK, MLA_HEADS, MLA_NOPE + MLA_V)
    wukv = jnp.concatenate([wukv[:, :, :MLA_NOPE].reshape(MLA_KV_RANK, -1),
                            wukv[:, :, MLA_NOPE:].reshape(MLA_KV_RANK, -1)], axis=1).astype(BF16)
    cm, sm = _rope_tables(S, MLA_ROPE // 2)
    cr, sr = _rope_tables(S, RET_DK // 2)
    qnope, qrope, knope, krope, v, rq, rk, rv, rg = _proj0(
        h, row(ab_norm[0]), win0, row(mla_q_norm[0]), wuq, row(mla_kv_norm[0]), wukv, cm, sm, cr, sr, S)
    mla = _mla_attention(qnope, qrope, knope, krope, v, B, S)
    ret = _retention(rq, rk, rv, rg, row(ret_gn[0]), *_retention_tables(), B, S)
    wo = ab_w_out[0].astype(BF16)
    nm = MLA_HEADS * MLA_V
    h = _mixer_out_ffn(h, mla, ret, wo[:nm], wo[nm:], row(ffn_norm[0]), _to_bf16(ffn_w_gate)[0],
                       _to_bf16(ffn_w_up)[0], _to_bf16(ffn_w_down)[0])

    w = cd_w_in[0]
    nq = SWA_HEADS * SWA_DIM
    pair_order = np.stack([np.arange(SWA_HEADS // 2), np.arange(SWA_HEADS // 2) + SWA_HEADS // 2], 1).reshape(-1)
    cols = (pair_order[:, None] * SWA_DIM + np.arange(SWA_DIM)[None, :]).reshape(-1)
    win1 = jnp.concatenate([w[:, :nq][:, cols], w[:, nq:]], axis=1).astype(BF16)
    sq, sk, sv, bq, bk, bv = _proj1(h, row(cd_norm[0]), win1)
    slopes = 2.0 ** (-8.0 * jnp.arange(1, SWA_HEADS + 1, dtype=F32) / SWA_HEADS)
    swa = _swa_attention(jnp.concatenate([swa_sinks[0].astype(F32), slopes]), sq, sk, sv, B, S)
    sb = _sb_attention(bq, bk, bv, B, S)
    wo = cd_w_out[0]
    wo_swa = wo[:nq][cols].astype(BF16)
    router = jnp.pad(moe_router[0], ((0, 0), (0, LANES - N_EXPERTS)))
    rhi = router.astype(BF16)
    rlo = (router - rhi.astype(F32)).astype(BF16)
    h, route = _mixer_out_router(h, swa, sb, wo_swa, wo[nq:].astype(BF16), row(moe_norm[0]),
                                 jnp.concatenate([rhi, rlo], axis=1))

    R = MOE_ROWS
    TK = T * TOP_K
    ranks, counts = _routing_ranks(route)
    counts = counts[0, :N_EXPERTS].astype(jnp.int32)
    padded = ((counts + R - 1) // R) * R
    pad_end = jnp.cumsum(padded)
    pad_start = pad_end - padded
    expert = route[:, :TOP_K].astype(jnp.int32)
    start_of = jnp.sum(jnp.where(expert[:, :, None] == jnp.arange(N_EXPERTS, dtype=jnp.int32), pad_start, 0), axis=-1)
    dest = (start_of + ranks[:, :TOP_K].astype(jnp.int32)).reshape(-1)
    cap = -(-TK // R) * R + N_EXPERTS * R
    nblk = cap // R
    blk_exp = jnp.minimum(jnp.searchsorted(pad_end, jnp.arange(nblk, dtype=jnp.int32) * R, side='right'),
                          N_EXPERTS - 1).astype(jnp.int32)
    sched = jnp.concatenate([blk_exp, (pad_end[-1:] // R).astype(jnp.int32)])

    def tiles(tm):
        return dest.reshape(T // tm, 1, TOP_K * tm)

    meta = jnp.concatenate([pad_end, padded]).astype(jnp.int32)
    wg, wu, wd, xg = _dispatch_cast(meta, tiles(T // DISPATCH_STEPS), moe_w_gate[0], moe_w_up[0], moe_w_down[0],
                                    h, cap)
    yg = _moe_experts(sched, xg, row(moe_norm[0]), wg, wu, wd)
    pos = tiles(2 * COMBINE_ROWS)
    out = _combine_norm(pos, h, route, row(final_norm), yg)
    return out.reshape(B, S, D)
```

```python
import functools

import jax
import jax.numpy as jnp
import numpy as np
from jax import lax
from jax.experimental import pallas as pl
from jax.experimental.pallas import tpu as pltpu

F32 = jnp.float32
BF16 = jnp.bfloat16

LANES = 128
EPS = 1e-6
LOG2E = 1.4426950408889634
ROPE_BASE = 10000.0
CHUNK = 128
MLA_HEADS, MLA_NOPE, MLA_ROPE, MLA_V = 8, 64, 32, 64
MLA_Q_RANK, MLA_KV_RANK = 384, 256
RET_HEADS, RET_DK, RET_DV = 8, 64, 128
SWA_HEADS, SWA_KV_HEADS, SWA_DIM, WINDOW = 16, 2, 64, 128
SB_HEADS, SB_DIM = 8, 64
N_EXPERTS, TOP_K = 8, 2

VMEM_LIMIT = 56 * 1024 * 1024
ROW_TILE = 512
FF_STEPS = 2
MOE_FF_STEPS = 2
CAST_ROWS = 512
COMBINE_ROWS = 256
COMBINE_GROUP = 32
RANK_ROWS = 512
DISPATCH_STEPS = 32
DISPATCH_BUFFERS = 3
DISPATCH_UNROLL = 16
MOE_ROWS = 512
MLA_TQ = 512
SB_T = 256


def _cparams(sem):
    return pltpu.CompilerParams(dimension_semantics=sem, vmem_limit_bytes=VMEM_LIMIT)


def _rms(x, g):
    return x * lax.rsqrt(jnp.mean(x * x, axis=-1, keepdims=True) + EPS) * g


def _rope_slab(x, cos, sin_signed, half):
    lane = lax.broadcasted_iota(jnp.int32, x.shape, 1)
    first = (lane % (2 * half)) < half
    partner = jnp.where(first, pltpu.roll(x, LANES - half, 1), pltpu.roll(x, half, 1))
    return x * cos + partner * sin_signed


def _nt_dot(a, b):
    return lax.dot_general(a, b, (((1,), (1,)), ((), ())), preferred_element_type=F32)


def _dot(a, b):
    return jnp.dot(a, b, preferred_element_type=F32)


def _proj0_kernel(h_ref, g_ref, win_ref, qn_ref, wuq_ref, kvn_ref, wukv_ref, cm_ref, sm_ref, cr_ref, sr_ref,
                  qnope_ref, qrope_ref, knope_ref, krope_ref, v_ref, rq_ref, rk_ref, rv_ref, rg_ref):
    xn = _rms(h_ref[...], g_ref[...]).astype(BF16)

    def mm(lo, hi):
        return _dot(xn, win_ref[:, lo:hi])

    cm, sm, cr, sr = cm_ref[...], sm_ref[...], cr_ref[...], sr_ref[...]
    q_scale = (MLA_NOPE + MLA_ROPE) ** -0.5
    qh = _dot(_rms(mm(0, 384), qn_ref[...]).astype(BF16), wuq_ref[...])
    qnope_ref[...] = (qh[:, :512] * q_scale).astype(BF16)
    for s in range(2):
        slab = qh[:, 512 + LANES * s:512 + LANES * (s + 1)]
        qrope_ref[:, LANES * s:LANES * (s + 1)] = (_rope_slab(slab, cm, sm, MLA_ROPE // 2) * q_scale).astype(BF16)
    kvh = _dot(_rms(mm(384, 640), kvn_ref[...]).astype(BF16), wukv_ref[...])
    knope_ref[...] = kvh[:, :512].astype(BF16)
    v_ref[...] = kvh[:, 512:].astype(BF16)
    krope_ref[...] = _rope_slab(mm(640, 768), cm, sm, MLA_ROPE // 2).astype(BF16)
    rq = mm(768, 1280)
    rk = mm(1280, 1792)
    for s in range(4):
        sl = slice(LANES * s, LANES * (s + 1))
        rq_ref[:, sl] = _rope_slab(rq[:, sl], cr, sr, RET_DK // 2).astype(BF16)
        rk_ref[:, sl] = (_rope_slab(rk[:, sl], cr, sr, RET_DK // 2) * (RET_DK ** -0.5)).astype(BF16)
    rv_ref[...] = mm(1792, 2816).astype(BF16)
    rg = mm(2816, 3840)
    rg_ref[...] = (rg * (1.0 / (1.0 + jnp.exp(-rg)))).astype(BF16)


def _proj0(h, g, win, qn, wuq, kvn, wukv, cm, sm, cr, sr, seq):
    T, D = h.shape
    tm = ROW_TILE
    nseq = seq // tm
    row = lambda w: pl.BlockSpec((tm, w), lambda i: (i, 0))
    full = lambda a: pl.BlockSpec(a.shape, lambda i: (0, 0))
    pos = pl.BlockSpec((tm, LANES), lambda i: (i % nseq, 0))
    widths = (512, 256, 512, 128, 512, 512, 512, 1024, 1024)
    return pl.pallas_call(
        _proj0_kernel,
        grid=(T // tm,),
        in_specs=[row(D), full(g), full(win), full(qn), full(wuq), full(kvn), full(wukv), pos, pos, pos, pos],
        out_specs=[row(w) for w in widths],
        out_shape=[jax.ShapeDtypeStruct((T, w), BF16) for w in widths],
        compiler_params=_cparams(("parallel",)),
        name="proj0",
    )(h, g, win, qn, wuq, kvn, wukv, cm, sm, cr, sr)


def _mla_kernel(qn_ref, qr_ref, kn_ref, kr_ref, v_ref, o_ref, kk_ref, *, tq):
    S = qn_ref.shape[0]
    p = pl.program_id(1)
    kk_ref[:, :LANES] = kn_ref[...]
    kk_ref[:, LANES:] = kr_ref[...]
    lane = lax.broadcasted_iota(jnp.int32, (1, LANES), 1)
    quarter = (p % 2) * 2
    row = lax.broadcasted_iota(jnp.int32, (tq, tq), 0)
    col = lax.broadcasted_iota(jnp.int32, (tq, tq), 1)
    causal = col <= row
    for i in range(S // tq):
        lo, hi = i * tq, (i + 1) * tq
        qn = qn_ref[lo:hi, :]
        qr = qr_ref[lo:hi, :]
        outs = []
        for hh in range(2):
            mn = (lane // 64 == hh).astype(BF16)
            mr = (lane // 32 == quarter + hh).astype(BF16)
            qa = jnp.concatenate([qn * mn, qr * mr], axis=1)
            sd = jnp.where(causal, _nt_dot(qa, kk_ref[lo:hi, :]), -jnp.inf)
            m = jnp.max(sd, axis=-1, keepdims=True)
            if i > 0:
                so = _nt_dot(qa, kk_ref[:lo, :])
                m = jnp.maximum(m, jnp.max(so, axis=-1, keepdims=True))
            pd = jnp.exp(sd - m)
            l = jnp.sum(pd, axis=-1, keepdims=True)
            acc = _dot(pd.astype(BF16), v_ref[lo:hi, :])
            if i > 0:
                po = jnp.exp(so - m)
                l = l + jnp.sum(po, axis=-1, keepdims=True)
                acc = acc + _dot(po.astype(BF16), v_ref[:lo, :])
            outs.append(acc / l)
        o_ref[lo:hi, :] = jnp.where(lane < 64, outs[0], outs[1]).astype(BF16)


def _mla_attention(qnope, qrope, knope, krope, v, batch, seq):
    T = qnope.shape[0]
    blk = lambda f: pl.BlockSpec((seq, LANES), f)
    return pl.pallas_call(
        functools.partial(_mla_kernel, tq=MLA_TQ),
        grid=(batch, MLA_HEADS // 2),
        in_specs=[blk(lambda b, p: (b, p)), blk(lambda b, p: (b, p // 2)), blk(lambda b, p: (b, p)),
                  blk(lambda b, p: (b, 0)), blk(lambda b, p: (b, p))],
        out_specs=blk(lambda b, p: (b, p)),
        out_shape=jax.ShapeDtypeStruct((T, MLA_HEADS * MLA_V), BF16),
        scratch_shapes=[pltpu.VMEM((seq, 2 * LANES), BF16)],
        compiler_params=_cparams(("parallel", "parallel")),
        name="mla_attention",
    )(qnope, qrope, knope, krope, v)


def _ret_kernel(q_ref, k_ref, v_ref, g_ref, gn_ref, din_ref, qd_ref, kd_ref, cd_ref, o_ref):
    S = q_ref.shape[0]
    h = pl.program_id(1)
    lane = lax.broadcasted_iota(jnp.int32, (1, LANES), 1)
    qmask = (lane // RET_DK == h % 2).astype(F32)
    din = din_ref[0]
    qd = qd_ref[0] * qmask
    kd = kd_ref[0]
    cd = cd_ref[0]
    gain = gn_ref[...]
    rows = [slice(c * CHUNK, (c + 1) * CHUNK) for c in range(S // CHUNK)]
    kvs = [_dot((k_ref[r, :].astype(F32) * kd).T.astype(BF16), v_ref[r, :]) for r in rows[:-1]]
    states = [None]
    for kv in kvs:
        states.append(kv if states[-1] is None else states[-1] * cd + kv)
    for r, state in zip(rows, states):
        q = q_ref[r, :].astype(F32)
        inner = _nt_dot((q * qmask).astype(BF16), k_ref[r, :]) * din
        y = _dot(inner.astype(BF16), v_ref[r, :])
        if state is not None:
            y = y + _dot((q * qd).astype(BF16), state.astype(BF16))
        mu = jnp.mean(y, axis=-1, keepdims=True)
        var = jnp.mean(jnp.square(y - mu), axis=-1, keepdims=True)
        yn = (y - mu) * lax.rsqrt(var + EPS) * gain
        o_ref[r, :] = (g_ref[r, :].astype(F32) * yn).astype(BF16)


def _retention(rq, rk, rv, rg, gn, din, qd, kd, cd, batch, seq):
    T = rq.shape[0]
    blk = lambda f: pl.BlockSpec((seq, LANES), f)
    tab = pl.BlockSpec((1, CHUNK, LANES), lambda b, h: (h, 0, 0))
    return pl.pallas_call(
        _ret_kernel,
        grid=(batch, RET_HEADS),
        in_specs=[blk(lambda b, h: (b, h // 2)), blk(lambda b, h: (b, h // 2)), blk(lambda b, h: (b, h)),
                  blk(lambda b, h: (b, h)), pl.BlockSpec((1, LANES), lambda b, h: (0, h)), tab, tab, tab,
                  pl.BlockSpec((1, 1, LANES), lambda b, h: (h, 0, 0))],
        out_specs=blk(lambda b, h: (b, h)),
        out_shape=jax.ShapeDtypeStruct((T, RET_HEADS * RET_DV), BF16),
        compiler_params=_cparams(("parallel", "parallel")),
        name="retention",
    )(rq, rk, rv, rg, gn, din, qd, kd, cd)


def _ffn_kernel(h_ref, a_ref, b_ref, wa_ref, wb_ref, g_ref, wg_ref, wu_ref, wd_ref, o_ref, xn_ref, acc_ref):
    j = pl.program_id(1)

    @pl.when(j == 0)
    def _():
        tm = h_ref.shape[0]
        for rows in (slice(0, tm // 2), slice(tm // 2, tm)):
            hn = h_ref[rows, :] + _dot(a_ref[rows, :], wa_ref[...]) + _dot(b_ref[rows, :], wb_ref[...])
            acc_ref[rows, :] = hn
            xn_ref[rows, :] = _rms(hn, g_ref[...]).astype(BF16)

    xn = xn_ref[...]
    gate = _dot(xn, wg_ref[...])
    act = gate * (1.0 / (1.0 + jnp.exp(-gate))) * _dot(xn, wu_ref[...])
    acc_ref[...] += _dot(act.astype(BF16), wd_ref[...])

    @pl.when(j == pl.num_programs(1) - 1)
    def _():
        o_ref[...] = acc_ref[...]


def _mixer_out_ffn(h, a, b, wa, wb, g, wg, wu, wd):
    T, D = h.shape
    F = wg.shape[1]
    tm, nf = ROW_TILE, FF_STEPS
    tf = F // nf
    row = lambda w: pl.BlockSpec((tm, w), lambda i, j: (i, 0))
    full = lambda x: pl.BlockSpec(x.shape, lambda i, j: (0, 0))
    return pl.pallas_call(
        _ffn_kernel,
        grid=(T // tm, nf),
        in_specs=[row(D), row(a.shape[1]), row(b.shape[1]), full(wa), full(wb), full(g),
                  pl.BlockSpec((D, tf), lambda i, j: (0, j)), pl.BlockSpec((D, tf), lambda i, j: (0, j)),
                  pl.BlockSpec((tf, D), lambda i, j: (j, 0))],
        out_specs=row(D),
        out_shape=jax.ShapeDtypeStruct((T, D), F32),
        scratch_shapes=[pltpu.VMEM((tm, D), BF16), pltpu.VMEM((tm, D), F32)],
        compiler_params=_cparams(("parallel", "arbitrary")),
        name="mixer_out_ffn",
    )(h, a, b, wa, wb, g, wg, wu, wd)


def _proj1_kernel(h_ref, g_ref, win_ref, sq_ref, sk_ref, sv_ref, bq_ref, bk_ref, bv_ref):
    xn = _rms(h_ref[...], g_ref[...]).astype(BF16)

    def mm(lo, hi):
        return _dot(xn, win_ref[:, lo:hi])

    sq_ref[...] = (mm(0, 1024) * (SWA_DIM ** -0.5)).astype(BF16)
    sk_ref[...] = mm(1024, 1152).astype(BF16)
    sv_ref[...] = mm(1152, 1280).astype(BF16)
    bq_ref[...] = (mm(1280, 1792) * (SB_DIM ** -0.5)).astype(BF16)
    bk_ref[...] = mm(1792, 2304).astype(BF16)
    bv_ref[...] = mm(2304, 2816).astype(BF16)


def _proj1(h, g, win):
    T, D = h.shape
    tm = ROW_TILE
    row = lambda w: pl.BlockSpec((tm, w), lambda i: (i, 0))
    full = lambda a: pl.BlockSpec(a.shape, lambda i: (0, 0))
    widths = (1024, 128, 128, 512, 512, 512)
    return pl.pallas_call(
        _proj1_kernel,
        grid=(T // tm,),
        in_specs=[row(D), full(g), full(win)],
        out_specs=[row(w) for w in widths],
        out_shape=[jax.ShapeDtypeStruct((T, w), BF16) for w in widths],
        compiler_params=_cparams(("parallel",)),
        name="proj1",
    )(h, g, win)


def _swa_kernel(tab_ref, q_ref, k_ref, v_ref, o_ref):
    S = q_ref.shape[0]
    W = WINDOW
    j = pl.program_id(1)
    lane = lax.broadcasted_iota(jnp.int32, (1, LANES), 1)
    dist2 = (lax.broadcasted_iota(jnp.int32, (W, 2 * W), 0) + W) - lax.broadcasted_iota(jnp.int32, (W, 2 * W), 1)
    dist1 = dist2[:, W:]
    heads = []
    for hh in range(2):
        head = j + (SWA_HEADS // 2) * hh
        sink, slope = tab_ref[head], tab_ref[SWA_HEADS + head]
        bias = [jnp.where((d >= 0) & (d < W), slope * d.astype(F32), jnp.inf) for d in (dist1, dist2)]
        heads.append((sink, bias, (lane // 64 == hh).astype(BF16)))
    for i in range(S // W):
        q = q_ref[i * W:(i + 1) * W, :]
        lo = max(i - 1, 0) * W
        k, v = k_ref[lo:(i + 1) * W, :], v_ref[lo:(i + 1) * W, :]
        outs = []
        for sink, bias, mask in heads:
            s = _nt_dot(q * mask, k) - bias[min(i, 1)]
            m = jnp.maximum(jnp.max(s, axis=-1, keepdims=True), sink)
            p = jnp.exp(s - m)
            den = jnp.sum(p, axis=-1, keepdims=True) + jnp.exp(sink - m)
            outs.append(_dot(p.astype(BF16), v) / den)
        o_ref[i * W:(i + 1) * W, :] = jnp.where(lane < 64, outs[0], outs[1]).astype(BF16)


def _swa_attention(tab, sq, sk, sv, batch, seq):
    T = sq.shape[0]
    blk = lambda f: pl.BlockSpec((seq, LANES), f)
    return pl.pallas_call(
        _swa_kernel,
        grid_spec=pltpu.PrefetchScalarGridSpec(
            num_scalar_prefetch=1,
            grid=(batch, SWA_HEADS // 2),
            in_specs=[blk(lambda b, j, t: (b, j)), blk(lambda b, j, t: (b, 0)), blk(lambda b, j, t: (b, 0))],
            out_specs=blk(lambda b, j, t: (b, j)),
        ),
        out_shape=jax.ShapeDtypeStruct((T, SWA_HEADS * SWA_DIM), BF16),
        compiler_params=_cparams(("parallel", "parallel")),
        name="swa_attention",
    )(tab, sq, sk, sv)


def _sb_kernel(q_ref, k_ref, v_ref, o_ref, *, t):
    S = q_ref.shape[0]
    lane = lax.broadcasted_iota(jnp.int32, (1, LANES), 1)
    row = lax.broadcasted_iota(jnp.int32, (t, t), 0)
    col = lax.broadcasted_iota(jnp.int32, (t, t), 1)
    before = col < row
    neg_later = jnp.where(row > col, -1.0, 0.0).astype(BF16)

    def softplus(z):
        return jnp.maximum(z, 0.0) + jnp.log(1.0 + jnp.exp2(jnp.abs(z) * (-LOG2E)))

    for i in range(S // t):
        rows = slice(i * t, (i + 1) * t)
        q = q_ref[rows, :]
        outs = []
        for hh in range(2):
            qm = q * (lane // 64 == hh).astype(BF16)
            z = _nt_dot(qm, k_ref[rows, :])
            sp = jnp.where(before, softplus(z), 0.0)
            a = jnp.where(before, jnp.exp((z - sp) + _dot(sp.astype(BF16), neg_later)), 0.0)
            acc = _dot(a.astype(BF16), v_ref[rows, :])
            run = -jnp.sum(sp, axis=-1, keepdims=True)
            for j in range(i - 1, -1, -1):
                krows = slice(j * t, (j + 1) * t)
                z = _nt_dot(qm, k_ref[krows, :])
                sp = softplus(z)
                a = jnp.exp((z - sp) + (_dot(sp.astype(BF16), neg_later) + run))
                acc = acc + _dot(a.astype(BF16), v_ref[krows, :])
                run = run - jnp.sum(sp, axis=-1, keepdims=True)
            outs.append(acc)
        o_ref[rows, :] = jnp.where(lane < 64, outs[0], outs[1]).astype(BF16)


def _sb_attention(bq, bk, bv, batch, seq):
    T = bq.shape[0]
    blk = pl.BlockSpec((seq, LANES), lambda b, p: (b, p))
    return pl.pallas_call(
        functools.partial(_sb_kernel, t=SB_T),
        grid=(batch, SB_HEADS // 2),
        in_specs=[blk, blk, blk],
        out_specs=blk,
        out_shape=jax.ShapeDtypeStruct((T, SB_HEADS * SB_DIM), BF16),
        compiler_params=_cparams(("parallel", "parallel")),
        name="sb_attention",
    )(bq, bk, bv)


def _split_bf16(x):
    hi = x.astype(BF16)
    return hi, (x - hi.astype(F32)).astype(BF16)


def _out_router_kernel(h_ref, a_ref, b_ref, wa_ref, wb_ref, g_ref, rcat_ref, o_ref, route_ref):
    tm = h_ref.shape[0]
    for rows in (slice(0, tm // 2), slice(tm // 2, tm)):
        hn = h_ref[rows, :] + _dot(a_ref[rows, :], wa_ref[...]) + _dot(b_ref[rows, :], wb_ref[...])
        o_ref[rows, :] = hn
        xhi, xlo = _split_bf16(_rms(hn, g_ref[...]))
        both = _dot(xhi, rcat_ref[...])
        logits = both[:, :LANES] + (both[:, LANES:] + _dot(xlo, rcat_ref[:, :LANES]))
        lane = lax.broadcasted_iota(jnp.int32, logits.shape, 1)
        logits = jnp.where(lane < N_EXPERTS, logits, -jnp.inf)
        m1 = jnp.max(logits, axis=-1, keepdims=True)
        i1 = jnp.min(jnp.where(logits == m1, lane, LANES), axis=-1, keepdims=True)
        rest = jnp.where(lane == i1, -jnp.inf, logits)
        m2 = jnp.max(rest, axis=-1, keepdims=True)
        i2 = jnp.min(jnp.where(rest == m2, lane, LANES), axis=-1, keepdims=True)
        e2 = jnp.exp(m2 - m1)
        den = 1.0 + e2
        route = jnp.where(lane == 0, i1.astype(F32), 0.0)
        route = jnp.where(lane == 1, i2.astype(F32), route)
        route = jnp.where(lane == 2, 1.0 / den, route)
        route = jnp.where(lane == 3, e2 / den, route)
        route_ref[rows, :] = route


def _mixer_out_router(h, a, b, wa, wb, g, rcat):
    T, D = h.shape
    tm = ROW_TILE
    row = lambda w: pl.BlockSpec((tm, w), lambda i: (i, 0))
    full = lambda x: pl.BlockSpec(x.shape, lambda i: (0, 0))
    return pl.pallas_call(
        _out_router_kernel,
        grid=(T // tm,),
        in_specs=[row(D), row(a.shape[1]), row(b.shape[1]), full(wa), full(wb), full(g), full(rcat)],
        out_specs=[row(D), row(LANES)],
        out_shape=[jax.ShapeDtypeStruct((T, D), F32), jax.ShapeDtypeStruct((T, LANES), F32)],
        compiler_params=_cparams(("parallel",)),
        name="mixer_out_router",
    )(h, a, b, wa, wb, g, rcat)


def _rank_kernel(route_ref, rank_ref, counts_ref):
    tm = route_ref.shape[0]

    @pl.when(pl.program_id(0) == 0)
    def _():
        counts_ref[...] = jnp.zeros_like(counts_ref)

    route = route_ref[...]
    lane = lax.broadcasted_iota(jnp.int32, route.shape, 1).astype(F32)
    oh0 = jnp.where(lane == route[:, 0:1], 1.0, 0.0)
    oh1 = jnp.where(lane == route[:, 1:2], 1.0, 0.0)
    both = oh0 + oh1
    earlier = (lax.broadcasted_iota(jnp.int32, (tm, tm), 1) < lax.broadcasted_iota(jnp.int32, (tm, tm), 0))
    prefix = _dot(earlier.astype(BF16), both.astype(BF16)) + counts_ref[0:1, :]
    r0 = jnp.sum(oh0 * prefix, axis=-1, keepdims=True)
    r1 = jnp.sum(oh1 * prefix, axis=-1, keepdims=True)
    rank_ref[...] = jnp.where(lane == 0.0, r0, jnp.where(lane == 1.0, r1, 0.0))
    counts_ref[...] += jnp.sum(both, axis=0, keepdims=True)


def _routing_ranks(route):
    T = route.shape[0]
    tm = RANK_ROWS
    return pl.pallas_call(
        _rank_kernel,
        grid=(T // tm,),
        in_specs=[pl.BlockSpec((tm, LANES), lambda i: (i, 0))],
        out_specs=[pl.BlockSpec((tm, LANES), lambda i: (i, 0)), pl.BlockSpec((8, LANES), lambda i: (0, 0))],
        out_shape=[jax.ShapeDtypeStruct((T, LANES), F32), jax.ShapeDtypeStruct((8, LANES), F32)],
        compiler_params=_cparams(("arbitrary",)),
        name="routing_ranks",
    )(route)


def _wait_rows(src_hbm, dst_ref, sem):
    pltpu.make_async_copy(src_hbm.at[pl.ds(0, dst_ref.shape[0])], dst_ref, sem).wait()


def _cast_kernel(w_ref, o_ref):
    o_ref[...] = w_ref[...].astype(BF16)


def _to_bf16(w):
    E, K, N = w.shape
    blk = pl.BlockSpec((1, CAST_ROWS, N), lambda e, k: (e, k, 0))
    return pl.pallas_call(
        _cast_kernel,
        grid=(E, K // CAST_ROWS),
        in_specs=[blk],
        out_specs=blk,
        out_shape=jax.ShapeDtypeStruct(w.shape, BF16),
        compiler_params=_cparams(("parallel", "parallel")),
        name="cast_bf16",
    )(w)


def _dispatch_kernel(meta_ref, dest_ref, wg_ref, wu_ref, wd_ref, h_hbm, og_ref, ou_ref, od_ref, xg_hbm,
                     zero_ref, hbuf, hsem, sem, zsem):
    step = pl.program_id(0) * pl.num_programs(1) + pl.program_id(1)
    nsteps = pl.num_programs(0) * pl.num_programs(1)
    tpb = hbuf.shape[1]
    R = zero_ref.shape[0]
    slot = step % DISPATCH_BUFFERS

    def zero_block(start):
        return pltpu.make_async_copy(zero_ref, xg_hbm.at[pl.ds(pl.multiple_of(start, R), R)], zsem)

    def h_block(s):
        return pltpu.make_async_copy(h_hbm.at[pl.ds(pl.multiple_of(s * tpb, tpb), tpb)],
                                     hbuf.at[s % DISPATCH_BUFFERS], hsem.at[s % DISPATCH_BUFFERS])

    def wait_scatter(s):
        for k in range(TOP_K):
            pltpu.make_async_copy(hbuf.at[s % DISPATCH_BUFFERS], xg_hbm.at[pl.ds(0, tpb)],
                                  sem.at[s % DISPATCH_BUFFERS]).wait()

    cap = xg_hbm.shape[0]
    fills = [(meta_ref[N_EXPERTS + e] > 0, meta_ref[e] - R) for e in range(N_EXPERTS)]
    fills += [(meta_ref[N_EXPERTS - 1] + b * R < cap, meta_ref[N_EXPERTS - 1] + b * R) for b in range(N_EXPERTS)]

    @pl.when(step == 0)
    def _():
        h_block(step).start()
        zero_ref[...] = jnp.zeros_like(zero_ref)
        for do_start in (True, False):
            for cond, start in fills:
                @pl.when(cond)
                def _():
                    zero_block(start).start() if do_start else zero_block(start).wait()

    @pl.when(step >= DISPATCH_BUFFERS - 1)
    def _():
        wait_scatter(step - (DISPATCH_BUFFERS - 1))

    @pl.when(step + 1 < nsteps)
    def _():
        h_block(step + 1).start()

    h_block(step).wait()
    src = hbuf.at[slot]

    def scatter_group(gi, carry):
        base = gi * DISPATCH_UNROLL
        rows = [[dest_ref[0, 0, TOP_K * (base + u) + k] for k in range(TOP_K)]
                for u in range(DISPATCH_UNROLL)]
        for u in range(DISPATCH_UNROLL):
            for k in range(TOP_K):
                pltpu.make_async_copy(src.at[pl.ds(base + u, 1)], xg_hbm.at[pl.ds(rows[u][k], 1)], sem.at[slot]).start()
        return carry

    lax.fori_loop(0, tpb // DISPATCH_UNROLL, scatter_group, 0)
    og_ref[...] = wg_ref[...].astype(BF16)
    ou_ref[...] = wu_ref[...].astype(BF16)
    od_ref[...] = wd_ref[...].astype(BF16)

    @pl.when(step == nsteps - 1)
    def _():
        for back in range(DISPATCH_BUFFERS - 2, -1, -1):
            @pl.when(step - back >= 0)
            def _():
                wait_scatter(step - back)


def _dispatch_cast(meta, dest, wg, wu, wd, h, cap):
    E, D, F = wg.shape
    nsteps, _, n = dest.shape
    nk = nsteps // E
    spec = lambda w: pl.BlockSpec((1, w.shape[1] // nk, w.shape[2]), lambda e, k, m: (e, k, 0))
    hbm = pl.BlockSpec(memory_space=pl.ANY)
    nbuf = DISPATCH_BUFFERS
    return pl.pallas_call(
        _dispatch_kernel,
        grid_spec=pltpu.PrefetchScalarGridSpec(
            num_scalar_prefetch=1,
            grid=(E, nk),
            in_specs=[pl.BlockSpec((1, 1, n), lambda e, k, m: (e * nk + k, 0, 0), memory_space=pltpu.SMEM),
                      spec(wg), spec(wu), spec(wd), hbm],
            out_specs=[spec(wg), spec(wu), spec(wd), hbm],
            scratch_shapes=[pltpu.VMEM((MOE_ROWS, D), F32), pltpu.VMEM((nbuf, n // TOP_K, D), F32),
                            pltpu.SemaphoreType.DMA((nbuf,)), pltpu.SemaphoreType.DMA((nbuf,)),
                            pltpu.SemaphoreType.DMA],
        ),
        out_shape=[jax.ShapeDtypeStruct(wg.shape, BF16), jax.ShapeDtypeStruct(wu.shape, BF16),
                   jax.ShapeDtypeStruct(wd.shape, BF16), jax.ShapeDtypeStruct((cap, D), F32)],
        compiler_params=_cparams(("arbitrary", "arbitrary")),
        name="dispatch_cast",
    )(meta, dest, wg, wu, wd, h)


def _moe_kernel(sched_ref, x_ref, g_ref, wg_ref, wu_ref, wd_ref, o_ref, xn_ref, acc_ref):
    i, j = pl.program_id(0), pl.program_id(1)
    nblk = pl.num_programs(0)
    used = i < sched_ref[nblk]

    @pl.when(j == 0)
    def _():
        xn_ref[...] = _rms(x_ref[...], g_ref[...]).astype(BF16)
        acc_ref[...] = jnp.zeros_like(acc_ref)

    @pl.when(used)
    def _():
        xn = xn_ref[...]
        gate = _dot(xn, wg_ref[0])
        act = gate * (1.0 / (1.0 + jnp.exp(-gate))) * _dot(xn, wu_ref[0])
        acc_ref[...] += _dot(act.astype(BF16), wd_ref[0])

    @pl.when(j == pl.num_programs(1) - 1)
    def _():
        o_ref[...] = acc_ref[...]


def _moe_experts(sched, xg, g, wg, wu, wd):
    cap, D = xg.shape
    F = wg.shape[2]
    R, nf = MOE_ROWS, MOE_FF_STEPS
    nblk, tf = cap // R, F // nf

    def blk_i(i, s):
        return jnp.minimum(i, s[nblk] - 1)

    def col_j(i, j, s):
        return jnp.where(i < s[nblk], j, nf - 1)

    return pl.pallas_call(
        _moe_kernel,
        grid_spec=pltpu.PrefetchScalarGridSpec(
            num_scalar_prefetch=1,
            grid=(nblk, nf),
            in_specs=[pl.BlockSpec((R, D), lambda i, j, s: (blk_i(i, s), 0)),
                      pl.BlockSpec(g.shape, lambda i, j, s: (0, 0)),
                      pl.BlockSpec((1, D, tf), lambda i, j, s: (s[blk_i(i, s)], 0, col_j(i, j, s))),
                      pl.BlockSpec((1, D, tf), lambda i, j, s: (s[blk_i(i, s)], 0, col_j(i, j, s))),
                      pl.BlockSpec((1, tf, D), lambda i, j, s: (s[blk_i(i, s)], col_j(i, j, s), 0))],
            out_specs=pl.BlockSpec((R, D), lambda i, j, s: (i, 0)),
            scratch_shapes=[pltpu.VMEM((R, D), BF16), pltpu.VMEM((R, D), F32)],
        ),
        out_shape=jax.ShapeDtypeStruct((cap, D), F32),
        compiler_params=_cparams(("arbitrary", "arbitrary")),
        name="moe_experts",
    )(sched, xg, g, wg, wu, wd)


def _combine_kernel(pos_ref, nxt_ref, h_ref, route_ref, g_ref, y_hbm, o_ref, ybuf0, ybuf1, sem):
    i = pl.program_id(0)
    n = ybuf0.shape[0]
    tm = n // TOP_K

    def issue(idx_ref, base, buf, s):
        for g0 in range(0, n, COMBINE_GROUP):
            rows = [idx_ref[0, 0, base + TOP_K * ((g0 + r) % tm) + (g0 + r) // tm] for r in range(COMBINE_GROUP)]
            for r in range(COMBINE_GROUP):
                pltpu.make_async_copy(y_hbm.at[pl.ds(rows[r], 1)], buf.at[pl.ds(g0 + r, 1)], sem.at[s]).start()

    def finish(buf, s, rows):
        _wait_rows(y_hbm, buf, sem.at[s])
        route = route_ref[rows, :]
        out = h_ref[rows, :] + (buf[:tm, :] * route[:, 2:3] + buf[tm:, :] * route[:, 3:4])
        o_ref[rows, :] = _rms(out, g_ref[...])

    @pl.when(i == 0)
    def _():
        issue(pos_ref, 0, ybuf0, 0)

    issue(pos_ref, n, ybuf1, 1)
    finish(ybuf0, 0, slice(0, tm))
    issue(nxt_ref, 0, ybuf0, 0)
    finish(ybuf1, 1, slice(tm, 2 * tm))

    @pl.when(i == pl.num_programs(0) - 1)
    def _():
        _wait_rows(y_hbm, ybuf0, sem.at[0])


def _combine_norm(pos, h, route, g, yg):
    T, D = h.shape
    nstep, _, n2 = pos.shape
    n = n2 // 2
    tm2 = n2 // TOP_K
    row = lambda w: pl.BlockSpec((tm2, w), lambda i: (i, 0))
    smem = lambda f: pl.BlockSpec((1, 1, n2), f, memory_space=pltpu.SMEM)
    return pl.pallas_call(
        _combine_kernel,
        grid=(nstep,),
        in_specs=[smem(lambda i: (i, 0, 0)), smem(lambda i: (jnp.minimum(i + 1, nstep - 1), 0, 0)),
                  row(D), row(LANES), pl.BlockSpec(g.shape, lambda i: (0, 0)), pl.BlockSpec(memory_space=pl.ANY)],
        out_specs=row(D),
        out_shape=jax.ShapeDtypeStruct((T, D), F32),
        scratch_shapes=[pltpu.VMEM((n, D), F32), pltpu.VMEM((n, D), F32), pltpu.SemaphoreType.DMA((2,))],
        compiler_params=_cparams(("arbitrary",)),
        name="combine_norm",
    )(pos, pos, h, route, g, yg)


def _rope_tables(seq, half):
    inv_freq = ROPE_BASE ** (-jnp.arange(half, dtype=F32) / half)
    ang = jnp.arange(seq).astype(F32)[:, None] * inv_freq[None, :]
    cos, sin = jnp.cos(ang), jnp.sin(ang)
    reps = LANES // (2 * half)
    return jnp.tile(jnp.concatenate([cos, cos], 1), (1, reps)), jnp.tile(jnp.concatenate([-sin, sin], 1), (1, reps))


def _retention_tables():
    log_gamma = jnp.log1p(-(2.0 ** (-5.0 - jnp.arange(RET_HEADS, dtype=F32))))
    idx = jnp.arange(CHUNK, dtype=F32)
    rel = idx[:, None] - idx[None, :]
    din = jnp.where(rel >= 0, jnp.exp(log_gamma[:, None, None] * jnp.maximum(rel, 0.0)), 0.0)
    qd = jnp.exp(log_gamma[:, None] * (idx + 1.0))
    kd = jnp.exp(log_gamma[:, None] * (CHUNK - 1.0 - idx))
    cd = jnp.exp(log_gamma * CHUNK)
    bc = lambda a: jnp.broadcast_to(a[:, :, None], (RET_HEADS, CHUNK, LANES))
    return din, bc(qd), bc(kd), jnp.broadcast_to(cd[:, None, None], (RET_HEADS, 1, LANES))


def kernel(x, ab_norm, ab_w_in, mla_q_norm, mla_w_uq, mla_kv_norm, mla_w_ukv, ret_gn, ab_w_out, ffn_norm,
           ffn_w_gate, ffn_w_up, ffn_w_down, cd_norm, cd_w_in, swa_sinks, cd_w_out, moe_norm, moe_router,
           moe_w_gate, moe_w_up, moe_w_down, final_norm):
    B, S, D = x.shape
    T = B * S
    h = x.reshape(T, D)
    row = lambda g: g.reshape(1, -1)

    w = ab_w_in[0]
    win0 = jnp.concatenate([w[:, :640], jnp.tile(w[:, 640:672], (1, 4)), w[:, 672:]], axis=1).astype(BF16)
    wuq = mla_w_uq[0].reshape(MLA_Q_RANK, MLA_HEADS, MLA_NOPE + MLA_ROPE)
    wuq = jnp.concatenate([wuq[:, :, :MLA_NOPE].reshape(MLA_Q_RANK, -1),
                           wuq[:, :, MLA_NOPE:].reshape(MLA_Q_RANK, -1)], axis=1).astype(BF16)
    wukv = mla_w_ukv[0].reshape(MLA_KV_RANK, MLA_HEADS, MLA_NOPE + MLA_V)
    wukv = jnp.concatenate([wukv[:, :, :MLA_NOPE].reshape(MLA_KV_RANK, -1),
                            wukv[:, :, MLA_NOPE:].reshape(MLA_KV_RANK, -1)], axis=1).astype(BF16)
    cm, sm = _rope_tables(S, MLA_ROPE // 2)
    cr, sr = _rope_tables(S, RET_DK // 2)
    qnope, qrope, knope, krope, v, rq, rk, rv, rg = _proj0(
        h, row(ab_norm[0]), win0, row(mla_q_norm[0]), wuq, row(mla_kv_norm[0]), wukv, cm, sm, cr, sr, S)
    mla = _mla_attention(qnope, qrope, knope, krope, v, B, S)
    ret = _retention(rq, rk, rv, rg, row(ret_gn[0]), *_retention_tables(), B, S)
    wo = ab_w_out[0].astype(BF16)
    nm = MLA_HEADS * MLA_V
    h = _mixer_out_ffn(h, mla, ret, wo[:nm], wo[nm:], row(ffn_norm[0]), _to_bf16(ffn_w_gate)[0],
                       _to_bf16(ffn_w_up)[0], _to_bf16(ffn_w_down)[0])

    w = cd_w_in[0]
    nq = SWA_HEADS * SWA_DIM
    pair_order = np.stack([np.arange(SWA_HEADS // 2), np.arange(SWA_HEADS // 2) + SWA_HEADS // 2], 1).reshape(-1)
    cols = (pair_order[:, None] * SWA_DIM + np.arange(SWA_DIM)[None, :]).reshape(-1)
    win1 = jnp.concatenate([w[:, :nq][:, cols], w[:, nq:]], axis=1).astype(BF16)
    sq, sk, sv, bq, bk, bv = _proj1(h, row(cd_norm[0]), win1)
    slopes = 2.0 ** (-8.0 * jnp.arange(1, SWA_HEADS + 1, dtype=F32) / SWA_HEADS)
    swa = _swa_attention(jnp.concatenate([swa_sinks[0].astype(F32), slopes]), sq, sk, sv, B, S)
    sb = _sb_attention(bq, bk, bv, B, S)
    wo = cd_w_out[0]
    wo_swa = wo[:nq][cols].astype(BF16)
    router = jnp.pad(moe_router[0], ((0, 0), (0, LANES - N_EXPERTS)))
    rhi = router.astype(BF16)
    rlo = (router - rhi.astype(F32)).astype(BF16)
    h, route = _mixer_out_router(h, swa, sb, wo_swa, wo[nq:].astype(BF16), row(moe_norm[0]),
                                 jnp.concatenate([rhi, rlo], axis=1))

    R = MOE_ROWS
    TK = T * TOP_K
    ranks, counts = _routing_ranks(route)
    counts = counts[0, :N_EXPERTS].astype(jnp.int32)
    padded = ((counts + R - 1) // R) * R
    eidx = jnp.arange(N_EXPERTS, dtype=jnp.int32)
    pad_end = jnp.sum(jnp.where(eidx[None, :] <= eidx[:, None], padded[None, :], 0), axis=1)
    pad_start = pad_end - padded
    expert = route[:, :TOP_K].astype(jnp.int32)
    start_of = jnp.sum(jnp.where(expert[:, :, None] == eidx, pad_start, 0), axis=-1)
    dest = (start_of + ranks[:, :TOP_K].astype(jnp.int32)).reshape(-1)
    cap = -(-TK // R) * R + N_EXPERTS * R
    nblk = cap // R
    blk_start = jnp.arange(nblk, dtype=jnp.int32) * R
    blk_exp = jnp.minimum(jnp.sum((blk_start[:, None] >= pad_end[None, :]).astype(jnp.int32), axis=1), N_EXPERTS - 1)
    sched = jnp.concatenate([blk_exp, (pad_end[-1:] // R).astype(jnp.int32)])

    def tiles(tm):
        return dest.reshape(T // tm, 1, TOP_K * tm)

    meta = jnp.concatenate([pad_end, padded]).astype(jnp.int32)
    wg, wu, wd, xg = _dispatch_cast(meta, tiles(T // DISPATCH_STEPS), moe_w_gate[0], moe_w_up[0], moe_w_down[0],
                                    h, cap)
    yg = _moe_experts(sched, xg, row(moe_norm[0]), wg, wu, wd)
    pos = tiles(2 * COMBINE_ROWS)
    out = _combine_norm(pos, h, route, row(final_norm), yg)
    return out.reshape(B, S, D)
```

```python
import functools

import jax
import jax.numpy as jnp
import numpy as np
from jax import lax
from jax.experimental import pallas as pl
from jax.experimental.pallas import tpu as pltpu

F32 = jnp.float32
BF16 = jnp.bfloat16

LANES = 128
EPS = 1e-6
LOG2E = 1.4426950408889634
ROPE_BASE = 10000.0
CHUNK = 128
MLA_HEADS, MLA_NOPE, MLA_ROPE, MLA_V = 8, 64, 32, 64
MLA_Q_RANK, MLA_KV_RANK = 384, 256
RET_HEADS, RET_DK, RET_DV = 8, 64, 128
SWA_HEADS, SWA_KV_HEADS, SWA_DIM, WINDOW = 16, 2, 64, 128
SB_HEADS, SB_DIM = 8, 64
N_EXPERTS, TOP_K = 8, 2
MLA_NOPE_ALL, MLA_ROPE_ALL, MLA_V_ALL = MLA_HEADS * MLA_NOPE, MLA_HEADS * MLA_ROPE, MLA_HEADS * MLA_V
AB_COLS = tuple(int(c) for c in np.cumsum((0, MLA_Q_RANK, MLA_KV_RANK, LANES) + (RET_HEADS * RET_DK,) * 2
                                          + (RET_HEADS * RET_DV,) * 2))
CD_COLS = tuple(int(c) for c in np.cumsum((0, SWA_HEADS * SWA_DIM) + (SWA_KV_HEADS * SWA_DIM,) * 2
                                          + (SB_HEADS * SB_DIM,) * 3))

VMEM_LIMIT = 56 * 1024 * 1024
ROW_TILE = 512
FF_STEPS = 2
MOE_FF_STEPS = 2
CAST_ROWS = 512
COMBINE_ROWS = 256
COMBINE_GROUP = 32
RANK_ROWS = 512
DISPATCH_STEPS = 32
DISPATCH_BUFFERS = 3
DISPATCH_UNROLL = 16
MOE_ROWS = 512
MLA_TQ = 512
SB_T = 256


def _cparams(sem):
    return pltpu.CompilerParams(dimension_semantics=sem, vmem_limit_bytes=VMEM_LIMIT)


def _rms(x, g):
    return x * lax.rsqrt(jnp.mean(x * x, axis=-1, keepdims=True) + EPS) * g


def _rope_slab(x, cos, sin_signed, half):
    lane = lax.broadcasted_iota(jnp.int32, x.shape, 1)
    first = (lane % (2 * half)) < half
    partner = jnp.where(first, pltpu.roll(x, LANES - half, 1), pltpu.roll(x, half, 1))
    return x * cos + partner * sin_signed


def _nt_dot(a, b):
    return lax.dot_general(a, b, (((1,), (1,)), ((), ())), preferred_element_type=F32)


def _dot(a, b):
    return jnp.dot(a, b, preferred_element_type=F32)


def _proj0_kernel(h_ref, g_ref, win_ref, qn_ref, wuq_ref, kvn_ref, wukv_ref, cm_ref, sm_ref, cr_ref, sr_ref,
                  qnope_ref, qrope_ref, knope_ref, krope_ref, v_ref, rq_ref, rk_ref, rv_ref, rg_ref):
    xn = _rms(h_ref[...], g_ref[...]).astype(BF16)

    def mm(seg):
        return _dot(xn, win_ref[:, AB_COLS[seg]:AB_COLS[seg + 1]])

    cm, sm, cr, sr = cm_ref[...], sm_ref[...], cr_ref[...], sr_ref[...]
    q_scale = (MLA_NOPE + MLA_ROPE) ** -0.5
    qh = _dot(_rms(mm(0), qn_ref[...]).astype(BF16), wuq_ref[...])
    qnope_ref[...] = (qh[:, :MLA_NOPE_ALL] * q_scale).astype(BF16)
    for s in range(MLA_ROPE_ALL // LANES):
        slab = qh[:, MLA_NOPE_ALL + LANES * s:MLA_NOPE_ALL + LANES * (s + 1)]
        qrope_ref[:, LANES * s:LANES * (s + 1)] = (_rope_slab(slab, cm, sm, MLA_ROPE // 2) * q_scale).astype(BF16)
    kvh = _dot(_rms(mm(1), kvn_ref[...]).astype(BF16), wukv_ref[...])
    knope_ref[...] = kvh[:, :MLA_NOPE_ALL].astype(BF16)
    v_ref[...] = kvh[:, MLA_NOPE_ALL:].astype(BF16)
    krope_ref[...] = _rope_slab(mm(2), cm, sm, MLA_ROPE // 2).astype(BF16)
    rq = mm(3)
    rk = mm(4)
    for s in range(RET_HEADS * RET_DK // LANES):
        sl = slice(LANES * s, LANES * (s + 1))
        rq_ref[:, sl] = _rope_slab(rq[:, sl], cr, sr, RET_DK // 2).astype(BF16)
        rk_ref[:, sl] = (_rope_slab(rk[:, sl], cr, sr, RET_DK // 2) * (RET_DK ** -0.5)).astype(BF16)
    rv_ref[...] = mm(5).astype(BF16)
    rg = mm(6)
    rg_ref[...] = (rg * (1.0 / (1.0 + jnp.exp(-rg)))).astype(BF16)


def _proj0(h, g, win, qn, wuq, kvn, wukv, cm, sm, cr, sr, seq):
    T, D = h.shape
    tm = ROW_TILE
    nseq = seq // tm
    row = lambda w: pl.BlockSpec((tm, w), lambda i: (i, 0))
    full = lambda a: pl.BlockSpec(a.shape, lambda i: (0, 0))
    pos = pl.BlockSpec((tm, LANES), lambda i: (i % nseq, 0))
    widths = (MLA_NOPE_ALL, MLA_ROPE_ALL, MLA_NOPE_ALL, LANES, MLA_V_ALL) + tuple(np.diff(AB_COLS[3:]))
    return pl.pallas_call(
        _proj0_kernel,
        grid=(T // tm,),
        in_specs=[row(D), full(g), full(win), full(qn), full(wuq), full(kvn), full(wukv), pos, pos, pos, pos],
        out_specs=[row(w) for w in widths],
        out_shape=[jax.ShapeDtypeStruct((T, w), BF16) for w in widths],
        compiler_params=_cparams(("parallel",)),
        name="proj0",
    )(h, g, win, qn, wuq, kvn, wukv, cm, sm, cr, sr)


def _mla_kernel(qn_ref, qr_ref, kn_ref, kr_ref, v_ref, o_ref, kk_ref, *, tq):
    S = qn_ref.shape[0]
    p = pl.program_id(1)
    kk_ref[:, :LANES] = kn_ref[...]
    kk_ref[:, LANES:] = kr_ref[...]
    lane = lax.broadcasted_iota(jnp.int32, (1, LANES), 1)
    quarter = (p % 2) * 2
    row = lax.broadcasted_iota(jnp.int32, (tq, tq), 0)
    col = lax.broadcasted_iota(jnp.int32, (tq, tq), 1)
    causal = col <= row
    for i in range(S // tq):
        lo, hi = i * tq, (i + 1) * tq
        qn = qn_ref[lo:hi, :]
        qr = qr_ref[lo:hi, :]
        outs = []
        for hh in range(2):
            mn = (lane // 64 == hh).astype(BF16)
            mr = (lane // 32 == quarter + hh).astype(BF16)
            qa = jnp.concatenate([qn * mn, qr * mr], axis=1)
            sd = jnp.where(causal, _nt_dot(qa, kk_ref[lo:hi, :]), -jnp.inf)
            m = jnp.max(sd, axis=-1, keepdims=True)
            if i > 0:
                so = _nt_dot(qa, kk_ref[:lo, :])
                m = jnp.maximum(m, jnp.max(so, axis=-1, keepdims=True))
            pd = jnp.exp(sd - m)
            l = jnp.sum(pd, axis=-1, keepdims=True)
            acc = _dot(pd.astype(BF16), v_ref[lo:hi, :])
            if i > 0:
                po = jnp.exp(so - m)
                l = l + jnp.sum(po, axis=-1, keepdims=True)
                acc = acc + _dot(po.astype(BF16), v_ref[:lo, :])
            outs.append(acc / l)
        o_ref[lo:hi, :] = jnp.where(lane < 64, outs[0], outs[1]).astype(BF16)


def _mla_attention(qnope, qrope, knope, krope, v, batch, seq):
    T = qnope.shape[0]
    blk = lambda f: pl.BlockSpec((seq, LANES), f)
    return pl.pallas_call(
        functools.partial(_mla_kernel, tq=MLA_TQ),
        grid=(batch, MLA_HEADS // 2),
        in_specs=[blk(lambda b, p: (b, p)), blk(lambda b, p: (b, p // 2)), blk(lambda b, p: (b, p)),
                  blk(lambda b, p: (b, 0)), blk(lambda b, p: (b, p))],
        out_specs=blk(lambda b, p: (b, p)),
        out_shape=jax.ShapeDtypeStruct((T, MLA_HEADS * MLA_V), BF16),
        scratch_shapes=[pltpu.VMEM((seq, 2 * LANES), BF16)],
        compiler_params=_cparams(("parallel", "parallel")),
        name="mla_attention",
    )(qnope, qrope, knope, krope, v)


def _ret_kernel(q_ref, k_ref, v_ref, g_ref, gn_ref, din_ref, qd_ref, kd_ref, cd_ref, o_ref):
    S = q_ref.shape[0]
    h = pl.program_id(1)
    lane = lax.broadcasted_iota(jnp.int32, (1, LANES), 1)
    qmask = (lane // RET_DK == h % 2).astype(F32)
    din = din_ref[0]
    qd = qd_ref[0] * qmask
    kd = kd_ref[0]
    cd = cd_ref[0]
    gain = gn_ref[...]
    rows = [slice(c * CHUNK, (c + 1) * CHUNK) for c in range(S // CHUNK)]
    kvs = [_dot((k_ref[r, :].astype(F32) * kd).T.astype(BF16), v_ref[r, :]) for r in rows[:-1]]
    states = [None]
    for kv in kvs:
        states.append(kv if states[-1] is None else states[-1] * cd + kv)
    for r, state in zip(rows, states):
        q = q_ref[r, :].astype(F32)
        inner = _nt_dot((q * qmask).astype(BF16), k_ref[r, :]) * din
        y = _dot(inner.astype(BF16), v_ref[r, :])
        if state is not None:
            y = y + _dot((q * qd).astype(BF16), state.astype(BF16))
        mu = jnp.mean(y, axis=-1, keepdims=True)
        var = jnp.mean(jnp.square(y - mu), axis=-1, keepdims=True)
        yn = (y - mu) * lax.rsqrt(var + EPS) * gain
        o_ref[r, :] = (g_ref[r, :].astype(F32) * yn).astype(BF16)


def _retention(rq, rk, rv, rg, gn, din, qd, kd, cd, batch, seq):
    T = rq.shape[0]
    blk = lambda f: pl.BlockSpec((seq, LANES), f)
    tab = pl.BlockSpec((1, CHUNK, LANES), lambda b, h: (h, 0, 0))
    return pl.pallas_call(
        _ret_kernel,
        grid=(batch, RET_HEADS),
        in_specs=[blk(lambda b, h: (b, h // 2)), blk(lambda b, h: (b, h // 2)), blk(lambda b, h: (b, h)),
                  blk(lambda b, h: (b, h)), pl.BlockSpec((1, LANES), lambda b, h: (0, h)), tab, tab, tab,
                  pl.BlockSpec((1, 1, LANES), lambda b, h: (h, 0, 0))],
        out_specs=blk(lambda b, h: (b, h)),
        out_shape=jax.ShapeDtypeStruct((T, RET_HEADS * RET_DV), BF16),
        compiler_params=_cparams(("parallel", "parallel")),
        name="retention",
    )(rq, rk, rv, rg, gn, din, qd, kd, cd)


def _ffn_kernel(h_ref, a_ref, b_ref, wa_ref, wb_ref, g_ref, wg_ref, wu_ref, wd_ref, o_ref, xn_ref, acc_ref):
    j = pl.program_id(1)

    @pl.when(j == 0)
    def _():
        tm = h_ref.shape[0]
        for rows in (slice(0, tm // 2), slice(tm // 2, tm)):
            hn = h_ref[rows, :] + _dot(a_ref[rows, :], wa_ref[...]) + _dot(b_ref[rows, :], wb_ref[...])
            acc_ref[rows, :] = hn
            xn_ref[rows, :] = _rms(hn, g_ref[...]).astype(BF16)

    xn = xn_ref[...]
    gate = _dot(xn, wg_ref[...])
    act = gate * (1.0 / (1.0 + jnp.exp(-gate))) * _dot(xn, wu_ref[...])
    acc_ref[...] += _dot(act.astype(BF16), wd_ref[...])

    @pl.when(j == pl.num_programs(1) - 1)
    def _():
        o_ref[...] = acc_ref[...]


def _mixer_out_ffn(h, a, b, wa, wb, g, wg, wu, wd):
    T, D = h.shape
    F = wg.shape[1]
    tm, nf = ROW_TILE, FF_STEPS
    tf = F // nf
    row = lambda w: pl.BlockSpec((tm, w), lambda i, j: (i, 0))
    full = lambda x: pl.BlockSpec(x.shape, lambda i, j: (0, 0))
    return pl.pallas_call(
        _ffn_kernel,
        grid=(T // tm, nf),
        in_specs=[row(D), row(a.shape[1]), row(b.shape[1]), full(wa), full(wb), full(g),
                  pl.BlockSpec((D, tf), lambda i, j: (0, j)), pl.BlockSpec((D, tf), lambda i, j: (0, j)),
                  pl.BlockSpec((tf, D), lambda i, j: (j, 0))],
        out_specs=row(D),
        out_shape=jax.ShapeDtypeStruct((T, D), F32),
        scratch_shapes=[pltpu.VMEM((tm, D), BF16), pltpu.VMEM((tm, D), F32)],
        compiler_params=_cparams(("parallel", "arbitrary")),
        name="mixer_out_ffn",
    )(h, a, b, wa, wb, g, wg, wu, wd)


def _proj1_kernel(h_ref, g_ref, win_ref, sq_ref, sk_ref, sv_ref, bq_ref, bk_ref, bv_ref):
    xn = _rms(h_ref[...], g_ref[...]).astype(BF16)

    def mm(seg):
        return _dot(xn, win_ref[:, CD_COLS[seg]:CD_COLS[seg + 1]])

    sq_ref[...] = (mm(0) * (SWA_DIM ** -0.5)).astype(BF16)
    sk_ref[...] = mm(1).astype(BF16)
    sv_ref[...] = mm(2).astype(BF16)
    bq_ref[...] = (mm(3) * (SB_DIM ** -0.5)).astype(BF16)
    bk_ref[...] = mm(4).astype(BF16)
    bv_ref[...] = mm(5).astype(BF16)


def _proj1(h, g, win):
    T, D = h.shape
    tm = ROW_TILE
    row = lambda w: pl.BlockSpec((tm, w), lambda i: (i, 0))
    full = lambda a: pl.BlockSpec(a.shape, lambda i: (0, 0))
    widths = tuple(np.diff(CD_COLS))
    return pl.pallas_call(
        _proj1_kernel,
        grid=(T // tm,),
        in_specs=[row(D), full(g), full(win)],
        out_specs=[row(w) for w in widths],
        out_shape=[jax.ShapeDtypeStruct((T, w), BF16) for w in widths],
        compiler_params=_cparams(("parallel",)),
        name="proj1",
    )(h, g, win)


def _swa_kernel(tab_ref, q_ref, k_ref, v_ref, o_ref):
    S = q_ref.shape[0]
    W = WINDOW
    j = pl.program_id(1)
    lane = lax.broadcasted_iota(jnp.int32, (1, LANES), 1)
    dist2 = (lax.broadcasted_iota(jnp.int32, (W, 2 * W), 0) + W) - lax.broadcasted_iota(jnp.int32, (W, 2 * W), 1)
    dist1 = dist2[:, W:]
    heads = []
    for hh in range(2):
        head = j + (SWA_HEADS // 2) * hh
        sink, slope = tab_ref[head], tab_ref[SWA_HEADS + head]
        bias = [jnp.where((d >= 0) & (d < W), slope * d.astype(F32), jnp.inf) for d in (dist1, dist2)]
        heads.append((sink, bias, (lane // 64 == hh).astype(BF16)))
    for i in range(S // W):
        q = q_ref[i * W:(i + 1) * W, :]
        lo = max(i - 1, 0) * W
        k, v = k_ref[lo:(i + 1) * W, :], v_ref[lo:(i + 1) * W, :]
        outs = []
        for sink, bias, mask in heads:
            s = _nt_dot(q * mask, k) - bias[min(i, 1)]
            m = jnp.maximum(jnp.max(s, axis=-1, keepdims=True), sink)
            p = jnp.exp(s - m)
            den = jnp.sum(p, axis=-1, keepdims=True) + jnp.exp(sink - m)
            outs.append(_dot(p.astype(BF16), v) / den)
        o_ref[i * W:(i + 1) * W, :] = jnp.where(lane < 64, outs[0], outs[1]).astype(BF16)


def _swa_attention(tab, sq, sk, sv, batch, seq):
    T = sq.shape[0]
    blk = lambda f: pl.BlockSpec((seq, LANES), f)
    return pl.pallas_call(
        _swa_kernel,
        grid_spec=pltpu.PrefetchScalarGridSpec(
            num_scalar_prefetch=1,
            grid=(batch, SWA_HEADS // 2),
            in_specs=[blk(lambda b, j, t: (b, j)), blk(lambda b, j, t: (b, 0)), blk(lambda b, j, t: (b, 0))],
            out_specs=blk(lambda b, j, t: (b, j)),
        ),
        out_shape=jax.ShapeDtypeStruct((T, SWA_HEADS * SWA_DIM), BF16),
        compiler_params=_cparams(("parallel", "parallel")),
        name="swa_attention",
    )(tab, sq, sk, sv)


def _sb_kernel(q_ref, k_ref, v_ref, o_ref, *, t):
    S = q_ref.shape[0]
    lane = lax.broadcasted_iota(jnp.int32, (1, LANES), 1)
    row = lax.broadcasted_iota(jnp.int32, (t, t), 0)
    col = lax.broadcasted_iota(jnp.int32, (t, t), 1)
    before = col < row
    neg_later = jnp.where(row > col, -1.0, 0.0).astype(BF16)

    def softplus(z):
        return jnp.maximum(z, 0.0) + jnp.log(1.0 + jnp.exp2(jnp.abs(z) * (-LOG2E)))

    for i in range(S // t):
        rows = slice(i * t, (i + 1) * t)
        q = q_ref[rows, :]
        outs = []
        for hh in range(2):
            qm = q * (lane // 64 == hh).astype(BF16)
            z = _nt_dot(qm, k_ref[rows, :])
            sp = jnp.where(before, softplus(z), 0.0)
            a = jnp.where(before, jnp.exp((z - sp) + _dot(sp.astype(BF16), neg_later)), 0.0)
            acc = _dot(a.astype(BF16), v_ref[rows, :])
            run = -jnp.sum(sp, axis=-1, keepdims=True)
            for j in range(i - 1, -1, -1):
                krows = slice(j * t, (j + 1) * t)
                z = _nt_dot(qm, k_ref[krows, :])
                sp = softplus(z)
                a = jnp.exp((z - sp) + (_dot(sp.astype(BF16), neg_later) + run))
                acc = acc + _dot(a.astype(BF16), v_ref[krows, :])
                run = run - jnp.sum(sp, axis=-1, keepdims=True)
            outs.append(acc)
        o_ref[rows, :] = jnp.where(lane < 64, outs[0], outs[1]).astype(BF16)


def _sb_attention(bq, bk, bv, batch, seq):
    T = bq.shape[0]
    blk = pl.BlockSpec((seq, LANES), lambda b, p: (b, p))
    return pl.pallas_call(
        functools.partial(_sb_kernel, t=SB_T),
        grid=(batch, SB_HEADS // 2),
        in_specs=[blk, blk, blk],
        out_specs=blk,
        out_shape=jax.ShapeDtypeStruct((T, SB_HEADS * SB_DIM), BF16),
        compiler_params=_cparams(("parallel", "parallel")),
        name="sb_attention",
    )(bq, bk, bv)


def _split_bf16(x):
    hi = x.astype(BF16)
    return hi, (x - hi.astype(F32)).astype(BF16)


def _out_router_kernel(h_ref, a_ref, b_ref, wa_ref, wb_ref, g_ref, rcat_ref, o_ref, route_ref):
    tm = h_ref.shape[0]
    for rows in (slice(0, tm // 2), slice(tm // 2, tm)):
        hn = h_ref[rows, :] + _dot(a_ref[rows, :], wa_ref[...]) + _dot(b_ref[rows, :], wb_ref[...])
        o_ref[rows, :] = hn
        xhi, xlo = _split_bf16(_rms(hn, g_ref[...]))
        both = _dot(xhi, rcat_ref[...])
        logits = both[:, :LANES] + (both[:, LANES:] + _dot(xlo, rcat_ref[:, :LANES]))
        lane = lax.broadcasted_iota(jnp.int32, logits.shape, 1)
        logits = jnp.where(lane < N_EXPERTS, logits, -jnp.inf)
        m1 = jnp.max(logits, axis=-1, keepdims=True)
        i1 = jnp.min(jnp.where(logits == m1, lane, LANES), axis=-1, keepdims=True)
        rest = jnp.where(lane == i1, -jnp.inf, logits)
        m2 = jnp.max(rest, axis=-1, keepdims=True)
        i2 = jnp.min(jnp.where(rest == m2, lane, LANES), axis=-1, keepdims=True)
        e2 = jnp.exp(m2 - m1)
        den = 1.0 + e2
        route = jnp.where(lane == 0, i1.astype(F32), 0.0)
        route = jnp.where(lane == 1, i2.astype(F32), route)
        route = jnp.where(lane == 2, 1.0 / den, route)
        route = jnp.where(lane == 3, e2 / den, route)
        route_ref[rows, :] = route


def _mixer_out_router(h, a, b, wa, wb, g, rcat):
    T, D = h.shape
    tm = ROW_TILE
    row = lambda w: pl.BlockSpec((tm, w), lambda i: (i, 0))
    full = lambda x: pl.BlockSpec(x.shape, lambda i: (0, 0))
    return pl.pallas_call(
        _out_router_kernel,
        grid=(T // tm,),
        in_specs=[row(D), row(a.shape[1]), row(b.shape[1]), full(wa), full(wb), full(g), full(rcat)],
        out_specs=[row(D), row(LANES)],
        out_shape=[jax.ShapeDtypeStruct((T, D), F32), jax.ShapeDtypeStruct((T, LANES), F32)],
        compiler_params=_cparams(("parallel",)),
        name="mixer_out_router",
    )(h, a, b, wa, wb, g, rcat)


def _rank_kernel(route_ref, rank_ref, counts_ref):
    tm = route_ref.shape[0]

    @pl.when(pl.program_id(0) == 0)
    def _():
        counts_ref[...] = jnp.zeros_like(counts_ref)

    route = route_ref[...]
    lane = lax.broadcasted_iota(jnp.int32, route.shape, 1).astype(F32)
    oh0 = jnp.where(lane == route[:, 0:1], 1.0, 0.0)
    oh1 = jnp.where(lane == route[:, 1:2], 1.0, 0.0)
    both = oh0 + oh1
    earlier = (lax.broadcasted_iota(jnp.int32, (tm, tm), 1) < lax.broadcasted_iota(jnp.int32, (tm, tm), 0))
    prefix = _dot(earlier.astype(BF16), both.astype(BF16)) + counts_ref[0:1, :]
    r0 = jnp.sum(oh0 * prefix, axis=-1, keepdims=True)
    r1 = jnp.sum(oh1 * prefix, axis=-1, keepdims=True)
    rank_ref[...] = jnp.where(lane == 0.0, r0, jnp.where(lane == 1.0, r1, 0.0))
    counts_ref[...] += jnp.sum(both, axis=0, keepdims=True)


def _routing_ranks(route):
    T = route.shape[0]
    tm = RANK_ROWS
    return pl.pallas_call(
        _rank_kernel,
        grid=(T // tm,),
        in_specs=[pl.BlockSpec((tm, LANES), lambda i: (i, 0))],
        out_specs=[pl.BlockSpec((tm, LANES), lambda i: (i, 0)), pl.BlockSpec((8, LANES), lambda i: (0, 0))],
        out_shape=[jax.ShapeDtypeStruct((T, LANES), F32), jax.ShapeDtypeStruct((8, LANES), F32)],
        compiler_params=_cparams(("arbitrary",)),
        name="routing_ranks",
    )(route)


def _wait_rows(src_hbm, dst_ref, sem):
    pltpu.make_async_copy(src_hbm.at[pl.ds(0, dst_ref.shape[0])], dst_ref, sem).wait()


def _cast_kernel(w_ref, o_ref):
    o_ref[...] = w_ref[...].astype(BF16)


def _to_bf16(w):
    E, K, N = w.shape
    blk = pl.BlockSpec((1, CAST_ROWS, N), lambda e, k: (e, k, 0))
    return pl.pallas_call(
        _cast_kernel,
        grid=(E, K // CAST_ROWS),
        in_specs=[blk],
        out_specs=blk,
        out_shape=jax.ShapeDtypeStruct(w.shape, BF16),
        compiler_params=_cparams(("parallel", "parallel")),
        name="cast_bf16",
    )(w)


def _dispatch_kernel(meta_ref, dest_ref, wg_ref, wu_ref, wd_ref, h_hbm, og_ref, ou_ref, od_ref, xg_hbm,
                     zero_ref, hbuf, hsem, sem, zsem):
    step = pl.program_id(0) * pl.num_programs(1) + pl.program_id(1)
    nsteps = pl.num_programs(0) * pl.num_programs(1)
    tpb = hbuf.shape[1]
    R = zero_ref.shape[0]
    slot = step % DISPATCH_BUFFERS

    def zero_block(start):
        return pltpu.make_async_copy(zero_ref, xg_hbm.at[pl.ds(pl.multiple_of(start, R), R)], zsem)

    def h_block(s):
        return pltpu.make_async_copy(h_hbm.at[pl.ds(pl.multiple_of(s * tpb, tpb), tpb)],
                                     hbuf.at[s % DISPATCH_BUFFERS], hsem.at[s % DISPATCH_BUFFERS])

    def wait_scatter(s):
        for k in range(TOP_K):
            pltpu.make_async_copy(hbuf.at[s % DISPATCH_BUFFERS], xg_hbm.at[pl.ds(0, tpb)],
                                  sem.at[s % DISPATCH_BUFFERS]).wait()

    cap = xg_hbm.shape[0]
    fills = [(meta_ref[N_EXPERTS + e] > 0, meta_ref[e] - R) for e in range(N_EXPERTS)]
    fills += [(meta_ref[N_EXPERTS - 1] + b * R < cap, meta_ref[N_EXPERTS - 1] + b * R) for b in range(N_EXPERTS)]

    @pl.when(step == 0)
    def _():
        h_block(step).start()
        zero_ref[...] = jnp.zeros_like(zero_ref)
        for do_start in (True, False):
            for cond, start in fills:
                @pl.when(cond)
                def _():
                    zero_block(start).start() if do_start else zero_block(start).wait()

    @pl.when(step >= DISPATCH_BUFFERS - 1)
    def _():
        wait_scatter(step - (DISPATCH_BUFFERS - 1))

    @pl.when(step + 1 < nsteps)
    def _():
        h_block(step + 1).start()

    h_block(step).wait()
    src = hbuf.at[slot]

    def scatter_group(gi, carry):
        base = gi * DISPATCH_UNROLL
        rows = [[dest_ref[0, 0, TOP_K * (base + u) + k] for k in range(TOP_K)]
                for u in range(DISPATCH_UNROLL)]
        for u in range(DISPATCH_UNROLL):
            for k in range(TOP_K):
                pltpu.make_async_copy(src.at[pl.ds(base + u, 1)], xg_hbm.at[pl.ds(rows[u][k], 1)], sem.at[slot]).start()
        return carry

    lax.fori_loop(0, tpb // DISPATCH_UNROLL, scatter_group, 0)
    og_ref[...] = wg_ref[...].astype(BF16)
    ou_ref[...] = wu_ref[...].astype(BF16)
    od_ref[...] = wd_ref[...].astype(BF16)

    @pl.when(step == nsteps - 1)
    def _():
        for back in range(DISPATCH_BUFFERS - 2, -1, -1):
            @pl.when(step - back >= 0)
            def _():
                wait_scatter(step - back)


def _dispatch_cast(meta, dest, wg, wu, wd, h, cap):
    E, D, F = wg.shape
    nsteps, _, n = dest.shape
    nk = nsteps // E
    spec = lambda w: pl.BlockSpec((1, w.shape[1] // nk, w.shape[2]), lambda e, k, m: (e, k, 0))
    hbm = pl.BlockSpec(memory_space=pl.ANY)
    nbuf = DISPATCH_BUFFERS
    return pl.pallas_call(
        _dispatch_kernel,
        grid_spec=pltpu.PrefetchScalarGridSpec(
            num_scalar_prefetch=1,
            grid=(E, nk),
            in_specs=[pl.BlockSpec((1, 1, n), lambda e, k, m: (e * nk + k, 0, 0), memory_space=pltpu.SMEM),
                      spec(wg), spec(wu), spec(wd), hbm],
            out_specs=[spec(wg), spec(wu), spec(wd), hbm],
            scratch_shapes=[pltpu.VMEM((MOE_ROWS, D), F32), pltpu.VMEM((nbuf, n // TOP_K, D), F32),
                            pltpu.SemaphoreType.DMA((nbuf,)), pltpu.SemaphoreType.DMA((nbuf,)),
                            pltpu.SemaphoreType.DMA],
        ),
        out_shape=[jax.ShapeDtypeStruct(wg.shape, BF16), jax.ShapeDtypeStruct(wu.shape, BF16),
                   jax.ShapeDtypeStruct(wd.shape, BF16), jax.ShapeDtypeStruct((cap, D), F32)],
        compiler_params=_cparams(("arbitrary", "arbitrary")),
        name="dispatch_cast",
    )(meta, dest, wg, wu, wd, h)


def _moe_kernel(sched_ref, x_ref, xnext_ref, g_ref, wg_ref, wu_ref, wd_ref, o_ref, xn_ref, acc_ref):
    i, j = pl.program_id(0), pl.program_id(1)
    nblk, nf = pl.num_programs(0), pl.num_programs(1)
    used = i < sched_ref[nblk]
    slot = i % 2
    share = x_ref.shape[0] // nf

    @pl.when((i == 0) & (j == 0))
    def _():
        xn_ref[0] = _rms(x_ref[...], g_ref[...]).astype(BF16)

    @pl.when(j == 0)
    def _():
        acc_ref[...] = jnp.zeros_like(acc_ref)

    @pl.when(used)
    def _():
        xn = xn_ref[slot]
        gate = _dot(xn, wg_ref[0])
        act = gate * (1.0 / (1.0 + jnp.exp(-gate))) * _dot(xn, wu_ref[0])
        acc_ref[...] += _dot(act.astype(BF16), wd_ref[0])
        rows = pl.ds(pl.multiple_of(j * share, share), share)
        xn_ref[1 - slot, rows, :] = _rms(xnext_ref[rows, :], g_ref[...]).astype(BF16)

    @pl.when(j == nf - 1)
    def _():
        o_ref[...] = acc_ref[...]


def _moe_experts(sched, xg, g, wg, wu, wd):
    cap, D = xg.shape
    F = wg.shape[2]
    R, nf = MOE_ROWS, MOE_FF_STEPS
    nblk, tf = cap // R, F // nf

    def blk_i(i, s):
        return jnp.minimum(i, s[nblk] - 1)

    def col_j(i, j, s):
        return jnp.where(i < s[nblk], j, nf - 1)

    return pl.pallas_call(
        _moe_kernel,
        grid_spec=pltpu.PrefetchScalarGridSpec(
            num_scalar_prefetch=1,
            grid=(nblk, nf),
            in_specs=[pl.BlockSpec((R, D), lambda i, j, s: (blk_i(i, s), 0)),
                      pl.BlockSpec((R, D), lambda i, j, s: (blk_i(i + 1, s), 0)),
                      pl.BlockSpec(g.shape, lambda i, j, s: (0, 0)),
                      pl.BlockSpec((1, D, tf), lambda i, j, s: (s[blk_i(i, s)], 0, col_j(i, j, s))),
                      pl.BlockSpec((1, D, tf), lambda i, j, s: (s[blk_i(i, s)], 0, col_j(i, j, s))),
                      pl.BlockSpec((1, tf, D), lambda i, j, s: (s[blk_i(i, s)], col_j(i, j, s), 0))],
            out_specs=pl.BlockSpec((R, D), lambda i, j, s: (i, 0)),
            scratch_shapes=[pltpu.VMEM((2, R, D), BF16), pltpu.VMEM((R, D), F32)],
        ),
        out_shape=jax.ShapeDtypeStruct((cap, D), F32),
        compiler_params=_cparams(("arbitrary", "arbitrary")),
        name="moe_experts",
    )(sched, xg, xg, g, wg, wu, wd)


def _combine_kernel(pos_ref, nxt_ref, h_ref, route_ref, g_ref, y_hbm, o_ref, ybuf0, ybuf1, sem):
    i = pl.program_id(0)
    n = ybuf0.shape[0]
    tm = n // TOP_K

    def issue(idx_ref, base, buf, s):
        for g0 in range(0, n, COMBINE_GROUP):
            rows = [idx_ref[0, 0, base + TOP_K * ((g0 + r) % tm) + (g0 + r) // tm] for r in range(COMBINE_GROUP)]
            for r in range(COMBINE_GROUP):
                pltpu.make_async_copy(y_hbm.at[pl.ds(rows[r], 1)], buf.at[pl.ds(g0 + r, 1)], sem.at[s]).start()

    def finish(buf, s, rows):
        _wait_rows(y_hbm, buf, sem.at[s])
        route = route_ref[rows, :]
        out = h_ref[rows, :] + (buf[:tm, :] * route[:, 2:3] + buf[tm:, :] * route[:, 3:4])
        o_ref[rows, :] = _rms(out, g_ref[...])

    @pl.when(i == 0)
    def _():
        issue(pos_ref, 0, ybuf0, 0)

    issue(pos_ref, n, ybuf1, 1)
    finish(ybuf0, 0, slice(0, tm))
    issue(nxt_ref, 0, ybuf0, 0)
    finish(ybuf1, 1, slice(tm, 2 * tm))

    @pl.when(i == pl.num_programs(0) - 1)
    def _():
        _wait_rows(y_hbm, ybuf0, sem.at[0])


def _combine_norm(pos, h, route, g, yg):
    T, D = h.shape
    nstep, _, n2 = pos.shape
    n = n2 // 2
    tm2 = n2 // TOP_K
    row = lambda w: pl.BlockSpec((tm2, w), lambda i: (i, 0))
    smem = lambda f: pl.BlockSpec((1, 1, n2), f, memory_space=pltpu.SMEM)
    return pl.pallas_call(
        _combine_kernel,
        grid=(nstep,),
        in_specs=[smem(lambda i: (i, 0, 0)), smem(lambda i: (jnp.minimum(i + 1, nstep - 1), 0, 0)),
                  row(D), row(LANES), pl.BlockSpec(g.shape, lambda i: (0, 0)), pl.BlockSpec(memory_space=pl.ANY)],
        out_specs=row(D),
        out_shape=jax.ShapeDtypeStruct((T, D), F32),
        scratch_shapes=[pltpu.VMEM((n, D), F32), pltpu.VMEM((n, D), F32), pltpu.SemaphoreType.DMA((2,))],
        compiler_params=_cparams(("arbitrary",)),
        name="combine_norm",
    )(pos, pos, h, route, g, yg)


def _rope_tables(seq, half):
    inv_freq = ROPE_BASE ** (-jnp.arange(half, dtype=F32) / half)
    ang = jnp.arange(seq).astype(F32)[:, None] * inv_freq[None, :]
    cos, sin = jnp.cos(ang), jnp.sin(ang)
    reps = LANES // (2 * half)
    return jnp.tile(jnp.concatenate([cos, cos], 1), (1, reps)), jnp.tile(jnp.concatenate([-sin, sin], 1), (1, reps))


def _retention_tables():
    log_gamma = jnp.log1p(-(2.0 ** (-5.0 - jnp.arange(RET_HEADS, dtype=F32))))
    idx = jnp.arange(CHUNK, dtype=F32)
    rel = idx[:, None] - idx[None, :]
    din = jnp.where(rel >= 0, jnp.exp(log_gamma[:, None, None] * jnp.maximum(rel, 0.0)), 0.0)
    qd = jnp.exp(log_gamma[:, None] * (idx + 1.0))
    kd = jnp.exp(log_gamma[:, None] * (CHUNK - 1.0 - idx))
    cd = jnp.exp(log_gamma * CHUNK)
    bc = lambda a: jnp.broadcast_to(a[:, :, None], (RET_HEADS, CHUNK, LANES))
    return din, bc(qd), bc(kd), jnp.broadcast_to(cd[:, None, None], (RET_HEADS, 1, LANES))


def kernel(x, ab_norm, ab_w_in, mla_q_norm, mla_w_uq, mla_kv_norm, mla_w_ukv, ret_gn, ab_w_out, ffn_norm,
           ffn_w_gate, ffn_w_up, ffn_w_down, cd_norm, cd_w_in, swa_sinks, cd_w_out, moe_norm, moe_router,
           moe_w_gate, moe_w_up, moe_w_down, final_norm):
    B, S, D = x.shape
    T = B * S
    h = x.reshape(T, D)
    row = lambda g: g.reshape(1, -1)

    w = ab_w_in[0]
    lat = MLA_Q_RANK + MLA_KV_RANK
    win0 = jnp.concatenate([w[:, :lat], jnp.tile(w[:, lat:lat + MLA_ROPE], (1, LANES // MLA_ROPE)),
                            w[:, lat + MLA_ROPE:]], axis=1).astype(BF16)
    wuq = mla_w_uq[0].reshape(MLA_Q_RANK, MLA_HEADS, MLA_NOPE + MLA_ROPE)
    wuq = jnp.concatenate([wuq[:, :, :MLA_NOPE].reshape(MLA_Q_RANK, -1),
                           wuq[:, :, MLA_NOPE:].reshape(MLA_Q_RANK, -1)], axis=1).astype(BF16)
    wukv = mla_w_ukv[0].reshape(MLA_KV_RANK, MLA_HEADS, MLA_NOPE + MLA_V)
    wukv = jnp.concatenate([wukv[:, :, :MLA_NOPE].reshape(MLA_KV_RANK, -1),
                            wukv[:, :, MLA_NOPE:].reshape(MLA_KV_RANK, -1)], axis=1).astype(BF16)
    cm, sm = _rope_tables(S, MLA_ROPE // 2)
    cr, sr = _rope_tables(S, RET_DK // 2)
    qnope, qrope, knope, krope, v, rq, rk, rv, rg = _proj0(
        h, row(ab_norm[0]), win0, row(mla_q_norm[0]), wuq, row(mla_kv_norm[0]), wukv, cm, sm, cr, sr, S)
    mla = _mla_attention(qnope, qrope, knope, krope, v, B, S)
    ret = _retention(rq, rk, rv, rg, row(ret_gn[0]), *_retention_tables(), B, S)
    wo = ab_w_out[0].astype(BF16)
    nm = MLA_HEADS * MLA_V
    h = _mixer_out_ffn(h, mla, ret, wo[:nm], wo[nm:], row(ffn_norm[0]), _to_bf16(ffn_w_gate)[0],
                       _to_bf16(ffn_w_up)[0], _to_bf16(ffn_w_down)[0])

    w = cd_w_in[0]
    nq = SWA_HEADS * SWA_DIM
    pair_order = np.stack([np.arange(SWA_HEADS // 2), np.arange(SWA_HEADS // 2) + SWA_HEADS // 2], 1).reshape(-1)
    cols = (pair_order[:, None] * SWA_DIM + np.arange(SWA_DIM)[None, :]).reshape(-1)
    win1 = jnp.concatenate([w[:, :nq][:, cols], w[:, nq:]], axis=1).astype(BF16)
    sq, sk, sv, bq, bk, bv = _proj1(h, row(cd_norm[0]), win1)
    slopes = 2.0 ** (-8.0 * jnp.arange(1, SWA_HEADS + 1, dtype=F32) / SWA_HEADS)
    swa = _swa_attention(jnp.concatenate([swa_sinks[0].astype(F32), slopes]), sq, sk, sv, B, S)
    sb = _sb_attention(bq, bk, bv, B, S)
    wo = cd_w_out[0]
    wo_swa = wo[:nq][cols].astype(BF16)
    router = jnp.pad(moe_router[0], ((0, 0), (0, LANES - N_EXPERTS)))
    rhi = router.astype(BF16)
    rlo = (router - rhi.astype(F32)).astype(BF16)
    h, route = _mixer_out_router(h, swa, sb, wo_swa, wo[nq:].astype(BF16), row(moe_norm[0]),
                                 jnp.concatenate([rhi, rlo], axis=1))

    R = MOE_ROWS
    TK = T * TOP_K
    ranks, counts = _routing_ranks(route)
    counts = counts[0, :N_EXPERTS].astype(jnp.int32)
    padded = ((counts + R - 1) // R) * R
    eidx = jnp.arange(N_EXPERTS, dtype=jnp.int32)
    pad_end = jnp.sum(jnp.where(eidx[None, :] <= eidx[:, None], padded[None, :], 0), axis=1)
    pad_start = pad_end - padded
    expert = route[:, :TOP_K].astype(jnp.int32)
    start_of = jnp.sum(jnp.where(expert[:, :, None] == eidx, pad_start, 0), axis=-1)
    dest = (start_of + ranks[:, :TOP_K].astype(jnp.int32)).reshape(-1)
    cap = -(-TK // R) * R + N_EXPERTS * R
    nblk = cap // R
    blk_start = jnp.arange(nblk, dtype=jnp.int32) * R
    blk_exp = jnp.minimum(jnp.sum((blk_start[:, None] >= pad_end[None, :]).astype(jnp.int32), axis=1), N_EXPERTS - 1)
    sched = jnp.concatenate([blk_exp, (pad_end[-1:] // R).astype(jnp.int32)])

    def tiles(tm):
        return dest.reshape(T // tm, 1, TOP_K * tm)

    meta = jnp.concatenate([pad_end, padded]).astype(jnp.int32)
    wg, wu, wd, xg = _dispatch_cast(meta, tiles(T // DISPATCH_STEPS), moe_w_gate[0], moe_w_up[0], moe_w_down[0],
                                    h, cap)
    yg = _moe_experts(sched, xg, row(moe_norm[0]), wg, wu, wd)
    pos = tiles(2 * COMBINE_ROWS)
    out = _combine_norm(pos, h, route, row(final_norm), yg)
    return out.reshape(B, S, D)
```

```python
import functools

import jax
import jax.numpy as jnp
import numpy as np
from jax import lax
from jax.experimental import pallas as pl
from jax.experimental.pallas import tpu as pltpu

F32 = jnp.float32
BF16 = jnp.bfloat16

LANES = 128
EPS = 1e-6
LOG2E = 1.4426950408889634
ROPE_BASE = 10000.0
CHUNK = 128
MLA_HEADS, MLA_NOPE, MLA_ROPE, MLA_V = 8, 64, 32, 64
MLA_Q_RANK, MLA_KV_RANK = 384, 256
RET_HEADS, RET_DK, RET_DV = 8, 64, 128
SWA_HEADS, SWA_KV_HEADS, SWA_DIM, WINDOW = 16, 2, 64, 128
SB_HEADS, SB_DIM = 8, 64
N_EXPERTS, TOP_K = 8, 2
MLA_NOPE_ALL, MLA_ROPE_ALL, MLA_V_ALL = MLA_HEADS * MLA_NOPE, MLA_HEADS * MLA_ROPE, MLA_HEADS * MLA_V
AB_COLS = tuple(int(c) for c in np.cumsum((0, MLA_Q_RANK, MLA_KV_RANK, LANES) + (RET_HEADS * RET_DK,) * 2
                                          + (RET_HEADS * RET_DV,) * 2))
CD_COLS = tuple(int(c) for c in np.cumsum((0, SWA_HEADS * SWA_DIM) + (SWA_KV_HEADS * SWA_DIM,) * 2
                                          + (SB_HEADS * SB_DIM,) * 3))

VMEM_LIMIT = 56 * 1024 * 1024
ROW_TILE = 512
FF_STEPS = 2
MOE_FF_STEPS = 2
CAST_ROWS = 512
COMBINE_ROWS = 256
COMBINE_GROUP = 32
RANK_ROWS = 512
DISPATCH_STEPS = 32
DISPATCH_BUFFERS = 3
DISPATCH_UNROLL = 16
MOE_ROWS = 512
MLA_TQ = 512
SB_QTILES = 2
SB_T = 256


def _cparams(sem):
    return pltpu.CompilerParams(dimension_semantics=sem, vmem_limit_bytes=VMEM_LIMIT)


def _rms(x, g):
    return x * lax.rsqrt(jnp.mean(x * x, axis=-1, keepdims=True) + EPS) * g


def _rope_slab(x, cos, sin_signed, half):
    lane = lax.broadcasted_iota(jnp.int32, x.shape, 1)
    first = (lane % (2 * half)) < half
    partner = jnp.where(first, pltpu.roll(x, LANES - half, 1), pltpu.roll(x, half, 1))
    return x * cos + partner * sin_signed


def _nt_dot(a, b):
    return lax.dot_general(a, b, (((1,), (1,)), ((), ())), preferred_element_type=F32)


def _dot(a, b):
    return jnp.dot(a, b, preferred_element_type=F32)


def _proj0_kernel(h_ref, g_ref, win_ref, qn_ref, wuq_ref, kvn_ref, wukv_ref, cm_ref, sm_ref, cr_ref, sr_ref,
                  qnope_ref, qrope_ref, knope_ref, krope_ref, v_ref, rq_ref, rk_ref, rv_ref, rg_ref):
    xn = _rms(h_ref[...], g_ref[...]).astype(BF16)

    def mm(seg):
        return _dot(xn, win_ref[:, AB_COLS[seg]:AB_COLS[seg + 1]])

    cm, sm, cr, sr = cm_ref[...], sm_ref[...], cr_ref[...], sr_ref[...]
    q_scale = (MLA_NOPE + MLA_ROPE) ** -0.5
    qh = _dot(_rms(mm(0), qn_ref[...]).astype(BF16), wuq_ref[...])
    qnope_ref[...] = (qh[:, :MLA_NOPE_ALL] * q_scale).astype(BF16)
    for s in range(MLA_ROPE_ALL // LANES):
        slab = qh[:, MLA_NOPE_ALL + LANES * s:MLA_NOPE_ALL + LANES * (s + 1)]
        qrope_ref[:, LANES * s:LANES * (s + 1)] = (_rope_slab(slab, cm, sm, MLA_ROPE // 2) * q_scale).astype(BF16)
    kvh = _dot(_rms(mm(1), kvn_ref[...]).astype(BF16), wukv_ref[...])
    knope_ref[...] = kvh[:, :MLA_NOPE_ALL].astype(BF16)
    v_ref[...] = kvh[:, MLA_NOPE_ALL:].astype(BF16)
    krope_ref[...] = _rope_slab(mm(2), cm, sm, MLA_ROPE // 2).astype(BF16)
    rq = mm(3)
    rk = mm(4)
    for s in range(RET_HEADS * RET_DK // LANES):
        sl = slice(LANES * s, LANES * (s + 1))
        rq_ref[:, sl] = _rope_slab(rq[:, sl], cr, sr, RET_DK // 2).astype(BF16)
        rk_ref[:, sl] = (_rope_slab(rk[:, sl], cr, sr, RET_DK // 2) * (RET_DK ** -0.5)).astype(BF16)
    rv_ref[...] = mm(5).astype(BF16)
    rg = mm(6)
    rg_ref[...] = (rg * (1.0 / (1.0 + jnp.exp(-rg)))).astype(BF16)


def _proj0(h, g, win, qn, wuq, kvn, wukv, cm, sm, cr, sr, seq):
    T, D = h.shape
    tm = ROW_TILE
    nseq = seq // tm
    row = lambda w: pl.BlockSpec((tm, w), lambda i: (i, 0))
    full = lambda a: pl.BlockSpec(a.shape, lambda i: (0, 0))
    pos = pl.BlockSpec((tm, LANES), lambda i: (i % nseq, 0))
    widths = (MLA_NOPE_ALL, MLA_ROPE_ALL, MLA_NOPE_ALL, LANES, MLA_V_ALL) + tuple(np.diff(AB_COLS[3:]))
    return pl.pallas_call(
        _proj0_kernel,
        grid=(T // tm,),
        in_specs=[row(D), full(g), full(win), full(qn), full(wuq), full(kvn), full(wukv), pos, pos, pos, pos],
        out_specs=[row(w) for w in widths],
        out_shape=[jax.ShapeDtypeStruct((T, w), BF16) for w in widths],
        compiler_params=_cparams(("parallel",)),
        name="proj0",
    )(h, g, win, qn, wuq, kvn, wukv, cm, sm, cr, sr)


def _mla_kernel(qn_ref, qr_ref, kn_ref, kr_ref, v_ref, o_ref, kk_ref, *, tq):
    S = qn_ref.shape[0]
    p = pl.program_id(1)
    kk_ref[:, :LANES] = kn_ref[...]
    kk_ref[:, LANES:] = kr_ref[...]
    lane = lax.broadcasted_iota(jnp.int32, (1, LANES), 1)
    quarter = (p % 2) * 2
    row = lax.broadcasted_iota(jnp.int32, (tq, tq), 0)
    col = lax.broadcasted_iota(jnp.int32, (tq, tq), 1)
    causal = col <= row
    for i in range(S // tq):
        lo, hi = i * tq, (i + 1) * tq
        qn = qn_ref[lo:hi, :]
        qr = qr_ref[lo:hi, :]
        outs = []
        for hh in range(2):
            mn = (lane // 64 == hh).astype(BF16)
            mr = (lane // 32 == quarter + hh).astype(BF16)
            qa = jnp.concatenate([qn * mn, qr * mr], axis=1)
            sd = jnp.where(causal, _nt_dot(qa, kk_ref[lo:hi, :]), -jnp.inf)
            m = jnp.max(sd, axis=-1, keepdims=True)
            if i > 0:
                so = _nt_dot(qa, kk_ref[:lo, :])
                m = jnp.maximum(m, jnp.max(so, axis=-1, keepdims=True))
            pd = jnp.exp(sd - m)
            l = jnp.sum(pd, axis=-1, keepdims=True)
            acc = _dot(pd.astype(BF16), v_ref[lo:hi, :])
            if i > 0:
                po = jnp.exp(so - m)
                l = l + jnp.sum(po, axis=-1, keepdims=True)
                acc = acc + _dot(po.astype(BF16), v_ref[:lo, :])
            outs.append(acc / l)
        o_ref[lo:hi, :] = jnp.where(lane < 64, outs[0], outs[1]).astype(BF16)


def _mla_attention(qnope, qrope, knope, krope, v, batch, seq):
    T = qnope.shape[0]
    blk = lambda f: pl.BlockSpec((seq, LANES), f)
    return pl.pallas_call(
        functools.partial(_mla_kernel, tq=MLA_TQ),
        grid=(batch, MLA_HEADS // 2),
        in_specs=[blk(lambda b, p: (b, p)), blk(lambda b, p: (b, p // 2)), blk(lambda b, p: (b, p)),
                  blk(lambda b, p: (b, 0)), blk(lambda b, p: (b, p))],
        out_specs=blk(lambda b, p: (b, p)),
        out_shape=jax.ShapeDtypeStruct((T, MLA_HEADS * MLA_V), BF16),
        scratch_shapes=[pltpu.VMEM((seq, 2 * LANES), BF16)],
        compiler_params=_cparams(("parallel", "parallel")),
        name="mla_attention",
    )(qnope, qrope, knope, krope, v)


def _ret_kernel(q_ref, k_ref, v_ref, g_ref, gn_ref, din_ref, qd_ref, kd_ref, cd_ref, o_ref):
    S = q_ref.shape[0]
    h = pl.program_id(1)
    lane = lax.broadcasted_iota(jnp.int32, (1, LANES), 1)
    qmask = (lane // RET_DK == h % 2).astype(F32)
    din = din_ref[0]
    qd = qd_ref[0] * qmask
    kd = kd_ref[0]
    cd = cd_ref[0]
    gain = gn_ref[...]
    rows = [slice(c * CHUNK, (c + 1) * CHUNK) for c in range(S // CHUNK)]
    kvs = [_dot((k_ref[r, :].astype(F32) * kd).T.astype(BF16), v_ref[r, :]) for r in rows[:-1]]
    states = [None]
    for kv in kvs:
        states.append(kv if states[-1] is None else states[-1] * cd + kv)
    for r, state in zip(rows, states):
        q = q_ref[r, :].astype(F32)
        inner = _nt_dot((q * qmask).astype(BF16), k_ref[r, :]) * din
        y = _dot(inner.astype(BF16), v_ref[r, :])
        if state is not None:
            y = y + _dot((q * qd).astype(BF16), state.astype(BF16))
        mu = jnp.mean(y, axis=-1, keepdims=True)
        var = jnp.mean(jnp.square(y - mu), axis=-1, keepdims=True)
        yn = (y - mu) * lax.rsqrt(var + EPS) * gain
        o_ref[r, :] = (g_ref[r, :].astype(F32) * yn).astype(BF16)


def _retention(rq, rk, rv, rg, gn, din, qd, kd, cd, batch, seq):
    T = rq.shape[0]
    blk = lambda f: pl.BlockSpec((seq, LANES), f)
    tab = pl.BlockSpec((1, CHUNK, LANES), lambda b, h: (h, 0, 0))
    return pl.pallas_call(
        _ret_kernel,
        grid=(batch, RET_HEADS),
        in_specs=[blk(lambda b, h: (b, h // 2)), blk(lambda b, h: (b, h // 2)), blk(lambda b, h: (b, h)),
                  blk(lambda b, h: (b, h)), pl.BlockSpec((1, LANES), lambda b, h: (0, h)), tab, tab, tab,
                  pl.BlockSpec((1, 1, LANES), lambda b, h: (h, 0, 0))],
        out_specs=blk(lambda b, h: (b, h)),
        out_shape=jax.ShapeDtypeStruct((T, RET_HEADS * RET_DV), BF16),
        compiler_params=_cparams(("parallel", "parallel")),
        name="retention",
    )(rq, rk, rv, rg, gn, din, qd, kd, cd)


def _ffn_kernel(h_ref, a_ref, b_ref, wa_ref, wb_ref, g_ref, wg_ref, wu_ref, wd_ref, o_ref, xn_ref, acc_ref):
    j = pl.program_id(1)

    @pl.when(j == 0)
    def _():
        tm = h_ref.shape[0]
        for rows in (slice(0, tm // 2), slice(tm // 2, tm)):
            hn = h_ref[rows, :] + _dot(a_ref[rows, :], wa_ref[...]) + _dot(b_ref[rows, :], wb_ref[...])
            acc_ref[rows, :] = hn
            xn_ref[rows, :] = _rms(hn, g_ref[...]).astype(BF16)

    xn = xn_ref[...]
    gate = _dot(xn, wg_ref[...])
    act = gate * (1.0 / (1.0 + jnp.exp(-gate))) * _dot(xn, wu_ref[...])
    acc_ref[...] += _dot(act.astype(BF16), wd_ref[...])

    @pl.when(j == pl.num_programs(1) - 1)
    def _():
        o_ref[...] = acc_ref[...]


def _mixer_out_ffn(h, a, b, wa, wb, g, wg, wu, wd):
    T, D = h.shape
    F = wg.shape[1]
    tm, nf = ROW_TILE, FF_STEPS
    tf = F // nf
    row = lambda w: pl.BlockSpec((tm, w), lambda i, j: (i, 0))
    full = lambda x: pl.BlockSpec(x.shape, lambda i, j: (0, 0))
    return pl.pallas_call(
        _ffn_kernel,
        grid=(T // tm, nf),
        in_specs=[row(D), row(a.shape[1]), row(b.shape[1]), full(wa), full(wb), full(g),
                  pl.BlockSpec((D, tf), lambda i, j: (0, j)), pl.BlockSpec((D, tf), lambda i, j: (0, j)),
                  pl.BlockSpec((tf, D), lambda i, j: (j, 0))],
        out_specs=row(D),
        out_shape=jax.ShapeDtypeStruct((T, D), F32),
        scratch_shapes=[pltpu.VMEM((tm, D), BF16), pltpu.VMEM((tm, D), F32)],
        compiler_params=_cparams(("parallel", "arbitrary")),
        name="mixer_out_ffn",
    )(h, a, b, wa, wb, g, wg, wu, wd)


def _proj1_kernel(h_ref, g_ref, win_ref, sq_ref, sk_ref, sv_ref, bq_ref, bk_ref, bv_ref):
    xn = _rms(h_ref[...], g_ref[...]).astype(BF16)

    def mm(seg):
        return _dot(xn, win_ref[:, CD_COLS[seg]:CD_COLS[seg + 1]])

    sq_ref[...] = (mm(0) * (SWA_DIM ** -0.5)).astype(BF16)
    sk_ref[...] = mm(1).astype(BF16)
    sv_ref[...] = mm(2).astype(BF16)
    bq_ref[...] = (mm(3) * (SB_DIM ** -0.5)).astype(BF16)
    bk_ref[...] = mm(4).astype(BF16)
    bv_ref[...] = mm(5).astype(BF16)


def _proj1(h, g, win):
    T, D = h.shape
    tm = ROW_TILE
    row = lambda w: pl.BlockSpec((tm, w), lambda i: (i, 0))
    full = lambda a: pl.BlockSpec(a.shape, lambda i: (0, 0))
    widths = tuple(np.diff(CD_COLS))
    return pl.pallas_call(
        _proj1_kernel,
        grid=(T // tm,),
        in_specs=[row(D), full(g), full(win)],
        out_specs=[row(w) for w in widths],
        out_shape=[jax.ShapeDtypeStruct((T, w), BF16) for w in widths],
        compiler_params=_cparams(("parallel",)),
        name="proj1",
    )(h, g, win)


def _swa_kernel(tab_ref, q_ref, k_ref, v_ref, o_ref):
    S = q_ref.shape[0]
    W = WINDOW
    j = pl.program_id(1)
    lane = lax.broadcasted_iota(jnp.int32, (1, LANES), 1)
    dist2 = (lax.broadcasted_iota(jnp.int32, (W, 2 * W), 0) + W) - lax.broadcasted_iota(jnp.int32, (W, 2 * W), 1)
    dist1 = dist2[:, W:]
    heads = []
    for hh in range(2):
        head = j + (SWA_HEADS // 2) * hh
        sink, slope = tab_ref[head], tab_ref[SWA_HEADS + head]
        bias = [jnp.where((d >= 0) & (d < W), slope * d.astype(F32), jnp.inf) for d in (dist1, dist2)]
        heads.append((sink, bias, (lane // 64 == hh).astype(BF16)))
    for i in range(S // W):
        q = q_ref[i * W:(i + 1) * W, :]
        lo = max(i - 1, 0) * W
        k, v = k_ref[lo:(i + 1) * W, :], v_ref[lo:(i + 1) * W, :]
        outs = []
        for sink, bias, mask in heads:
            s = _nt_dot(q * mask, k) - bias[min(i, 1)]
            m = jnp.maximum(jnp.max(s, axis=-1, keepdims=True), sink)
            p = jnp.exp(s - m)
            den = jnp.sum(p, axis=-1, keepdims=True) + jnp.exp(sink - m)
            outs.append(_dot(p.astype(BF16), v) / den)
        o_ref[i * W:(i + 1) * W, :] = jnp.where(lane < 64, outs[0], outs[1]).astype(BF16)


def _swa_attention(tab, sq, sk, sv, batch, seq):
    T = sq.shape[0]
    blk = lambda f: pl.BlockSpec((seq, LANES), f)
    return pl.pallas_call(
        _swa_kernel,
        grid_spec=pltpu.PrefetchScalarGridSpec(
            num_scalar_prefetch=1,
            grid=(batch, SWA_HEADS // 2),
            in_specs=[blk(lambda b, j, t: (b, j)), blk(lambda b, j, t: (b, 0)), blk(lambda b, j, t: (b, 0))],
            out_specs=blk(lambda b, j, t: (b, j)),
        ),
        out_shape=jax.ShapeDtypeStruct((T, SWA_HEADS * SWA_DIM), BF16),
        compiler_params=_cparams(("parallel", "parallel")),
        name="swa_attention",
    )(tab, sq, sk, sv)


def _sb_kernel(q_ref, k_ref, v_ref, o_ref, *, t):
    S = q_ref.shape[0]
    lane = lax.broadcasted_iota(jnp.int32, (1, LANES), 1)
    row = lax.broadcasted_iota(jnp.int32, (t, t), 0)
    col = lax.broadcasted_iota(jnp.int32, (t, t), 1)
    before = col < row
    neg_later = jnp.where(row > col, -1.0, 0.0).astype(BF16)

    def softplus(z):
        return jnp.maximum(z, 0.0) + jnp.log(1.0 + jnp.exp2(jnp.abs(z) * (-LOG2E)))

    def diag_tile(qm, rows):
        z = _nt_dot(qm, k_ref[rows, :])
        sp = jnp.where(before, softplus(z), 0.0)
        a = jnp.where(before, jnp.exp((z - sp) + _dot(sp.astype(BF16), neg_later)), 0.0)
        return _dot(a.astype(BF16), v_ref[rows, :]), -jnp.sum(sp, axis=-1, keepdims=True)

    def full_tile(qm, krows, acc, run):
        z = _nt_dot(qm, k_ref[krows, :])
        sp = softplus(z)
        a = jnp.exp((z - sp) + (_dot(sp.astype(BF16), neg_later) + run))
        return acc + _dot(a.astype(BF16), v_ref[krows, :]), run - jnp.sum(sp, axis=-1, keepdims=True)

    m = SB_QTILES
    for i in range(S // (m * t)):
        lo = i * m * t
        q = q_ref[lo:lo + m * t, :]
        outs = []
        for hh in range(2):
            qm = q * (lane // 64 == hh).astype(BF16)
            accs, runs = [], []
            for r in range(m):
                qs = qm[r * t:(r + 1) * t]
                acc, run = diag_tile(qs, slice(lo + r * t, lo + (r + 1) * t))
                for jj in range(r - 1, -1, -1):
                    acc, run = full_tile(qs, slice(lo + jj * t, lo + (jj + 1) * t), acc, run)
                accs.append(acc)
                runs.append(run)
            acc = jnp.concatenate(accs, axis=0)
            run = jnp.concatenate(runs, axis=0)
            for j in range(i * m - 1, -1, -1):
                acc, run = full_tile(qm, slice(j * t, (j + 1) * t), acc, run)
            outs.append(acc)
        o_ref[lo:lo + m * t, :] = jnp.where(lane < 64, outs[0], outs[1]).astype(BF16)


def _sb_attention(bq, bk, bv, batch, seq):
    T = bq.shape[0]
    blk = pl.BlockSpec((seq, LANES), lambda b, p: (b, p))
    return pl.pallas_call(
        functools.partial(_sb_kernel, t=SB_T),
        grid=(batch, SB_HEADS // 2),
        in_specs=[blk, blk, blk],
        out_specs=blk,
        out_shape=jax.ShapeDtypeStruct((T, SB_HEADS * SB_DIM), BF16),
        compiler_params=_cparams(("parallel", "parallel")),
        name="sb_attention",
    )(bq, bk, bv)


def _split_bf16(x):
    hi = x.astype(BF16)
    return hi, (x - hi.astype(F32)).astype(BF16)


def _out_router_kernel(h_ref, a_ref, b_ref, wa_ref, wb_ref, g_ref, rcat_ref, o_ref, route_ref):
    tm = h_ref.shape[0]
    for rows in (slice(0, tm // 2), slice(tm // 2, tm)):
        hn = h_ref[rows, :] + _dot(a_ref[rows, :], wa_ref[...]) + _dot(b_ref[rows, :], wb_ref[...])
        o_ref[rows, :] = hn
        xhi, xlo = _split_bf16(_rms(hn, g_ref[...]))
        both = _dot(xhi, rcat_ref[...])
        logits = both[:, :LANES] + (both[:, LANES:] + _dot(xlo, rcat_ref[:, :LANES]))
        lane = lax.broadcasted_iota(jnp.int32, logits.shape, 1)
        logits = jnp.where(lane < N_EXPERTS, logits, -jnp.inf)
        m1 = jnp.max(logits, axis=-1, keepdims=True)
        i1 = jnp.min(jnp.where(logits == m1, lane, LANES), axis=-1, keepdims=True)
        rest = jnp.where(lane == i1, -jnp.inf, logits)
        m2 = jnp.max(rest, axis=-1, keepdims=True)
        i2 = jnp.min(jnp.where(rest == m2, lane, LANES), axis=-1, keepdims=True)
        e2 = jnp.exp(m2 - m1)
        den = 1.0 + e2
        route = jnp.where(lane == 0, i1.astype(F32), 0.0)
        route = jnp.where(lane == 1, i2.astype(F32), route)
        route = jnp.where(lane == 2, 1.0 / den, route)
        route = jnp.where(lane == 3, e2 / den, route)
        route_ref[rows, :] = route


def _mixer_out_router(h, a, b, wa, wb, g, rcat):
    T, D = h.shape
    tm = ROW_TILE
    row = lambda w: pl.BlockSpec((tm, w), lambda i: (i, 0))
    full = lambda x: pl.BlockSpec(x.shape, lambda i: (0, 0))
    return pl.pallas_call(
        _out_router_kernel,
        grid=(T // tm,),
        in_specs=[row(D), row(a.shape[1]), row(b.shape[1]), full(wa), full(wb), full(g), full(rcat)],
        out_specs=[row(D), row(LANES)],
        out_shape=[jax.ShapeDtypeStruct((T, D), F32), jax.ShapeDtypeStruct((T, LANES), F32)],
        compiler_params=_cparams(("parallel",)),
        name="mixer_out_router",
    )(h, a, b, wa, wb, g, rcat)


def _rank_kernel(route_ref, rank_ref, counts_ref):
    tm = route_ref.shape[0]

    @pl.when(pl.program_id(0) == 0)
    def _():
        counts_ref[...] = jnp.zeros_like(counts_ref)

    route = route_ref[...]
    lane = lax.broadcasted_iota(jnp.int32, route.shape, 1).astype(F32)
    oh0 = jnp.where(lane == route[:, 0:1], 1.0, 0.0)
    oh1 = jnp.where(lane == route[:, 1:2], 1.0, 0.0)
    both = oh0 + oh1
    earlier = (lax.broadcasted_iota(jnp.int32, (tm, tm), 1) < lax.broadcasted_iota(jnp.int32, (tm, tm), 0))
    prefix = _dot(earlier.astype(BF16), both.astype(BF16)) + counts_ref[0:1, :]
    r0 = jnp.sum(oh0 * prefix, axis=-1, keepdims=True)
    r1 = jnp.sum(oh1 * prefix, axis=-1, keepdims=True)
    rank_ref[...] = jnp.where(lane == 0.0, r0, jnp.where(lane == 1.0, r1, 0.0))
    counts_ref[...] += jnp.sum(both, axis=0, keepdims=True)


def _routing_ranks(route):
    T = route.shape[0]
    tm = RANK_ROWS
    return pl.pallas_call(
        _rank_kernel,
        grid=(T // tm,),
        in_specs=[pl.BlockSpec((tm, LANES), lambda i: (i, 0))],
        out_specs=[pl.BlockSpec((tm, LANES), lambda i: (i, 0)), pl.BlockSpec((8, LANES), lambda i: (0, 0))],
        out_shape=[jax.ShapeDtypeStruct((T, LANES), F32), jax.ShapeDtypeStruct((8, LANES), F32)],
        compiler_params=_cparams(("arbitrary",)),
        name="routing_ranks",
    )(route)


def _wait_rows(src_hbm, dst_ref, sem):
    pltpu.make_async_copy(src_hbm.at[pl.ds(0, dst_ref.shape[0])], dst_ref, sem).wait()


def _cast_kernel(w_ref, o_ref):
    o_ref[...] = w_ref[...].astype(BF16)


def _to_bf16(w):
    E, K, N = w.shape
    blk = pl.BlockSpec((1, CAST_ROWS, N), lambda e, k: (e, k, 0))
    return pl.pallas_call(
        _cast_kernel,
        grid=(E, K // CAST_ROWS),
        in_specs=[blk],
        out_specs=blk,
        out_shape=jax.ShapeDtypeStruct(w.shape, BF16),
        compiler_params=_cparams(("parallel", "parallel")),
        name="cast_bf16",
    )(w)


def _dispatch_kernel(meta_ref, dest_ref, wg_ref, wu_ref, wd_ref, h_hbm, og_ref, ou_ref, od_ref, xg_hbm,
                     zero_ref, hbuf, hsem, sem, zsem):
    step = pl.program_id(0) * pl.num_programs(1) + pl.program_id(1)
    nsteps = pl.num_programs(0) * pl.num_programs(1)
    tpb = hbuf.shape[1]
    R = zero_ref.shape[0]
    slot = step % DISPATCH_BUFFERS

    def zero_block(start):
        return pltpu.make_async_copy(zero_ref, xg_hbm.at[pl.ds(pl.multiple_of(start, R), R)], zsem)

    def h_block(s):
        return pltpu.make_async_copy(h_hbm.at[pl.ds(pl.multiple_of(s * tpb, tpb), tpb)],
                                     hbuf.at[s % DISPATCH_BUFFERS], hsem.at[s % DISPATCH_BUFFERS])

    def wait_scatter(s):
        for k in range(TOP_K):
            pltpu.make_async_copy(hbuf.at[s % DISPATCH_BUFFERS], xg_hbm.at[pl.ds(0, tpb)],
                                  sem.at[s % DISPATCH_BUFFERS]).wait()

    cap = xg_hbm.shape[0]
    fills = [(meta_ref[N_EXPERTS + e] > 0, meta_ref[e] - R) for e in range(N_EXPERTS)]
    fills += [(meta_ref[N_EXPERTS - 1] + b * R < cap, meta_ref[N_EXPERTS - 1] + b * R) for b in range(N_EXPERTS)]

    @pl.when(step == 0)
    def _():
        h_block(step).start()
        zero_ref[...] = jnp.zeros_like(zero_ref)
        for do_start in (True, False):
            for cond, start in fills:
                @pl.when(cond)
                def _():
                    zero_block(start).start() if do_start else zero_block(start).wait()

    @pl.when(step >= DISPATCH_BUFFERS - 1)
    def _():
        wait_scatter(step - (DISPATCH_BUFFERS - 1))

    @pl.when(step + 1 < nsteps)
    def _():
        h_block(step + 1).start()

    h_block(step).wait()
    src = hbuf.at[slot]

    def scatter_group(gi, carry):
        base = gi * DISPATCH_UNROLL
        rows = [[dest_ref[0, 0, TOP_K * (base + u) + k] for k in range(TOP_K)]
                for u in range(DISPATCH_UNROLL)]
        for u in range(DISPATCH_UNROLL):
            for k in range(TOP_K):
                pltpu.make_async_copy(src.at[pl.ds(base + u, 1)], xg_hbm.at[pl.ds(rows[u][k], 1)], sem.at[slot]).start()
        return carry

    lax.fori_loop(0, tpb // DISPATCH_UNROLL, scatter_group, 0)
    og_ref[...] = wg_ref[...].astype(BF16)
    ou_ref[...] = wu_ref[...].astype(BF16)
    od_ref[...] = wd_ref[...].astype(BF16)

    @pl.when(step == nsteps - 1)
    def _():
        for back in range(DISPATCH_BUFFERS - 2, -1, -1):
            @pl.when(step - back >= 0)
            def _():
                wait_scatter(step - back)


def _dispatch_cast(meta, dest, wg, wu, wd, h, cap):
    E, D, F = wg.shape
    nsteps, _, n = dest.shape
    nk = nsteps // E
    spec = lambda w: pl.BlockSpec((1, w.shape[1] // nk, w.shape[2]), lambda e, k, m: (e, k, 0))
    hbm = pl.BlockSpec(memory_space=pl.ANY)
    nbuf = DISPATCH_BUFFERS
    return pl.pallas_call(
        _dispatch_kernel,
        grid_spec=pltpu.PrefetchScalarGridSpec(
            num_scalar_prefetch=1,
            grid=(E, nk),
            in_specs=[pl.BlockSpec((1, 1, n), lambda e, k, m: (e * nk + k, 0, 0), memory_space=pltpu.SMEM),
                      spec(wg), spec(wu), spec(wd), hbm],
            out_specs=[spec(wg), spec(wu), spec(wd), hbm],
            scratch_shapes=[pltpu.VMEM((MOE_ROWS, D), F32), pltpu.VMEM((nbuf, n // TOP_K, D), F32),
                            pltpu.SemaphoreType.DMA((nbuf,)), pltpu.SemaphoreType.DMA((nbuf,)),
                            pltpu.SemaphoreType.DMA],
        ),
        out_shape=[jax.ShapeDtypeStruct(wg.shape, BF16), jax.ShapeDtypeStruct(wu.shape, BF16),
                   jax.ShapeDtypeStruct(wd.shape, BF16), jax.ShapeDtypeStruct((cap, D), F32)],
        compiler_params=_cparams(("arbitrary", "arbitrary")),
        name="dispatch_cast",
    )(meta, dest, wg, wu, wd, h)


def _moe_kernel(sched_ref, x_ref, xnext_ref, g_ref, wg_ref, wu_ref, wd_ref, o_ref, xn_ref, acc_ref):
    i, j = pl.program_id(0), pl.program_id(1)
    nblk, nf = pl.num_programs(0), pl.num_programs(1)
    used = i < sched_ref[nblk]
    slot = i % 2
    share = x_ref.shape[0] // nf

    @pl.when((i == 0) & (j == 0))
    def _():
        xn_ref[0] = _rms(x_ref[...], g_ref[...]).astype(BF16)

    @pl.when(j == 0)
    def _():
        acc_ref[...] = jnp.zeros_like(acc_ref)

    @pl.when(used)
    def _():
        xn = xn_ref[slot]
        gate = _dot(xn, wg_ref[0])
        act = gate * (1.0 / (1.0 + jnp.exp(-gate))) * _dot(xn, wu_ref[0])
        acc_ref[...] += _dot(act.astype(BF16), wd_ref[0])
        rows = pl.ds(pl.multiple_of(j * share, share), share)
        xn_ref[1 - slot, rows, :] = _rms(xnext_ref[rows, :], g_ref[...]).astype(BF16)

    @pl.when(j == nf - 1)
    def _():
        o_ref[...] = acc_ref[...]


def _moe_experts(sched, xg, g, wg, wu, wd):
    cap, D = xg.shape
    F = wg.shape[2]
    R, nf = MOE_ROWS, MOE_FF_STEPS
    nblk, tf = cap // R, F // nf

    def blk_i(i, s):
        return jnp.minimum(i, s[nblk] - 1)

    def col_j(i, j, s):
        return jnp.where(i < s[nblk], j, nf - 1)

    return pl.pallas_call(
        _moe_kernel,
        grid_spec=pltpu.PrefetchScalarGridSpec(
            num_scalar_prefetch=1,
            grid=(nblk, nf),
            in_specs=[pl.BlockSpec((R, D), lambda i, j, s: (blk_i(i, s), 0)),
                      pl.BlockSpec((R, D), lambda i, j, s: (blk_i(i + 1, s), 0)),
                      pl.BlockSpec(g.shape, lambda i, j, s: (0, 0)),
                      pl.BlockSpec((1, D, tf), lambda i, j, s: (s[blk_i(i, s)], 0, col_j(i, j, s))),
                      pl.BlockSpec((1, D, tf), lambda i, j, s: (s[blk_i(i, s)], 0, col_j(i, j, s))),
                      pl.BlockSpec((1, tf, D), lambda i, j, s: (s[blk_i(i, s)], col_j(i, j, s), 0))],
            out_specs=pl.BlockSpec((R, D), lambda i, j, s: (i, 0)),
            scratch_shapes=[pltpu.VMEM((2, R, D), BF16), pltpu.VMEM((R, D), F32)],
        ),
        out_shape=jax.ShapeDtypeStruct((cap, D), F32),
        compiler_params=_cparams(("arbitrary", "arbitrary")),
        name="moe_experts",
    )(sched, xg, xg, g, wg, wu, wd)


def _combine_kernel(pos_ref, nxt_ref, h_ref, route_ref, g_ref, y_hbm, o_ref, ybuf0, ybuf1, sem):
    i = pl.program_id(0)
    n = ybuf0.shape[0]
    tm = n // TOP_K

    def issue(idx_ref, base, buf, s):
        for g0 in range(0, n, COMBINE_GROUP):
            rows = [idx_ref[0, 0, base + TOP_K * ((g0 + r) % tm) + (g0 + r) // tm] for r in range(COMBINE_GROUP)]
            for r in range(COMBINE_GROUP):
                pltpu.make_async_copy(y_hbm.at[pl.ds(rows[r], 1)], buf.at[pl.ds(g0 + r, 1)], sem.at[s]).start()

    def finish(buf, s, rows):
        _wait_rows(y_hbm, buf, sem.at[s])
        route = route_ref[rows, :]
        out = h_ref[rows, :] + (buf[:tm, :] * route[:, 2:3] + buf[tm:, :] * route[:, 3:4])
        o_ref[rows, :] = _rms(out, g_ref[...])

    @pl.when(i == 0)
    def _():
        issue(pos_ref, 0, ybuf0, 0)

    issue(pos_ref, n, ybuf1, 1)
    finish(ybuf0, 0, slice(0, tm))
    issue(nxt_ref, 0, ybuf0, 0)
    finish(ybuf1, 1, slice(tm, 2 * tm))

    @pl.when(i == pl.num_programs(0) - 1)
    def _():
        _wait_rows(y_hbm, ybuf0, sem.at[0])


def _combine_norm(pos, h, route, g, yg):
    T, D = h.shape
    nstep, _, n2 = pos.shape
    n = n2 // 2
    tm2 = n2 // TOP_K
    row = lambda w: pl.BlockSpec((tm2, w), lambda i: (i, 0))
    smem = lambda f: pl.BlockSpec((1, 1, n2), f, memory_space=pltpu.SMEM)
    return pl.pallas_call(
        _combine_kernel,
        grid=(nstep,),
        in_specs=[smem(lambda i: (i, 0, 0)), smem(lambda i: (jnp.minimum(i + 1, nstep - 1), 0, 0)),
                  row(D), row(LANES), pl.BlockSpec(g.shape, lambda i: (0, 0)), pl.BlockSpec(memory_space=pl.ANY)],
        out_specs=row(D),
        out_shape=jax.ShapeDtypeStruct((T, D), F32),
        scratch_shapes=[pltpu.VMEM((n, D), F32), pltpu.VMEM((n, D), F32), pltpu.SemaphoreType.DMA((2,))],
        compiler_params=_cparams(("arbitrary",)),
        name="combine_norm",
    )(pos, pos, h, route, g, yg)


def _rope_tables(seq, half):
    inv_freq = ROPE_BASE ** (-jnp.arange(half, dtype=F32) / half)
    ang = jnp.arange(seq).astype(F32)[:, None] * inv_freq[None, :]
    cos, sin = jnp.cos(ang), jnp.sin(ang)
    reps = LANES // (2 * half)
    return jnp.tile(jnp.concatenate([cos, cos], 1), (1, reps)), jnp.tile(jnp.concatenate([-sin, sin], 1), (1, reps))


def _retention_tables():
    log_gamma = jnp.log1p(-(2.0 ** (-5.0 - jnp.arange(RET_HEADS, dtype=F32))))
    idx = jnp.arange(CHUNK, dtype=F32)
    rel = idx[:, None] - idx[None, :]
    din = jnp.where(rel >= 0, jnp.exp(log_gamma[:, None, None] * jnp.maximum(rel, 0.0)), 0.0)
    qd = jnp.exp(log_gamma[:, None] * (idx + 1.0))
    kd = jnp.exp(log_gamma[:, None] * (CHUNK - 1.0 - idx))
    cd = jnp.exp(log_gamma * CHUNK)
    bc = lambda a: jnp.broadcast_to(a[:, :, None], (RET_HEADS, CHUNK, LANES))
    return din, bc(qd), bc(kd), jnp.broadcast_to(cd[:, None, None], (RET_HEADS, 1, LANES))


def kernel(x, ab_norm, ab_w_in, mla_q_norm, mla_w_uq, mla_kv_norm, mla_w_ukv, ret_gn, ab_w_out, ffn_norm,
           ffn_w_gate, ffn_w_up, ffn_w_down, cd_norm, cd_w_in, swa_sinks, cd_w_out, moe_norm, moe_router,
           moe_w_gate, moe_w_up, moe_w_down, final_norm):
    B, S, D = x.shape
    T = B * S
    h = x.reshape(T, D)
    row = lambda g: g.reshape(1, -1)

    w = ab_w_in[0]
    lat = MLA_Q_RANK + MLA_KV_RANK
    win0 = jnp.concatenate([w[:, :lat], jnp.tile(w[:, lat:lat + MLA_ROPE], (1, LANES // MLA_ROPE)),
                            w[:, lat + MLA_ROPE:]], axis=1).astype(BF16)
    wuq = mla_w_uq[0].reshape(MLA_Q_RANK, MLA_HEADS, MLA_NOPE + MLA_ROPE)
    wuq = jnp.concatenate([wuq[:, :, :MLA_NOPE].reshape(MLA_Q_RANK, -1),
                           wuq[:, :, MLA_NOPE:].reshape(MLA_Q_RANK, -1)], axis=1).astype(BF16)
    wukv = mla_w_ukv[0].reshape(MLA_KV_RANK, MLA_HEADS, MLA_NOPE + MLA_V)
    wukv = jnp.concatenate([wukv[:, :, :MLA_NOPE].reshape(MLA_KV_RANK, -1),
                            wukv[:, :, MLA_NOPE:].reshape(MLA_KV_RANK, -1)], axis=1).astype(BF16)
    cm, sm = _rope_tables(S, MLA_ROPE // 2)
    cr, sr = _rope_tables(S, RET_DK // 2)
    qnope, qrope, knope, krope, v, rq, rk, rv, rg = _proj0(
        h, row(ab_norm[0]), win0, row(mla_q_norm[0]), wuq, row(mla_kv_norm[0]), wukv, cm, sm, cr, sr, S)
    mla = _mla_attention(qnope, qrope, knope, krope, v, B, S)
    ret = _retention(rq, rk, rv, rg, row(ret_gn[0]), *_retention_tables(), B, S)
    wo = ab_w_out[0].astype(BF16)
    nm = MLA_HEADS * MLA_V
    h = _mixer_out_ffn(h, mla, ret, wo[:nm], wo[nm:], row(ffn_norm[0]), _to_bf16(ffn_w_gate)[0],
                       _to_bf16(ffn_w_up)[0], _to_bf16(ffn_w_down)[0])

    w = cd_w_in[0]
    nq = SWA_HEADS * SWA_DIM
    pair_order = np.stack([np.arange(SWA_HEADS // 2), np.arange(SWA_HEADS // 2) + SWA_HEADS // 2], 1).reshape(-1)
    cols = (pair_order[:, None] * SWA_DIM + np.arange(SWA_DIM)[None, :]).reshape(-1)
    win1 = jnp.concatenate([w[:, :nq][:, cols], w[:, nq:]], axis=1).astype(BF16)
    sq, sk, sv, bq, bk, bv = _proj1(h, row(cd_norm[0]), win1)
    slopes = 2.0 ** (-8.0 * jnp.arange(1, SWA_HEADS + 1, dtype=F32) / SWA_HEADS)
    swa = _swa_attention(jnp.concatenate([swa_sinks[0].astype(F32), slopes]), sq, sk, sv, B, S)
    sb = _sb_attention(bq, bk, bv, B, S)
    wo = cd_w_out[0]
    wo_swa = wo[:nq][cols].astype(BF16)
    router = jnp.pad(moe_router[0], ((0, 0), (0, LANES - N_EXPERTS)))
    rhi = router.astype(BF16)
    rlo = (router - rhi.astype(F32)).astype(BF16)
    h, route = _mixer_out_router(h, swa, sb, wo_swa, wo[nq:].astype(BF16), row(moe_norm[0]),
                                 jnp.concatenate([rhi, rlo], axis=1))

    R = MOE_ROWS
    TK = T * TOP_K
    ranks, counts = _routing_ranks(route)
    counts = counts[0, :N_EXPERTS].astype(jnp.int32)
    padded = ((counts + R - 1) // R) * R
    eidx = jnp.arange(N_EXPERTS, dtype=jnp.int32)
    pad_end = jnp.sum(jnp.where(eidx[None, :] <= eidx[:, None], padded[None, :], 0), axis=1)
    pad_start = pad_end - padded
    expert = route[:, :TOP_K].astype(jnp.int32)
    start_of = jnp.sum(jnp.where(expert[:, :, None] == eidx, pad_start, 0), axis=-1)
    dest = (start_of + ranks[:, :TOP_K].astype(jnp.int32)).reshape(-1)
    cap = -(-TK // R) * R + N_EXPERTS * R
    nblk = cap // R
    blk_start = jnp.arange(nblk, dtype=jnp.int32) * R
    blk_exp = jnp.minimum(jnp.sum((blk_start[:, None] >= pad_end[None, :]).astype(jnp.int32), axis=1), N_EXPERTS - 1)
    sched = jnp.concatenate([blk_exp, (pad_end[-1:] // R).astype(jnp.int32)])

    def tiles(tm):
        return dest.reshape(T // tm, 1, TOP_K * tm)

    meta = jnp.concatenate([pad_end, padded]).astype(jnp.int32)
    wg, wu, wd, xg = _dispatch_cast(meta, tiles(T // DISPATCH_STEPS), moe_w_gate[0], moe_w_up[0], moe_w_down[0],
                                    h, cap)
    yg = _moe_experts(sched, xg, row(moe_norm[0]), wg, wu, wd)
    pos = tiles(2 * COMBINE_ROWS)
    out = _combine_norm(pos, h, route, row(final_norm), yg)
    return out.reshape(B, S, D)
```

```python
import functools

import jax
import jax.numpy as jnp
import numpy as np
from jax import lax
from jax.experimental import pallas as pl
from jax.experimental.pallas import tpu as pltpu

F32 = jnp.float32
BF16 = jnp.bfloat16

LANES = 128
EPS = 1e-6
LOG2E = 1.4426950408889634
ROPE_BASE = 10000.0
CHUNK = 128
MLA_HEADS, MLA_NOPE, MLA_ROPE, MLA_V = 8, 64, 32, 64
MLA_Q_RANK, MLA_KV_RANK = 384, 256
RET_HEADS, RET_DK, RET_DV = 8, 64, 128
SWA_HEADS, SWA_KV_HEADS, SWA_DIM, WINDOW = 16, 2, 64, 128
SB_HEADS, SB_DIM = 8, 64
N_EXPERTS, TOP_K = 8, 2
MLA_NOPE_ALL, MLA_ROPE_ALL, MLA_V_ALL = MLA_HEADS * MLA_NOPE, MLA_HEADS * MLA_ROPE, MLA_HEADS * MLA_V
AB_COLS = tuple(int(c) for c in np.cumsum((0, MLA_Q_RANK, MLA_KV_RANK, LANES) + (RET_HEADS * RET_DK,) * 2
                                          + (RET_HEADS * RET_DV,) * 2))
CD_COLS = tuple(int(c) for c in np.cumsum((0, SWA_HEADS * SWA_DIM) + (SWA_KV_HEADS * SWA_DIM,) * 2
                                          + (SB_HEADS * SB_DIM,) * 3))

VMEM_LIMIT = 56 * 1024 * 1024
ROW_TILE = 512
FF_STEPS = 2
MOE_FF_STEPS = 2
CAST_ROWS = 512
COMBINE_ROWS = 256
COMBINE_GROUP = 32
RANK_ROWS = 512
DISPATCH_STEPS = 32
DISPATCH_BUFFERS = 3
DISPATCH_UNROLL = 16
MOE_ROWS = 512
MLA_TQ = 512
SB_QTILES = 2
SB_T = 256


def _cparams(sem):
    return pltpu.CompilerParams(dimension_semantics=sem, vmem_limit_bytes=VMEM_LIMIT)


def _rms(x, g):
    return x * lax.rsqrt(jnp.mean(x * x, axis=-1, keepdims=True) + EPS) * g


def _rope_slab(x, cos, sin_signed, half):
    lane = lax.broadcasted_iota(jnp.int32, x.shape, 1)
    first = (lane % (2 * half)) < half
    partner = jnp.where(first, pltpu.roll(x, LANES - half, 1), pltpu.roll(x, half, 1))
    return x * cos + partner * sin_signed


def _nt_dot(a, b):
    return lax.dot_general(a, b, (((1,), (1,)), ((), ())), preferred_element_type=F32)


def _dot(a, b):
    return jnp.dot(a, b, preferred_element_type=F32)


def _proj0_kernel(h_ref, g_ref, win_ref, qn_ref, wuq_ref, kvn_ref, wukv_ref, cm_ref, sm_ref, cr_ref, sr_ref,
                  qnope_ref, qrope_ref, knope_ref, krope_ref, v_ref, rq_ref, rk_ref, rv_ref, rg_ref):
    xn = _rms(h_ref[...], g_ref[...]).astype(BF16)

    def mm(seg):
        return _dot(xn, win_ref[:, AB_COLS[seg]:AB_COLS[seg + 1]])

    cm, sm, cr, sr = cm_ref[...], sm_ref[...], cr_ref[...], sr_ref[...]
    q_scale = (MLA_NOPE + MLA_ROPE) ** -0.5
    qh = _dot(_rms(mm(0), qn_ref[...]).astype(BF16), wuq_ref[...])
    qnope_ref[...] = (qh[:, :MLA_NOPE_ALL] * q_scale).astype(BF16)
    for s in range(MLA_ROPE_ALL // LANES):
        slab = qh[:, MLA_NOPE_ALL + LANES * s:MLA_NOPE_ALL + LANES * (s + 1)]
        qrope_ref[:, LANES * s:LANES * (s + 1)] = (_rope_slab(slab, cm, sm, MLA_ROPE // 2) * q_scale).astype(BF16)
    kvh = _dot(_rms(mm(1), kvn_ref[...]).astype(BF16), wukv_ref[...])
    knope_ref[...] = kvh[:, :MLA_NOPE_ALL].astype(BF16)
    v_ref[...] = kvh[:, MLA_NOPE_ALL:].astype(BF16)
    krope_ref[...] = _rope_slab(mm(2), cm, sm, MLA_ROPE // 2).astype(BF16)
    rq = mm(3)
    rk = mm(4)
    for s in range(RET_HEADS * RET_DK // LANES):
        sl = slice(LANES * s, LANES * (s + 1))
        rq_ref[:, sl] = _rope_slab(rq[:, sl], cr, sr, RET_DK // 2).astype(BF16)
        rk_ref[:, sl] = (_rope_slab(rk[:, sl], cr, sr, RET_DK // 2) * (RET_DK ** -0.5)).astype(BF16)
    rv_ref[...] = mm(5).astype(BF16)
    rg = mm(6)
    rg_ref[...] = (rg * (1.0 / (1.0 + jnp.exp(-rg)))).astype(BF16)


def _proj0(h, g, win, qn, wuq, kvn, wukv, cm, sm, cr, sr, seq):
    T, D = h.shape
    tm = ROW_TILE
    nseq = seq // tm
    row = lambda w: pl.BlockSpec((tm, w), lambda i: (i, 0))
    full = lambda a: pl.BlockSpec(a.shape, lambda i: (0, 0))
    pos = pl.BlockSpec((tm, LANES), lambda i: (i % nseq, 0))
    widths = (MLA_NOPE_ALL, MLA_ROPE_ALL, MLA_NOPE_ALL, LANES, MLA_V_ALL) + tuple(np.diff(AB_COLS[3:]))
    return pl.pallas_call(
        _proj0_kernel,
        grid=(T // tm,),
        in_specs=[row(D), full(g), full(win), full(qn), full(wuq), full(kvn), full(wukv), pos, pos, pos, pos],
        out_specs=[row(w) for w in widths],
        out_shape=[jax.ShapeDtypeStruct((T, w), BF16) for w in widths],
        compiler_params=_cparams(("parallel",)),
        name="proj0",
    )(h, g, win, qn, wuq, kvn, wukv, cm, sm, cr, sr)


def _mla_kernel(qn_ref, qr_ref, kn_ref, kr_ref, v_ref, o_ref, kk_ref, *, tq):
    S = qn_ref.shape[0]
    p = pl.program_id(1)
    kk_ref[:, :LANES] = kn_ref[...]
    kk_ref[:, LANES:] = kr_ref[...]
    lane = lax.broadcasted_iota(jnp.int32, (1, LANES), 1)
    quarter = (p % 2) * 2
    row = lax.broadcasted_iota(jnp.int32, (tq, tq), 0)
    col = lax.broadcasted_iota(jnp.int32, (tq, tq), 1)
    causal = col <= row
    for i in range(S // tq):
        lo, hi = i * tq, (i + 1) * tq
        qn = qn_ref[lo:hi, :]
        qr = qr_ref[lo:hi, :]
        outs = []
        for hh in range(2):
            mn = (lane // 64 == hh).astype(BF16)
            mr = (lane // 32 == quarter + hh).astype(BF16)
            qa = jnp.concatenate([qn * mn, qr * mr], axis=1)
            sd = jnp.where(causal, _nt_dot(qa, kk_ref[lo:hi, :]), -jnp.inf)
            m = jnp.max(sd, axis=-1, keepdims=True)
            if i > 0:
                so = _nt_dot(qa, kk_ref[:lo, :])
                m = jnp.maximum(m, jnp.max(so, axis=-1, keepdims=True))
            pd = jnp.exp(sd - m)
            l = jnp.sum(pd, axis=-1, keepdims=True)
            acc = _dot(pd.astype(BF16), v_ref[lo:hi, :])
            if i > 0:
                po = jnp.exp(so - m)
                l = l + jnp.sum(po, axis=-1, keepdims=True)
                acc = acc + _dot(po.astype(BF16), v_ref[:lo, :])
            outs.append(acc / l)
        o_ref[lo:hi, :] = jnp.where(lane < 64, outs[0], outs[1]).astype(BF16)


def _mla_attention(qnope, qrope, knope, krope, v, batch, seq):
    T = qnope.shape[0]
    blk = lambda f: pl.BlockSpec((seq, LANES), f)
    return pl.pallas_call(
        functools.partial(_mla_kernel, tq=MLA_TQ),
        grid=(batch, MLA_HEADS // 2),
        in_specs=[blk(lambda b, p: (b, p)), blk(lambda b, p: (b, p // 2)), blk(lambda b, p: (b, p)),
                  blk(lambda b, p: (b, 0)), blk(lambda b, p: (b, p))],
        out_specs=blk(lambda b, p: (b, p)),
        out_shape=jax.ShapeDtypeStruct((T, MLA_HEADS * MLA_V), BF16),
        scratch_shapes=[pltpu.VMEM((seq, 2 * LANES), BF16)],
        compiler_params=_cparams(("parallel", "parallel")),
        name="mla_attention",
    )(qnope, qrope, knope, krope, v)


def _ret_kernel(q_ref, k_ref, v_ref, g_ref, gn_ref, din_ref, qd_ref, kd_ref, cd_ref, o_ref):
    S = q_ref.shape[0]
    h = pl.program_id(1)
    lane = lax.broadcasted_iota(jnp.int32, (1, LANES), 1)
    qmask = (lane // RET_DK == h % 2).astype(F32)
    din = din_ref[0]
    qd = qd_ref[0] * qmask
    kd = kd_ref[0]
    cd = cd_ref[0]
    gain = gn_ref[...]
    rows = [slice(c * CHUNK, (c + 1) * CHUNK) for c in range(S // CHUNK)]
    kvs = [_dot((k_ref[r, :].astype(F32) * kd).T.astype(BF16), v_ref[r, :]) for r in rows[:-1]]
    states = [None]
    for kv in kvs:
        states.append(kv if states[-1] is None else states[-1] * cd + kv)
    for r, state in zip(rows, states):
        q = q_ref[r, :].astype(F32)
        inner = _nt_dot((q * qmask).astype(BF16), k_ref[r, :]) * din
        y = _dot(inner.astype(BF16), v_ref[r, :])
        if state is not None:
            y = y + _dot((q * qd).astype(BF16), state.astype(BF16))
        mu = jnp.mean(y, axis=-1, keepdims=True)
        var = jnp.mean(jnp.square(y - mu), axis=-1, keepdims=True)
        yn = (y - mu) * lax.rsqrt(var + EPS) * gain
        o_ref[r, :] = (g_ref[r, :].astype(F32) * yn).astype(BF16)


def _retention(rq, rk, rv, rg, gn, din, qd, kd, cd, batch, seq):
    T = rq.shape[0]
    blk = lambda f: pl.BlockSpec((seq, LANES), f)
    tab = pl.BlockSpec((1, CHUNK, LANES), lambda b, h: (h, 0, 0))
    return pl.pallas_call(
        _ret_kernel,
        grid=(batch, RET_HEADS),
        in_specs=[blk(lambda b, h: (b, h // 2)), blk(lambda b, h: (b, h // 2)), blk(lambda b, h: (b, h)),
                  blk(lambda b, h: (b, h)), pl.BlockSpec((1, LANES), lambda b, h: (0, h)), tab, tab, tab,
                  pl.BlockSpec((1, 1, LANES), lambda b, h: (h, 0, 0))],
        out_specs=blk(lambda b, h: (b, h)),
        out_shape=jax.ShapeDtypeStruct((T, RET_HEADS * RET_DV), BF16),
        compiler_params=_cparams(("parallel", "parallel")),
        name="retention",
    )(rq, rk, rv, rg, gn, din, qd, kd, cd)


def _ffn_kernel(h_ref, a_ref, b_ref, wa_ref, wb_ref, g_ref, wg_ref, wu_ref, wd_ref, o_ref, xn_ref, acc_ref):
    j = pl.program_id(1)

    @pl.when(j == 0)
    def _():
        tm = h_ref.shape[0]
        for rows in (slice(0, tm // 2), slice(tm // 2, tm)):
            hn = h_ref[rows, :] + _dot(a_ref[rows, :], wa_ref[...]) + _dot(b_ref[rows, :], wb_ref[...])
            acc_ref[rows, :] = hn
            xn_ref[rows, :] = _rms(hn, g_ref[...]).astype(BF16)

    xn = xn_ref[...]
    gate = _dot(xn, wg_ref[...])
    act = gate * (1.0 / (1.0 + jnp.exp(-gate))) * _dot(xn, wu_ref[...])
    acc_ref[...] += _dot(act.astype(BF16), wd_ref[...])

    @pl.when(j == pl.num_programs(1) - 1)
    def _():
        o_ref[...] = acc_ref[...]


def _mixer_out_ffn(h, a, b, wa, wb, g, wg, wu, wd):
    T, D = h.shape
    F = wg.shape[1]
    tm, nf = ROW_TILE, FF_STEPS
    tf = F // nf
    row = lambda w: pl.BlockSpec((tm, w), lambda i, j: (i, 0))
    full = lambda x: pl.BlockSpec(x.shape, lambda i, j: (0, 0))
    return pl.pallas_call(
        _ffn_kernel,
        grid=(T // tm, nf),
        in_specs=[row(D), row(a.shape[1]), row(b.shape[1]), full(wa), full(wb), full(g),
                  pl.BlockSpec((D, tf), lambda i, j: (0, j)), pl.BlockSpec((D, tf), lambda i, j: (0, j)),
                  pl.BlockSpec((tf, D), lambda i, j: (j, 0))],
        out_specs=row(D),
        out_shape=jax.ShapeDtypeStruct((T, D), F32),
        scratch_shapes=[pltpu.VMEM((tm, D), BF16), pltpu.VMEM((tm, D), F32)],
        compiler_params=_cparams(("parallel", "arbitrary")),
        name="mixer_out_ffn",
    )(h, a, b, wa, wb, g, wg, wu, wd)


def _proj1_kernel(h_ref, g_ref, win_ref, sq_ref, sk_ref, sv_ref, bq_ref, bk_ref, bv_ref):
    xn = _rms(h_ref[...], g_ref[...]).astype(BF16)

    def mm(seg):
        return _dot(xn, win_ref[:, CD_COLS[seg]:CD_COLS[seg + 1]])

    sq_ref[...] = (mm(0) * (SWA_DIM ** -0.5)).astype(BF16)
    sk_ref[...] = mm(1).astype(BF16)
    sv_ref[...] = mm(2).astype(BF16)
    bq_ref[...] = (mm(3) * (SB_DIM ** -0.5)).astype(BF16)
    bk_ref[...] = mm(4).astype(BF16)
    bv_ref[...] = mm(5).astype(BF16)


def _proj1(h, g, win):
    T, D = h.shape
    tm = ROW_TILE
    row = lambda w: pl.BlockSpec((tm, w), lambda i: (i, 0))
    full = lambda a: pl.BlockSpec(a.shape, lambda i: (0, 0))
    widths = tuple(np.diff(CD_COLS))
    return pl.pallas_call(
        _proj1_kernel,
        grid=(T // tm,),
        in_specs=[row(D), full(g), full(win)],
        out_specs=[row(w) for w in widths],
        out_shape=[jax.ShapeDtypeStruct((T, w), BF16) for w in widths],
        compiler_params=_cparams(("parallel",)),
        name="proj1",
    )(h, g, win)


def _swa_kernel(tab_ref, q_ref, k_ref, v_ref, o_ref):
    S = q_ref.shape[0]
    W = WINDOW
    j = pl.program_id(1)
    lane = lax.broadcasted_iota(jnp.int32, (1, LANES), 1)
    r2 = lax.broadcasted_iota(jnp.int32, (2 * W, 2 * W), 0)
    c2 = lax.broadcasted_iota(jnp.int32, (2 * W, 2 * W), 1)
    upper = r2 < W
    dist2 = jnp.where(upper, r2, r2 - W) + W - c2
    dist1 = dist2[:, W:]
    heads = [j, j + SWA_HEADS // 2]
    sink = jnp.where(upper[:, :1], tab_ref[heads[0]], tab_ref[heads[1]])
    slope = jnp.where(upper[:, :1], tab_ref[SWA_HEADS + heads[0]], tab_ref[SWA_HEADS + heads[1]])
    bias = [jnp.where((d >= 0) & (d < W), slope * d.astype(F32), jnp.inf) for d in (dist1, dist2)]
    masks = [(lane // 64 == hh).astype(BF16) for hh in range(2)]
    for i in range(S // W):
        q = q_ref[i * W:(i + 1) * W, :]
        lo = max(i - 1, 0) * W
        k, v = k_ref[lo:(i + 1) * W, :], v_ref[lo:(i + 1) * W, :]
        s = _nt_dot(jnp.concatenate([q * masks[0], q * masks[1]], axis=0), k) - bias[min(i, 1)]
        m = jnp.maximum(jnp.max(s, axis=-1, keepdims=True), sink)
        p = jnp.exp(s - m)
        den = jnp.sum(p, axis=-1, keepdims=True) + jnp.exp(sink - m)
        o = _dot(p.astype(BF16), v) / den
        o_ref[i * W:(i + 1) * W, :] = jnp.where(lane < 64, o[:W], o[W:]).astype(BF16)


def _swa_attention(tab, sq, sk, sv, batch, seq):
    T = sq.shape[0]
    blk = lambda f: pl.BlockSpec((seq, LANES), f)
    return pl.pallas_call(
        _swa_kernel,
        grid_spec=pltpu.PrefetchScalarGridSpec(
            num_scalar_prefetch=1,
            grid=(batch, SWA_HEADS // 2),
            in_specs=[blk(lambda b, j, t: (b, j)), blk(lambda b, j, t: (b, 0)), blk(lambda b, j, t: (b, 0))],
            out_specs=blk(lambda b, j, t: (b, j)),
        ),
        out_shape=jax.ShapeDtypeStruct((T, SWA_HEADS * SWA_DIM), BF16),
        compiler_params=_cparams(("parallel", "parallel")),
        name="swa_attention",
    )(tab, sq, sk, sv)


def _sb_kernel(q_ref, k_ref, v_ref, o_ref, *, t):
    S = q_ref.shape[0]
    lane = lax.broadcasted_iota(jnp.int32, (1, LANES), 1)
    row = lax.broadcasted_iota(jnp.int32, (t, t), 0)
    col = lax.broadcasted_iota(jnp.int32, (t, t), 1)
    before = col < row
    neg_later = jnp.where(row > col, -1.0, 0.0).astype(BF16)

    def softplus(z):
        return jnp.maximum(z, 0.0) + jnp.log(1.0 + jnp.exp2(jnp.abs(z) * (-LOG2E)))

    def diag_tile(qm, rows):
        z = _nt_dot(qm, k_ref[rows, :])
        sp = jnp.where(before, softplus(z), 0.0)
        a = jnp.where(before, jnp.exp((z - sp) + _dot(sp.astype(BF16), neg_later)), 0.0)
        return _dot(a.astype(BF16), v_ref[rows, :]), -jnp.sum(sp, axis=-1, keepdims=True)

    def full_tile(qm, krows, acc, run):
        z = _nt_dot(qm, k_ref[krows, :])
        sp = softplus(z)
        a = jnp.exp((z - sp) + (_dot(sp.astype(BF16), neg_later) + run))
        return acc + _dot(a.astype(BF16), v_ref[krows, :]), run - jnp.sum(sp, axis=-1, keepdims=True)

    m = SB_QTILES
    for i in range(S // (m * t)):
        lo = i * m * t
        q = q_ref[lo:lo + m * t, :]
        outs = []
        for hh in range(2):
            qm = q * (lane // 64 == hh).astype(BF16)
            accs, runs = [], []
            for r in range(m):
                qs = qm[r * t:(r + 1) * t]
                acc, run = diag_tile(qs, slice(lo + r * t, lo + (r + 1) * t))
                for jj in range(r - 1, -1, -1):
                    acc, run = full_tile(qs, slice(lo + jj * t, lo + (jj + 1) * t), acc, run)
                accs.append(acc)
                runs.append(run)
            acc = jnp.concatenate(accs, axis=0)
            run = jnp.concatenate(runs, axis=0)
            for j in range(i * m - 1, -1, -1):
                acc, run = full_tile(qm, slice(j * t, (j + 1) * t), acc, run)
            outs.append(acc)
        o_ref[lo:lo + m * t, :] = jnp.where(lane < 64, outs[0], outs[1]).astype(BF16)


def _sb_attention(bq, bk, bv, batch, seq):
    T = bq.shape[0]
    blk = pl.BlockSpec((seq, LANES), lambda b, p: (b, p))
    return pl.pallas_call(
        functools.partial(_sb_kernel, t=SB_T),
        grid=(batch, SB_HEADS // 2),
        in_specs=[blk, blk, blk],
        out_specs=blk,
        out_shape=jax.ShapeDtypeStruct((T, SB_HEADS * SB_DIM), BF16),
        compiler_params=_cparams(("parallel", "parallel")),
        name="sb_attention",
    )(bq, bk, bv)


def _split_bf16(x):
    hi = x.astype(BF16)
    return hi, (x - hi.astype(F32)).astype(BF16)


def _out_router_kernel(h_ref, a_ref, b_ref, wa_ref, wb_ref, g_ref, rcat_ref, o_ref, route_ref):
    tm = h_ref.shape[0]
    for rows in (slice(0, tm // 2), slice(tm // 2, tm)):
        hn = h_ref[rows, :] + _dot(a_ref[rows, :], wa_ref[...]) + _dot(b_ref[rows, :], wb_ref[...])
        o_ref[rows, :] = hn
        xhi, xlo = _split_bf16(_rms(hn, g_ref[...]))
        both = _dot(xhi, rcat_ref[...])
        logits = both[:, :LANES] + (both[:, LANES:] + _dot(xlo, rcat_ref[:, :LANES]))
        lane = lax.broadcasted_iota(jnp.int32, logits.shape, 1)
        logits = jnp.where(lane < N_EXPERTS, logits, -jnp.inf)
        m1 = jnp.max(logits, axis=-1, keepdims=True)
        i1 = jnp.min(jnp.where(logits == m1, lane, LANES), axis=-1, keepdims=True)
        rest = jnp.where(lane == i1, -jnp.inf, logits)
        m2 = jnp.max(rest, axis=-1, keepdims=True)
        i2 = jnp.min(jnp.where(rest == m2, lane, LANES), axis=-1, keepdims=True)
        e2 = jnp.exp(m2 - m1)
        den = 1.0 + e2
        route = jnp.where(lane == 0, i1.astype(F32), 0.0)
        route = jnp.where(lane == 1, i2.astype(F32), route)
        route = jnp.where(lane == 2, 1.0 / den, route)
        route = jnp.where(lane == 3, e2 / den, route)
        route_ref[rows, :] = route


def _mixer_out_router(h, a, b, wa, wb, g, rcat):
    T, D = h.shape
    tm = ROW_TILE
    row = lambda w: pl.BlockSpec((tm, w), lambda i: (i, 0))
    full = lambda x: pl.BlockSpec(x.shape, lambda i: (0, 0))
    return pl.pallas_call(
        _out_router_kernel,
        grid=(T // tm,),
        in_specs=[row(D), row(a.shape[1]), row(b.shape[1]), full(wa), full(wb), full(g), full(rcat)],
        out_specs=[row(D), row(LANES)],
        out_shape=[jax.ShapeDtypeStruct((T, D), F32), jax.ShapeDtypeStruct((T, LANES), F32)],
        compiler_params=_cparams(("parallel",)),
        name="mixer_out_router",
    )(h, a, b, wa, wb, g, rcat)


def _rank_kernel(route_ref, rank_ref, counts_ref):
    tm = route_ref.shape[0]

    @pl.when(pl.program_id(0) == 0)
    def _():
        counts_ref[...] = jnp.zeros_like(counts_ref)

    route = route_ref[...]
    lane = lax.broadcasted_iota(jnp.int32, route.shape, 1).astype(F32)
    oh0 = jnp.where(lane == route[:, 0:1], 1.0, 0.0)
    oh1 = jnp.where(lane == route[:, 1:2], 1.0, 0.0)
    both = oh0 + oh1
    earlier = (lax.broadcasted_iota(jnp.int32, (tm, tm), 1) < lax.broadcasted_iota(jnp.int32, (tm, tm), 0))
    prefix = _dot(earlier.astype(BF16), both.astype(BF16)) + counts_ref[0:1, :]
    r0 = jnp.sum(oh0 * prefix, axis=-1, keepdims=True)
    r1 = jnp.sum(oh1 * prefix, axis=-1, keepdims=True)
    rank_ref[...] = jnp.where(lane == 0.0, r0, jnp.where(lane == 1.0, r1, 0.0))
    counts_ref[...] += jnp.sum(both, axis=0, keepdims=True)


def _routing_ranks(route):
    T = route.shape[0]
    tm = RANK_ROWS
    return pl.pallas_call(
        _rank_kernel,
        grid=(T // tm,),
        in_specs=[pl.BlockSpec((tm, LANES), lambda i: (i, 0))],
        out_specs=[pl.BlockSpec((tm, LANES), lambda i: (i, 0)), pl.BlockSpec((8, LANES), lambda i: (0, 0))],
        out_shape=[jax.ShapeDtypeStruct((T, LANES), F32), jax.ShapeDtypeStruct((8, LANES), F32)],
        compiler_params=_cparams(("arbitrary",)),
        name="routing_ranks",
    )(route)


def _wait_rows(src_hbm, dst_ref, sem):
    pltpu.make_async_copy(src_hbm.at[pl.ds(0, dst_ref.shape[0])], dst_ref, sem).wait()


def _cast_kernel(w_ref, o_ref):
    o_ref[...] = w_ref[...].astype(BF16)


def _to_bf16(w):
    E, K, N = w.shape
    blk = pl.BlockSpec((1, CAST_ROWS, N), lambda e, k: (e, k, 0))
    return pl.pallas_call(
        _cast_kernel,
        grid=(E, K // CAST_ROWS),
        in_specs=[blk],
        out_specs=blk,
        out_shape=jax.ShapeDtypeStruct(w.shape, BF16),
        compiler_params=_cparams(("parallel", "parallel")),
        name="cast_bf16",
    )(w)


def _dispatch_kernel(meta_ref, dest_ref, wg_ref, wu_ref, wd_ref, h_hbm, og_ref, ou_ref, od_ref, xg_hbm,
                     zero_ref, hbuf, hsem, sem, zsem):
    step = pl.program_id(0) * pl.num_programs(1) + pl.program_id(1)
    nsteps = pl.num_programs(0) * pl.num_programs(1)
    tpb = hbuf.shape[1]
    R = zero_ref.shape[0]
    slot = step % DISPATCH_BUFFERS

    def zero_block(start):
        return pltpu.make_async_copy(zero_ref, xg_hbm.at[pl.ds(pl.multiple_of(start, R), R)], zsem)

    def h_block(s):
        return pltpu.make_async_copy(h_hbm.at[pl.ds(pl.multiple_of(s * tpb, tpb), tpb)],
                                     hbuf.at[s % DISPATCH_BUFFERS], hsem.at[s % DISPATCH_BUFFERS])

    def wait_scatter(s):
        for k in range(TOP_K):
            pltpu.make_async_copy(hbuf.at[s % DISPATCH_BUFFERS], xg_hbm.at[pl.ds(0, tpb)],
                                  sem.at[s % DISPATCH_BUFFERS]).wait()

    cap = xg_hbm.shape[0]
    fills = [(meta_ref[N_EXPERTS + e] > 0, meta_ref[e] - R) for e in range(N_EXPERTS)]
    fills += [(meta_ref[N_EXPERTS - 1] + b * R < cap, meta_ref[N_EXPERTS - 1] + b * R) for b in range(N_EXPERTS)]

    @pl.when(step == 0)
    def _():
        h_block(step).start()
        zero_ref[...] = jnp.zeros_like(zero_ref)
        for do_start in (True, False):
            for cond, start in fills:
                @pl.when(cond)
                def _():
                    zero_block(start).start() if do_start else zero_block(start).wait()

    @pl.when(step >= DISPATCH_BUFFERS - 1)
    def _():
        wait_scatter(step - (DISPATCH_BUFFERS - 1))

    @pl.when(step + 1 < nsteps)
    def _():
        h_block(step + 1).start()

    h_block(step).wait()
    src = hbuf.at[slot]

    def scatter_group(gi, carry):
        base = gi * DISPATCH_UNROLL
        rows = [[dest_ref[0, 0, TOP_K * (base + u) + k] for k in range(TOP_K)]
                for u in range(DISPATCH_UNROLL)]
        for u in range(DISPATCH_UNROLL):
            for k in range(TOP_K):
                pltpu.make_async_copy(src.at[pl.ds(base + u, 1)], xg_hbm.at[pl.ds(rows[u][k], 1)], sem.at[slot]).start()
        return carry

    lax.fori_loop(0, tpb // DISPATCH_UNROLL, scatter_group, 0)
    og_ref[...] = wg_ref[...].astype(BF16)
    ou_ref[...] = wu_ref[...].astype(BF16)
    od_ref[...] = wd_ref[...].astype(BF16)

    @pl.when(step == nsteps - 1)
    def _():
        for back in range(DISPATCH_BUFFERS - 2, -1, -1):
            @pl.when(step - back >= 0)
            def _():
                wait_scatter(step - back)


def _dispatch_cast(meta, dest, wg, wu, wd, h, cap):
    E, D, F = wg.shape
    nsteps, _, n = dest.shape
    nk = nsteps // E
    spec = lambda w: pl.BlockSpec((1, w.shape[1] // nk, w.shape[2]), lambda e, k, m: (e, k, 0))
    hbm = pl.BlockSpec(memory_space=pl.ANY)
    nbuf = DISPATCH_BUFFERS
    return pl.pallas_call(
        _dispatch_kernel,
        grid_spec=pltpu.PrefetchScalarGridSpec(
            num_scalar_prefetch=1,
            grid=(E, nk),
            in_specs=[pl.BlockSpec((1, 1, n), lambda e, k, m: (e * nk + k, 0, 0), memory_space=pltpu.SMEM),
                      spec(wg), spec(wu), spec(wd), hbm],
            out_specs=[spec(wg), spec(wu), spec(wd), hbm],
            scratch_shapes=[pltpu.VMEM((MOE_ROWS, D), F32), pltpu.VMEM((nbuf, n // TOP_K, D), F32),
                            pltpu.SemaphoreType.DMA((nbuf,)), pltpu.SemaphoreType.DMA((nbuf,)),
                            pltpu.SemaphoreType.DMA],
        ),
        out_shape=[jax.ShapeDtypeStruct(wg.shape, BF16), jax.ShapeDtypeStruct(wu.shape, BF16),
                   jax.ShapeDtypeStruct(wd.shape, BF16), jax.ShapeDtypeStruct((cap, D), F32)],
        compiler_params=_cparams(("arbitrary", "arbitrary")),
        name="dispatch_cast",
    )(meta, dest, wg, wu, wd, h)


def _moe_kernel(sched_ref, x_ref, xnext_ref, g_ref, wg_ref, wu_ref, wd_ref, o_ref, xn_ref, acc_ref):
    i, j = pl.program_id(0), pl.program_id(1)
    nblk, nf = pl.num_programs(0), pl.num_programs(1)
    used = i < sched_ref[nblk]
    slot = i % 2
    share = x_ref.shape[0] // nf

    @pl.when((i == 0) & (j == 0))
    def _():
        xn_ref[0] = _rms(x_ref[...], g_ref[...]).astype(BF16)

    @pl.when(j == 0)
    def _():
        acc_ref[...] = jnp.zeros_like(acc_ref)

    @pl.when(used)
    def _():
        xn = xn_ref[slot]
        gate = _dot(xn, wg_ref[0])
        act = gate * (1.0 / (1.0 + jnp.exp(-gate))) * _dot(xn, wu_ref[0])
        acc_ref[...] += _dot(act.astype(BF16), wd_ref[0])
        rows = pl.ds(pl.multiple_of(j * share, share), share)
        xn_ref[1 - slot, rows, :] = _rms(xnext_ref[rows, :], g_ref[...]).astype(BF16)

    @pl.when(j == nf - 1)
    def _():
        o_ref[...] = acc_ref[...]


def _moe_experts(sched, xg, g, wg, wu, wd):
    cap, D = xg.shape
    F = wg.shape[2]
    R, nf = MOE_ROWS, MOE_FF_STEPS
    nblk, tf = cap // R, F // nf

    def blk_i(i, s):
        return jnp.minimum(i, s[nblk] - 1)

    def col_j(i, j, s):
        return jnp.where(i < s[nblk], j, nf - 1)

    return pl.pallas_call(
        _moe_kernel,
        grid_spec=pltpu.PrefetchScalarGridSpec(
            num_scalar_prefetch=1,
            grid=(nblk, nf),
            in_specs=[pl.BlockSpec((R, D), lambda i, j, s: (blk_i(i, s), 0)),
                      pl.BlockSpec((R, D), lambda i, j, s: (blk_i(i + 1, s), 0)),
                      pl.BlockSpec(g.shape, lambda i, j, s: (0, 0)),
                      pl.BlockSpec((1, D, tf), lambda i, j, s: (s[blk_i(i, s)], 0, col_j(i, j, s))),
                      pl.BlockSpec((1, D, tf), lambda i, j, s: (s[blk_i(i, s)], 0, col_j(i, j, s))),
                      pl.BlockSpec((1, tf, D), lambda i, j, s: (s[blk_i(i, s)], col_j(i, j, s), 0))],
            out_specs=pl.BlockSpec((R, D), lambda i, j, s: (i, 0)),
            scratch_shapes=[pltpu.VMEM((2, R, D), BF16), pltpu.VMEM((R, D), F32)],
        ),
        out_shape=jax.ShapeDtypeStruct((cap, D), F32),
        compiler_params=_cparams(("arbitrary", "arbitrary")),
        name="moe_experts",
    )(sched, xg, xg, g, wg, wu, wd)


def _combine_kernel(pos_ref, nxt_ref, h_ref, route_ref, g_ref, y_hbm, o_ref, ybuf0, ybuf1, sem):
    i = pl.program_id(0)
    n = ybuf0.shape[0]
    tm = n // TOP_K

    def issue(idx_ref, base, buf, s):
        for g0 in range(0, n, COMBINE_GROUP):
            rows = [idx_ref[0, 0, base + TOP_K * ((g0 + r) % tm) + (g0 + r) // tm] for r in range(COMBINE_GROUP)]
            for r in range(COMBINE_GROUP):
                pltpu.make_async_copy(y_hbm.at[pl.ds(rows[r], 1)], buf.at[pl.ds(g0 + r, 1)], sem.at[s]).start()

    def finish(buf, s, rows):
        _wait_rows(y_hbm, buf, sem.at[s])
        route = route_ref[rows, :]
        out = h_ref[rows, :] + (buf[:tm, :] * route[:, 2:3] + buf[tm:, :] * route[:, 3:4])
        o_ref[rows, :] = _rms(out, g_ref[...])

    @pl.when(i == 0)
    def _():
        issue(pos_ref, 0, ybuf0, 0)

    issue(pos_ref, n, ybuf1, 1)
    finish(ybuf0, 0, slice(0, tm))
    issue(nxt_ref, 0, ybuf0, 0)
    finish(ybuf1, 1, slice(tm, 2 * tm))

    @pl.when(i == pl.num_programs(0) - 1)
    def _():
        _wait_rows(y_hbm, ybuf0, sem.at[0])


def _combine_norm(pos, h, route, g, yg):
    T, D = h.shape
    nstep, _, n2 = pos.shape
    n = n2 // 2
    tm2 = n2 // TOP_K
    row = lambda w: pl.BlockSpec((tm2, w), lambda i: (i, 0))
    smem = lambda f: pl.BlockSpec((1, 1, n2), f, memory_space=pltpu.SMEM)
    return pl.pallas_call(
        _combine_kernel,
        grid=(nstep,),
        in_specs=[smem(lambda i: (i, 0, 0)), smem(lambda i: (jnp.minimum(i + 1, nstep - 1), 0, 0)),
                  row(D), row(LANES), pl.BlockSpec(g.shape, lambda i: (0, 0)), pl.BlockSpec(memory_space=pl.ANY)],
        out_specs=row(D),
        out_shape=jax.ShapeDtypeStruct((T, D), F32),
        scratch_shapes=[pltpu.VMEM((n, D), F32), pltpu.VMEM((n, D), F32), pltpu.SemaphoreType.DMA((2,))],
        compiler_params=_cparams(("arbitrary",)),
        name="combine_norm",
    )(pos, pos, h, route, g, yg)


def _rope_tables(seq, half):
    inv_freq = ROPE_BASE ** (-jnp.arange(half, dtype=F32) / half)
    ang = jnp.arange(seq).astype(F32)[:, None] * inv_freq[None, :]
    cos, sin = jnp.cos(ang), jnp.sin(ang)
    reps = LANES // (2 * half)
    return jnp.tile(jnp.concatenate([cos, cos], 1), (1, reps)), jnp.tile(jnp.concatenate([-sin, sin], 1), (1, reps))


def _retention_tables():
    log_gamma = jnp.log1p(-(2.0 ** (-5.0 - jnp.arange(RET_HEADS, dtype=F32))))
    idx = jnp.arange(CHUNK, dtype=F32)
    rel = idx[:, None] - idx[None, :]
    din = jnp.where(rel >= 0, jnp.exp(log_gamma[:, None, None] * jnp.maximum(rel, 0.0)), 0.0)
    qd = jnp.exp(log_gamma[:, None] * (idx + 1.0))
    kd = jnp.exp(log_gamma[:, None] * (CHUNK - 1.0 - idx))
    cd = jnp.exp(log_gamma * CHUNK)
    bc = lambda a: jnp.broadcast_to(a[:, :, None], (RET_HEADS, CHUNK, LANES))
    return din, bc(qd), bc(kd), jnp.broadcast_to(cd[:, None, None], (RET_HEADS, 1, LANES))


def kernel(x, ab_norm, ab_w_in, mla_q_norm, mla_w_uq, mla_kv_norm, mla_w_ukv, ret_gn, ab_w_out, ffn_norm,
           ffn_w_gate, ffn_w_up, ffn_w_down, cd_norm, cd_w_in, swa_sinks, cd_w_out, moe_norm, moe_router,
           moe_w_gate, moe_w_up, moe_w_down, final_norm):
    B, S, D = x.shape
    T = B * S
    h = x.reshape(T, D)
    row = lambda g: g.reshape(1, -1)

    w = ab_w_in[0]
    lat = MLA_Q_RANK + MLA_KV_RANK
    win0 = jnp.concatenate([w[:, :lat], jnp.tile(w[:, lat:lat + MLA_ROPE], (1, LANES // MLA_ROPE)),
                            w[:, lat + MLA_ROPE:]], axis=1).astype(BF16)
    wuq = mla_w_uq[0].reshape(MLA_Q_RANK, MLA_HEADS, MLA_NOPE + MLA_ROPE)
    wuq = jnp.concatenate([wuq[:, :, :MLA_NOPE].reshape(MLA_Q_RANK, -1),
                           wuq[:, :, MLA_NOPE:].reshape(MLA_Q_RANK, -1)], axis=1).astype(BF16)
    wukv = mla_w_ukv[0].reshape(MLA_KV_RANK, MLA_HEADS, MLA_NOPE + MLA_V)
    wukv = jnp.concatenate([wukv[:, :, :MLA_NOPE].reshape(MLA_KV_RANK, -1),
                            wukv[:, :, MLA_NOPE:].reshape(MLA_KV_RANK, -1)], axis=1).astype(BF16)
    cm, sm = _rope_tables(S, MLA_ROPE // 2)
    cr, sr = _rope_tables(S, RET_DK // 2)
    qnope, qrope, knope, krope, v, rq, rk, rv, rg = _proj0(
        h, row(ab_norm[0]), win0, row(mla_q_norm[0]), wuq, row(mla_kv_norm[0]), wukv, cm, sm, cr, sr, S)
    mla = _mla_attention(qnope, qrope, knope, krope, v, B, S)
    ret = _retention(rq, rk, rv, rg, row(ret_gn[0]), *_retention_tables(), B, S)
    wo = ab_w_out[0].astype(BF16)
    nm = MLA_HEADS * MLA_V
    h = _mixer_out_ffn(h, mla, ret, wo[:nm], wo[nm:], row(ffn_norm[0]), _to_bf16(ffn_w_gate)[0],
                       _to_bf16(ffn_w_up)[0], _to_bf16(ffn_w_down)[0])

    w = cd_w_in[0]
    nq = SWA_HEADS * SWA_DIM
    pair_order = np.stack([np.arange(SWA_HEADS // 2), np.arange(SWA_HEADS // 2) + SWA_HEADS // 2], 1).reshape(-1)
    cols = (pair_order[:, None] * SWA_DIM + np.arange(SWA_DIM)[None, :]).reshape(-1)
    win1 = jnp.concatenate([w[:, :nq][:, cols], w[:, nq:]], axis=1).astype(BF16)
    sq, sk, sv, bq, bk, bv = _proj1(h, row(cd_norm[0]), win1)
    slopes = 2.0 ** (-8.0 * jnp.arange(1, SWA_HEADS + 1, dtype=F32) / SWA_HEADS)
    swa = _swa_attention(jnp.concatenate([swa_sinks[0].astype(F32), slopes]), sq, sk, sv, B, S)
    sb = _sb_attention(bq, bk, bv, B, S)
    wo = cd_w_out[0]
    wo_swa = wo[:nq][cols].astype(BF16)
    router = jnp.pad(moe_router[0], ((0, 0), (0, LANES - N_EXPERTS)))
    rhi = router.astype(BF16)
    rlo = (router - rhi.astype(F32)).astype(BF16)
    h, route = _mixer_out_router(h, swa, sb, wo_swa, wo[nq:].astype(BF16), row(moe_norm[0]),
                                 jnp.concatenate([rhi, rlo], axis=1))

    R = MOE_ROWS
    TK = T * TOP_K
    ranks, counts = _routing_ranks(route)
    counts = counts[0, :N_EXPERTS].astype(jnp.int32)
    padded = ((counts + R - 1) // R) * R
    eidx = jnp.arange(N_EXPERTS, dtype=jnp.int32)
    pad_end = jnp.sum(jnp.where(eidx[None, :] <= eidx[:, None], padded[None, :], 0), axis=1)
    pad_start = pad_end - padded
    expert = route[:, :TOP_K].astype(jnp.int32)
    start_of = jnp.sum(jnp.where(expert[:, :, None] == eidx, pad_start, 0), axis=-1)
    dest = (start_of + ranks[:, :TOP_K].astype(jnp.int32)).reshape(-1)
    cap = -(-TK // R) * R + N_EXPERTS * R
    nblk = cap // R
    blk_start = jnp.arange(nblk, dtype=jnp.int32) * R
    blk_exp = jnp.minimum(jnp.sum((blk_start[:, None] >= pad_end[None, :]).astype(jnp.int32), axis=1), N_EXPERTS - 1)
    sched = jnp.concatenate([blk_exp, (pad_end[-1:] // R).astype(jnp.int32)])

    def tiles(tm):
        return dest.reshape(T // tm, 1, TOP_K * tm)

    meta = jnp.concatenate([pad_end, padded]).astype(jnp.int32)
    wg, wu, wd, xg = _dispatch_cast(meta, tiles(T // DISPATCH_STEPS), moe_w_gate[0], moe_w_up[0], moe_w_down[0],
                                    h, cap)
    yg = _moe_experts(sched, xg, row(moe_norm[0]), wg, wu, wd)
    pos = tiles(2 * COMBINE_ROWS)
    out = _combine_norm(pos, h, route, row(final_norm), yg)
    return out.reshape(B, S, D)
```

```python
import functools

import jax
import jax.numpy as jnp
import numpy as np
from jax import lax
from jax.experimental import pallas as pl
from jax.experimental.pallas import tpu as pltpu

F32 = jnp.float32
BF16 = jnp.bfloat16

LANES = 128
EPS = 1e-6
LOG2E = 1.4426950408889634
ROPE_BASE = 10000.0
CHUNK = 128
MLA_HEADS, MLA_NOPE, MLA_ROPE, MLA_V = 8, 64, 32, 64
MLA_Q_RANK, MLA_KV_RANK = 384, 256
RET_HEADS, RET_DK, RET_DV = 8, 64, 128
SWA_HEADS, SWA_KV_HEADS, SWA_DIM, WINDOW = 16, 2, 64, 128
SB_HEADS, SB_DIM = 8, 64
N_EXPERTS, TOP_K = 8, 2
MLA_NOPE_ALL, MLA_ROPE_ALL, MLA_V_ALL = MLA_HEADS * MLA_NOPE, MLA_HEADS * MLA_ROPE, MLA_HEADS * MLA_V
AB_COLS = tuple(int(c) for c in np.cumsum((0, MLA_Q_RANK, MLA_KV_RANK, LANES) + (RET_HEADS * RET_DK,) * 2
                                          + (RET_HEADS * RET_DV,) * 2))
CD_COLS = tuple(int(c) for c in np.cumsum((0, SWA_HEADS * SWA_DIM) + (SWA_KV_HEADS * SWA_DIM,) * 2
                                          + (SB_HEADS * SB_DIM,) * 3))

VMEM_LIMIT = 56 * 1024 * 1024
ROW_TILE = 512
FF_STEPS = 2
MOE_FF_STEPS = 2
CAST_ROWS = 512
COMBINE_ROWS = 256
COMBINE_GROUP = 32
RANK_ROWS = 512
DISPATCH_STEPS = 32
DISPATCH_BUFFERS = 3
DISPATCH_UNROLL = 16
MOE_ROWS = 512
MLA_TQ = 512
SB_QTILES = 2
SB_T = 256


def _cparams(sem):
    return pltpu.CompilerParams(dimension_semantics=sem, vmem_limit_bytes=VMEM_LIMIT)


def _rms(x, g):
    return x * lax.rsqrt(jnp.mean(x * x, axis=-1, keepdims=True) + EPS) * g


def _rope_slab(x, cos, sin_signed, half):
    lane = lax.broadcasted_iota(jnp.int32, x.shape, 1)
    first = (lane % (2 * half)) < half
    partner = jnp.where(first, pltpu.roll(x, LANES - half, 1), pltpu.roll(x, half, 1))
    return x * cos + partner * sin_signed


def _nt_dot(a, b):
    return lax.dot_general(a, b, (((1,), (1,)), ((), ())), preferred_element_type=F32)


def _dot(a, b):
    return jnp.dot(a, b, preferred_element_type=F32)


def _proj0_kernel(h_ref, g_ref, win_ref, qn_ref, wuq_ref, kvn_ref, wukv_ref, cm_ref, sm_ref, cr_ref, sr_ref,
                  qnope_ref, qrope_ref, knope_ref, krope_ref, v_ref, rq_ref, rk_ref, rv_ref, rg_ref):
    xn = _rms(h_ref[...], g_ref[...]).astype(BF16)

    def mm(seg):
        return _dot(xn, win_ref[:, AB_COLS[seg]:AB_COLS[seg + 1]])

    cm, sm, cr, sr = cm_ref[...], sm_ref[...], cr_ref[...], sr_ref[...]
    q_scale = (MLA_NOPE + MLA_ROPE) ** -0.5
    qh = _dot(_rms(mm(0), qn_ref[...]).astype(BF16), wuq_ref[...])
    qnope_ref[...] = (qh[:, :MLA_NOPE_ALL] * q_scale).astype(BF16)
    for s in range(MLA_ROPE_ALL // LANES):
        slab = qh[:, MLA_NOPE_ALL + LANES * s:MLA_NOPE_ALL + LANES * (s + 1)]
        qrope_ref[:, LANES * s:LANES * (s + 1)] = (_rope_slab(slab, cm, sm, MLA_ROPE // 2) * q_scale).astype(BF16)
    kvh = _dot(_rms(mm(1), kvn_ref[...]).astype(BF16), wukv_ref[...])
    knope_ref[...] = kvh[:, :MLA_NOPE_ALL].astype(BF16)
    v_ref[...] = kvh[:, MLA_NOPE_ALL:].astype(BF16)
    krope_ref[...] = _rope_slab(mm(2), cm, sm, MLA_ROPE // 2).astype(BF16)
    rq = mm(3)
    rk = mm(4)
    for s in range(RET_HEADS * RET_DK // LANES):
        sl = slice(LANES * s, LANES * (s + 1))
        rq_ref[:, sl] = _rope_slab(rq[:, sl], cr, sr, RET_DK // 2).astype(BF16)
        rk_ref[:, sl] = (_rope_slab(rk[:, sl], cr, sr, RET_DK // 2) * (RET_DK ** -0.5)).astype(BF16)
    rv_ref[...] = mm(5).astype(BF16)
    rg = mm(6)
    rg_ref[...] = (rg * (1.0 / (1.0 + jnp.exp(-rg)))).astype(BF16)


def _proj0(h, g, win, qn, wuq, kvn, wukv, cm, sm, cr, sr, seq):
    T, D = h.shape
    tm = ROW_TILE
    nseq = seq // tm
    row = lambda w: pl.BlockSpec((tm, w), lambda i: (i, 0))
    full = lambda a: pl.BlockSpec(a.shape, lambda i: (0, 0))
    pos = pl.BlockSpec((tm, LANES), lambda i: (i % nseq, 0))
    widths = (MLA_NOPE_ALL, MLA_ROPE_ALL, MLA_NOPE_ALL, LANES, MLA_V_ALL) + tuple(np.diff(AB_COLS[3:]))
    return pl.pallas_call(
        _proj0_kernel,
        grid=(T // tm,),
        in_specs=[row(D), full(g), full(win), full(qn), full(wuq), full(kvn), full(wukv), pos, pos, pos, pos],
        out_specs=[row(w) for w in widths],
        out_shape=[jax.ShapeDtypeStruct((T, w), BF16) for w in widths],
        compiler_params=_cparams(("parallel",)),
        name="proj0",
    )(h, g, win, qn, wuq, kvn, wukv, cm, sm, cr, sr)


def _mla_kernel(qn_ref, qr_ref, kn_ref, kr_ref, v_ref, o_ref, kk_ref, *, tq):
    S = qn_ref.shape[0]
    p = pl.program_id(1)
    kk_ref[:, :LANES] = kn_ref[...]
    kk_ref[:, LANES:] = kr_ref[...]
    lane = lax.broadcasted_iota(jnp.int32, (1, LANES), 1)
    quarter = (p % 2) * 2
    row = lax.broadcasted_iota(jnp.int32, (tq, tq), 0)
    col = lax.broadcasted_iota(jnp.int32, (tq, tq), 1)
    causal = col <= row
    for i in range(S // tq):
        lo, hi = i * tq, (i + 1) * tq
        qn = qn_ref[lo:hi, :]
        qr = qr_ref[lo:hi, :]
        outs = []
        for hh in range(2):
            mn = (lane // 64 == hh).astype(BF16)
            mr = (lane // 32 == quarter + hh).astype(BF16)
            qa = jnp.concatenate([qn * mn, qr * mr], axis=1)
            sd = jnp.where(causal, _nt_dot(qa, kk_ref[lo:hi, :]), -jnp.inf)
            m = jnp.max(sd, axis=-1, keepdims=True)
            if i > 0:
                so = _nt_dot(qa, kk_ref[:lo, :])
                m = jnp.maximum(m, jnp.max(so, axis=-1, keepdims=True))
            pd = jnp.exp(sd - m)
            l = jnp.sum(pd, axis=-1, keepdims=True)
            acc = _dot(pd.astype(BF16), v_ref[lo:hi, :])
            if i > 0:
                po = jnp.exp(so - m)
                l = l + jnp.sum(po, axis=-1, keepdims=True)
                acc = acc + _dot(po.astype(BF16), v_ref[:lo, :])
            outs.append(acc / l)
        o_ref[lo:hi, :] = jnp.where(lane < 64, outs[0], outs[1]).astype(BF16)


def _mla_attention(qnope, qrope, knope, krope, v, batch, seq):
    T = qnope.shape[0]
    blk = lambda f: pl.BlockSpec((seq, LANES), f)
    return pl.pallas_call(
        functools.partial(_mla_kernel, tq=MLA_TQ),
        grid=(batch, MLA_HEADS // 2),
        in_specs=[blk(lambda b, p: (b, p)), blk(lambda b, p: (b, p // 2)), blk(lambda b, p: (b, p)),
                  blk(lambda b, p: (b, 0)), blk(lambda b, p: (b, p))],
        out_specs=blk(lambda b, p: (b, p)),
        out_shape=jax.ShapeDtypeStruct((T, MLA_HEADS * MLA_V), BF16),
        scratch_shapes=[pltpu.VMEM((seq, 2 * LANES), BF16)],
        compiler_params=_cparams(("parallel", "parallel")),
        name="mla_attention",
    )(qnope, qrope, knope, krope, v)


def _ret_kernel(q_ref, k_ref, v_ref, g_ref, gn_ref, din_ref, qd_ref, kd_ref, cd_ref, o_ref):
    S = q_ref.shape[0]
    h = pl.program_id(1)
    lane = lax.broadcasted_iota(jnp.int32, (1, LANES), 1)
    qmask = (lane // RET_DK == h % 2).astype(F32)
    din = din_ref[0]
    qd = qd_ref[0] * qmask
    kd = kd_ref[0]
    cd = cd_ref[0]
    gain = gn_ref[...]
    rows = [slice(c * CHUNK, (c + 1) * CHUNK) for c in range(S // CHUNK)]
    kvs = [_dot((k_ref[r, :].astype(F32) * kd).T.astype(BF16), v_ref[r, :]) for r in rows[:-1]]
    states = [None]
    for kv in kvs:
        states.append(kv if states[-1] is None else states[-1] * cd + kv)
    for r, state in zip(rows, states):
        q = q_ref[r, :].astype(F32)
        inner = _nt_dot((q * qmask).astype(BF16), k_ref[r, :]) * din
        y = _dot(inner.astype(BF16), v_ref[r, :])
        if state is not None:
            y = y + _dot((q * qd).astype(BF16), state.astype(BF16))
        mu = jnp.mean(y, axis=-1, keepdims=True)
        var = jnp.mean(jnp.square(y - mu), axis=-1, keepdims=True)
        yn = (y - mu) * lax.rsqrt(var + EPS) * gain
        o_ref[r, :] = (g_ref[r, :].astype(F32) * yn).astype(BF16)


def _retention(rq, rk, rv, rg, gn, din, qd, kd, cd, batch, seq):
    T = rq.shape[0]
    blk = lambda f: pl.BlockSpec((seq, LANES), f)
    tab = pl.BlockSpec((1, CHUNK, LANES), lambda b, h: (h, 0, 0))
    return pl.pallas_call(
        _ret_kernel,
        grid=(batch, RET_HEADS),
        in_specs=[blk(lambda b, h: (b, h // 2)), blk(lambda b, h: (b, h // 2)), blk(lambda b, h: (b, h)),
                  blk(lambda b, h: (b, h)), pl.BlockSpec((1, LANES), lambda b, h: (0, h)), tab, tab, tab,
                  pl.BlockSpec((1, 1, LANES), lambda b, h: (h, 0, 0))],
        out_specs=blk(lambda b, h: (b, h)),
        out_shape=jax.ShapeDtypeStruct((T, RET_HEADS * RET_DV), BF16),
        compiler_params=_cparams(("parallel", "parallel")),
        name="retention",
    )(rq, rk, rv, rg, gn, din, qd, kd, cd)


def _ffn_kernel(h_ref, a_ref, b_ref, wa_ref, wb_ref, g_ref, wg_ref, wu_ref, wd_ref, o_ref, xn_ref, acc_ref):
    j = pl.program_id(1)

    @pl.when(j == 0)
    def _():
        tm = h_ref.shape[0]
        for rows in (slice(0, tm // 2), slice(tm // 2, tm)):
            hn = h_ref[rows, :] + _dot(a_ref[rows, :], wa_ref[...]) + _dot(b_ref[rows, :], wb_ref[...])
            acc_ref[rows, :] = hn
            xn_ref[rows, :] = _rms(hn, g_ref[...]).astype(BF16)

    xn = xn_ref[...]
    gate = _dot(xn, wg_ref[...])
    act = gate * (1.0 / (1.0 + jnp.exp(-gate))) * _dot(xn, wu_ref[...])
    acc_ref[...] += _dot(act.astype(BF16), wd_ref[...])

    @pl.when(j == pl.num_programs(1) - 1)
    def _():
        o_ref[...] = acc_ref[...]


def _mixer_out_ffn(h, a, b, wa, wb, g, wg, wu, wd):
    T, D = h.shape
    F = wg.shape[1]
    tm, nf = ROW_TILE, FF_STEPS
    tf = F // nf
    row = lambda w: pl.BlockSpec((tm, w), lambda i, j: (i, 0))
    full = lambda x: pl.BlockSpec(x.shape, lambda i, j: (0, 0))
    return pl.pallas_call(
        _ffn_kernel,
        grid=(T // tm, nf),
        in_specs=[row(D), row(a.shape[1]), row(b.shape[1]), full(wa), full(wb), full(g),
                  pl.BlockSpec((D, tf), lambda i, j: (0, j)), pl.BlockSpec((D, tf), lambda i, j: (0, j)),
                  pl.BlockSpec((tf, D), lambda i, j: (j, 0))],
        out_specs=row(D),
        out_shape=jax.ShapeDtypeStruct((T, D), F32),
        scratch_shapes=[pltpu.VMEM((tm, D), BF16), pltpu.VMEM((tm, D), F32)],
        compiler_params=_cparams(("parallel", "arbitrary")),
        name="mixer_out_ffn",
    )(h, a, b, wa, wb, g, wg, wu, wd)


def _proj1_kernel(h_ref, g_ref, win_ref, sq_ref, sk_ref, sv_ref, bq_ref, bk_ref, bv_ref):
    xn = _rms(h_ref[...], g_ref[...]).astype(BF16)

    def mm(seg):
        return _dot(xn, win_ref[:, CD_COLS[seg]:CD_COLS[seg + 1]])

    sq_ref[...] = (mm(0) * (SWA_DIM ** -0.5)).astype(BF16)
    sk_ref[...] = mm(1).astype(BF16)
    sv_ref[...] = mm(2).astype(BF16)
    bq_ref[...] = (mm(3) * (SB_DIM ** -0.5)).astype(BF16)
    bk_ref[...] = mm(4).astype(BF16)
    bv_ref[...] = mm(5).astype(BF16)


def _proj1(h, g, win):
    T, D = h.shape
    tm = ROW_TILE
    row = lambda w: pl.BlockSpec((tm, w), lambda i: (i, 0))
    full = lambda a: pl.BlockSpec(a.shape, lambda i: (0, 0))
    widths = tuple(np.diff(CD_COLS))
    return pl.pallas_call(
        _proj1_kernel,
        grid=(T // tm,),
        in_specs=[row(D), full(g), full(win)],
        out_specs=[row(w) for w in widths],
        out_shape=[jax.ShapeDtypeStruct((T, w), BF16) for w in widths],
        compiler_params=_cparams(("parallel",)),
        name="proj1",
    )(h, g, win)


def _swa_kernel(tab_ref, q_ref, k_ref, v_ref, o_ref):
    S = q_ref.shape[0]
    W = WINDOW
    j = pl.program_id(1)
    lane = lax.broadcasted_iota(jnp.int32, (1, LANES), 1)
    r2 = lax.broadcasted_iota(jnp.int32, (2 * W, 2 * W), 0)
    c2 = lax.broadcasted_iota(jnp.int32, (2 * W, 2 * W), 1)
    upper = r2 < W
    dist2 = jnp.where(upper, r2, r2 - W) + W - c2
    dist1 = dist2[:, W:]
    heads = [j, j + SWA_HEADS // 2]
    sink = jnp.where(upper[:, :1], tab_ref[heads[0]], tab_ref[heads[1]])
    slope = jnp.where(upper[:, :1], tab_ref[SWA_HEADS + heads[0]], tab_ref[SWA_HEADS + heads[1]])
    bias = [jnp.where((d >= 0) & (d < W), slope * d.astype(F32), jnp.inf) for d in (dist1, dist2)]
    masks = [(lane // 64 == hh).astype(BF16) for hh in range(2)]
    for i in range(S // W):
        q = q_ref[i * W:(i + 1) * W, :]
        lo = max(i - 1, 0) * W
        k, v = k_ref[lo:(i + 1) * W, :], v_ref[lo:(i + 1) * W, :]
        s = _nt_dot(jnp.concatenate([q * masks[0], q * masks[1]], axis=0), k) - bias[min(i, 1)]
        m = jnp.maximum(jnp.max(s, axis=-1, keepdims=True), sink)
        p = jnp.exp(s - m)
        den = jnp.sum(p, axis=-1, keepdims=True) + jnp.exp(sink - m)
        o = _dot(p.astype(BF16), v) / den
        o_ref[i * W:(i + 1) * W, :] = jnp.where(lane < 64, o[:W], o[W:]).astype(BF16)


def _swa_attention(tab, sq, sk, sv, batch, seq):
    T = sq.shape[0]
    blk = lambda f: pl.BlockSpec((seq, LANES), f)
    return pl.pallas_call(
        _swa_kernel,
        grid_spec=pltpu.PrefetchScalarGridSpec(
            num_scalar_prefetch=1,
            grid=(batch, SWA_HEADS // 2),
            in_specs=[blk(lambda b, j, t: (b, j)), blk(lambda b, j, t: (b, 0)), blk(lambda b, j, t: (b, 0))],
            out_specs=blk(lambda b, j, t: (b, j)),
        ),
        out_shape=jax.ShapeDtypeStruct((T, SWA_HEADS * SWA_DIM), BF16),
        compiler_params=_cparams(("parallel", "parallel")),
        name="swa_attention",
    )(tab, sq, sk, sv)


def _sb_kernel(q_ref, k_ref, v_ref, o_ref, *, t):
    S = q_ref.shape[0]
    lane = lax.broadcasted_iota(jnp.int32, (1, LANES), 1)
    row = lax.broadcasted_iota(jnp.int32, (t, t), 0)
    col = lax.broadcasted_iota(jnp.int32, (t, t), 1)
    before = col < row
    neg_later = jnp.where(row > col, -1.0, 0.0).astype(BF16)

    def softplus(z):
        return jnp.maximum(z, 0.0) + jnp.log(1.0 + jnp.exp2(jnp.abs(z) * (-LOG2E)))

    def diag_tile(qm, rows):
        z = _nt_dot(qm, k_ref[rows, :])
        sp = jnp.where(before, softplus(z), 0.0)
        a = jnp.where(before, jnp.exp((z - sp) + _dot(sp.astype(BF16), neg_later)), 0.0)
        return _dot(a.astype(BF16), v_ref[rows, :]), -jnp.sum(sp, axis=-1, keepdims=True)

    def full_tile(qm, krows, acc, run):
        z = _nt_dot(qm, k_ref[krows, :])
        sp = softplus(z)
        a = jnp.exp((z - sp) + (_dot(sp.astype(BF16), neg_later) + run))
        return acc + _dot(a.astype(BF16), v_ref[krows, :]), run - jnp.sum(sp, axis=-1, keepdims=True)

    m = SB_QTILES
    for i in range(S // (m * t)):
        lo = i * m * t
        q = q_ref[lo:lo + m * t, :]
        outs = []
        for hh in range(2):
            qm = q * (lane // 64 == hh).astype(BF16)
            accs, runs = [], []
            for r in range(m):
                qs = qm[r * t:(r + 1) * t]
                acc, run = diag_tile(qs, slice(lo + r * t, lo + (r + 1) * t))
                for jj in range(r - 1, -1, -1):
                    acc, run = full_tile(qs, slice(lo + jj * t, lo + (jj + 1) * t), acc, run)
                accs.append(acc)
                runs.append(run)
            acc = jnp.concatenate(accs, axis=0)
            run = jnp.concatenate(runs, axis=0)
            for j in range(i * m - 1, -1, -1):
                acc, run = full_tile(qm, slice(j * t, (j + 1) * t), acc, run)
            outs.append(acc)
        o_ref[lo:lo + m * t, :] = jnp.where(lane < 64, outs[0], outs[1]).astype(BF16)


def _sb_attention(bq, bk, bv, batch, seq):
    T = bq.shape[0]
    blk = pl.BlockSpec((seq, LANES), lambda b, p: (b, p))
    return pl.pallas_call(
        functools.partial(_sb_kernel, t=SB_T),
        grid=(batch, SB_HEADS // 2),
        in_specs=[blk, blk, blk],
        out_specs=blk,
        out_shape=jax.ShapeDtypeStruct((T, SB_HEADS * SB_DIM), BF16),
        compiler_params=_cparams(("parallel", "parallel")),
        name="sb_attention",
    )(bq, bk, bv)


def _split_bf16(x):
    hi = x.astype(BF16)
    return hi, (x - hi.astype(F32)).astype(BF16)


def _out_router_kernel(h_ref, a_ref, b_ref, wa_ref, wb_ref, g_ref, rcat_ref, o_ref, route_ref):
    tm = h_ref.shape[0]
    for rows in (slice(0, tm // 2), slice(tm // 2, tm)):
        hn = h_ref[rows, :] + _dot(a_ref[rows, :], wa_ref[...]) + _dot(b_ref[rows, :], wb_ref[...])
        o_ref[rows, :] = hn
        xhi, xlo = _split_bf16(_rms(hn, g_ref[...]))
        both = _dot(xhi, rcat_ref[...])
        logits = both[:, :LANES] + (both[:, LANES:] + _dot(xlo, rcat_ref[:, :LANES]))
        lane = lax.broadcasted_iota(jnp.int32, logits.shape, 1)
        logits = jnp.where(lane < N_EXPERTS, logits, -jnp.inf)
        m1 = jnp.max(logits, axis=-1, keepdims=True)
        i1 = jnp.min(jnp.where(logits == m1, lane, LANES), axis=-1, keepdims=True)
        rest = jnp.where(lane == i1, -jnp.inf, logits)
        m2 = jnp.max(rest, axis=-1, keepdims=True)
        i2 = jnp.min(jnp.where(rest == m2, lane, LANES), axis=-1, keepdims=True)
        e2 = jnp.exp(m2 - m1)
        den = 1.0 + e2
        route = jnp.where(lane == 0, i1.astype(F32), 0.0)
        route = jnp.where(lane == 1, i2.astype(F32), route)
        route = jnp.where(lane == 2, 1.0 / den, route)
        route = jnp.where(lane == 3, e2 / den, route)
        route_ref[rows, :] = route


def _mixer_out_router(h, a, b, wa, wb, g, rcat):
    T, D = h.shape
    tm = ROW_TILE
    row = lambda w: pl.BlockSpec((tm, w), lambda i: (i, 0))
    full = lambda x: pl.BlockSpec(x.shape, lambda i: (0, 0))
    return pl.pallas_call(
        _out_router_kernel,
        grid=(T // tm,),
        in_specs=[row(D), row(a.shape[1]), row(b.shape[1]), full(wa), full(wb), full(g), full(rcat)],
        out_specs=[row(D), row(LANES)],
        out_shape=[jax.ShapeDtypeStruct((T, D), F32), jax.ShapeDtypeStruct((T, LANES), F32)],
        compiler_params=_cparams(("parallel",)),
        name="mixer_out_router",
    )(h, a, b, wa, wb, g, rcat)


def _rank_kernel(route_ref, rank_ref, counts_ref):
    tm = route_ref.shape[0]

    @pl.when(pl.program_id(0) == 0)
    def _():
        counts_ref[...] = jnp.zeros_like(counts_ref)

    route = route_ref[...]
    lane = lax.broadcasted_iota(jnp.int32, route.shape, 1).astype(F32)
    oh0 = jnp.where(lane == route[:, 0:1], 1.0, 0.0)
    oh1 = jnp.where(lane == route[:, 1:2], 1.0, 0.0)
    both = oh0 + oh1
    earlier = (lax.broadcasted_iota(jnp.int32, (tm, tm), 1) < lax.broadcasted_iota(jnp.int32, (tm, tm), 0))
    prefix = _dot(earlier.astype(BF16), both.astype(BF16)) + counts_ref[0:1, :]
    r0 = jnp.sum(oh0 * prefix, axis=-1, keepdims=True)
    r1 = jnp.sum(oh1 * prefix, axis=-1, keepdims=True)
    rank_ref[...] = jnp.where(lane == 0.0, r0, jnp.where(lane == 1.0, r1, 0.0))
    counts_ref[...] += jnp.sum(both, axis=0, keepdims=True)


def _routing_ranks(route):
    T = route.shape[0]
    tm = RANK_ROWS
    return pl.pallas_call(
        _rank_kernel,
        grid=(T // tm,),
        in_specs=[pl.BlockSpec((tm, LANES), lambda i: (i, 0))],
        out_specs=[pl.BlockSpec((tm, LANES), lambda i: (i, 0)), pl.BlockSpec((8, LANES), lambda i: (0, 0))],
        out_shape=[jax.ShapeDtypeStruct((T, LANES), F32), jax.ShapeDtypeStruct((8, LANES), F32)],
        compiler_params=_cparams(("arbitrary",)),
        name="routing_ranks",
    )(route)


def _wait_rows(src_hbm, dst_ref, sem):
    pltpu.make_async_copy(src_hbm.at[pl.ds(0, dst_ref.shape[0])], dst_ref, sem).wait()


def _cast_kernel(w_ref, o_ref):
    o_ref[...] = w_ref[...].astype(BF16)


def _to_bf16(w):
    E, K, N = w.shape
    blk = pl.BlockSpec((1, CAST_ROWS, N), lambda e, k: (e, k, 0))
    return pl.pallas_call(
        _cast_kernel,
        grid=(E, K // CAST_ROWS),
        in_specs=[blk],
        out_specs=blk,
        out_shape=jax.ShapeDtypeStruct(w.shape, BF16),
        compiler_params=_cparams(("parallel", "parallel")),
        name="cast_bf16",
    )(w)


def _dispatch_kernel(meta_ref, dest_ref, wg_ref, wu_ref, wd_ref, h_hbm, og_ref, ou_ref, od_ref, xg_hbm,
                     zero_ref, hbuf, hsem, sem, zsem):
    step = pl.program_id(0) * pl.num_programs(1) + pl.program_id(1)
    nsteps = pl.num_programs(0) * pl.num_programs(1)
    tpb = hbuf.shape[1]
    R = zero_ref.shape[0]
    slot = step % DISPATCH_BUFFERS

    def zero_block(start):
        return pltpu.make_async_copy(zero_ref, xg_hbm.at[pl.ds(pl.multiple_of(start, R), R)], zsem)

    def h_block(s):
        return pltpu.make_async_copy(h_hbm.at[pl.ds(pl.multiple_of(s * tpb, tpb), tpb)],
                                     hbuf.at[s % DISPATCH_BUFFERS], hsem.at[s % DISPATCH_BUFFERS])

    def wait_scatter(s):
        for k in range(TOP_K):
            pltpu.make_async_copy(hbuf.at[s % DISPATCH_BUFFERS], xg_hbm.at[pl.ds(0, tpb)],
                                  sem.at[s % DISPATCH_BUFFERS]).wait()

    cap = xg_hbm.shape[0]
    fills = [(meta_ref[N_EXPERTS + e] > 0, meta_ref[e] - R) for e in range(N_EXPERTS)]
    fills += [(meta_ref[N_EXPERTS - 1] + b * R < cap, meta_ref[N_EXPERTS - 1] + b * R) for b in range(N_EXPERTS)]

    @pl.when(step == 0)
    def _():
        h_block(step).start()
        zero_ref[...] = jnp.zeros_like(zero_ref)
        for do_start in (True, False):
            for cond, start in fills:
                @pl.when(cond)
                def _():
                    zero_block(start).start() if do_start else zero_block(start).wait()

    @pl.when(step >= DISPATCH_BUFFERS - 1)
    def _():
        wait_scatter(step - (DISPATCH_BUFFERS - 1))

    @pl.when(step + 1 < nsteps)
    def _():
        h_block(step + 1).start()

    h_block(step).wait()
    src = hbuf.at[slot]

    def scatter_group(gi, carry):
        base = gi * DISPATCH_UNROLL
        rows = [[dest_ref[0, 0, TOP_K * (base + u) + k] for k in range(TOP_K)]
                for u in range(DISPATCH_UNROLL)]
        for u in range(DISPATCH_UNROLL):
            for k in range(TOP_K):
                pltpu.make_async_copy(src.at[pl.ds(base + u, 1)], xg_hbm.at[pl.ds(rows[u][k], 1)],
                                      sem.at[slot]).start(priority=k % 2)
        return carry

    lax.fori_loop(0, tpb // DISPATCH_UNROLL, scatter_group, 0)
    og_ref[...] = wg_ref[...].astype(BF16)
    ou_ref[...] = wu_ref[...].astype(BF16)
    od_ref[...] = wd_ref[...].astype(BF16)

    @pl.when(step == nsteps - 1)
    def _():
        for back in range(DISPATCH_BUFFERS - 2, -1, -1):
            @pl.when(step - back >= 0)
            def _():
                wait_scatter(step - back)


def _dispatch_cast(meta, dest, wg, wu, wd, h, cap):
    E, D, F = wg.shape
    nsteps, _, n = dest.shape
    nk = nsteps // E
    spec = lambda w: pl.BlockSpec((1, w.shape[1] // nk, w.shape[2]), lambda e, k, m: (e, k, 0))
    hbm = pl.BlockSpec(memory_space=pl.ANY)
    nbuf = DISPATCH_BUFFERS
    return pl.pallas_call(
        _dispatch_kernel,
        grid_spec=pltpu.PrefetchScalarGridSpec(
            num_scalar_prefetch=1,
            grid=(E, nk),
            in_specs=[pl.BlockSpec((1, 1, n), lambda e, k, m: (e * nk + k, 0, 0), memory_space=pltpu.SMEM),
                      spec(wg), spec(wu), spec(wd), hbm],
            out_specs=[spec(wg), spec(wu), spec(wd), hbm],
            scratch_shapes=[pltpu.VMEM((MOE_ROWS, D), F32), pltpu.VMEM((nbuf, n // TOP_K, D), F32),
                            pltpu.SemaphoreType.DMA((nbuf,)), pltpu.SemaphoreType.DMA((nbuf,)),
                            pltpu.SemaphoreType.DMA],
        ),
        out_shape=[jax.ShapeDtypeStruct(wg.shape, BF16), jax.ShapeDtypeStruct(wu.shape, BF16),
                   jax.ShapeDtypeStruct(wd.shape, BF16), jax.ShapeDtypeStruct((cap, D), F32)],
        compiler_params=_cparams(("arbitrary", "arbitrary")),
        name="dispatch_cast",
    )(meta, dest, wg, wu, wd, h)


def _moe_kernel(sched_ref, x_ref, xnext_ref, g_ref, wg_ref, wu_ref, wd_ref, o_ref, xn_ref, acc_ref):
    i, j = pl.program_id(0), pl.program_id(1)
    nblk, nf = pl.num_programs(0), pl.num_programs(1)
    used = i < sched_ref[nblk]
    slot = i % 2
    share = x_ref.shape[0] // nf

    @pl.when((i == 0) & (j == 0))
    def _():
        xn_ref[0] = _rms(x_ref[...], g_ref[...]).astype(BF16)

    @pl.when(j == 0)
    def _():
        acc_ref[...] = jnp.zeros_like(acc_ref)

    @pl.when(used)
    def _():
        xn = xn_ref[slot]
        gate = _dot(xn, wg_ref[0])
        act = gate * (1.0 / (1.0 + jnp.exp(-gate))) * _dot(xn, wu_ref[0])
        acc_ref[...] += _dot(act.astype(BF16), wd_ref[0])
        rows = pl.ds(pl.multiple_of(j * share, share), share)
        xn_ref[1 - slot, rows, :] = _rms(xnext_ref[rows, :], g_ref[...]).astype(BF16)

    @pl.when(j == nf - 1)
    def _():
        o_ref[...] = acc_ref[...]


def _moe_experts(sched, xg, g, wg, wu, wd):
    cap, D = xg.shape
    F = wg.shape[2]
    R, nf = MOE_ROWS, MOE_FF_STEPS
    nblk, tf = cap // R, F // nf

    def blk_i(i, s):
        return jnp.minimum(i, s[nblk] - 1)

    def col_j(i, j, s):
        return jnp.where(i < s[nblk], j, nf - 1)

    return pl.pallas_call(
        _moe_kernel,
        grid_spec=pltpu.PrefetchScalarGridSpec(
            num_scalar_prefetch=1,
            grid=(nblk, nf),
            in_specs=[pl.BlockSpec((R, D), lambda i, j, s: (blk_i(i, s), 0)),
                      pl.BlockSpec((R, D), lambda i, j, s: (blk_i(i + 1, s), 0)),
                      pl.BlockSpec(g.shape, lambda i, j, s: (0, 0)),
                      pl.BlockSpec((1, D, tf), lambda i, j, s: (s[blk_i(i, s)], 0, col_j(i, j, s))),
                      pl.BlockSpec((1, D, tf), lambda i, j, s: (s[blk_i(i, s)], 0, col_j(i, j, s))),
                      pl.BlockSpec((1, tf, D), lambda i, j, s: (s[blk_i(i, s)], col_j(i, j, s), 0))],
            out_specs=pl.BlockSpec((R, D), lambda i, j, s: (i, 0)),
            scratch_shapes=[pltpu.VMEM((2, R, D), BF16), pltpu.VMEM((R, D), F32)],
        ),
        out_shape=jax.ShapeDtypeStruct((cap, D), F32),
        compiler_params=_cparams(("arbitrary", "arbitrary")),
        name="moe_experts",
    )(sched, xg, xg, g, wg, wu, wd)


def _combine_kernel(pos_ref, nxt_ref, h_ref, route_ref, g_ref, y_hbm, o_ref, ybuf0, ybuf1, sem):
    i = pl.program_id(0)
    n = ybuf0.shape[0]
    tm = n // TOP_K

    def issue(idx_ref, base, buf, s):
        for g0 in range(0, n, COMBINE_GROUP):
            rows = [idx_ref[0, 0, base + TOP_K * ((g0 + r) % tm) + (g0 + r) // tm] for r in range(COMBINE_GROUP)]
            for r in range(COMBINE_GROUP):
                pltpu.make_async_copy(y_hbm.at[pl.ds(rows[r], 1)], buf.at[pl.ds(g0 + r, 1)],
                                      sem.at[s]).start(priority=r % 2)

    def finish(buf, s, rows):
        _wait_rows(y_hbm, buf, sem.at[s])
        route = route_ref[rows, :]
        out = h_ref[rows, :] + (buf[:tm, :] * route[:, 2:3] + buf[tm:, :] * route[:, 3:4])
        o_ref[rows, :] = _rms(out, g_ref[...])

    @pl.when(i == 0)
    def _():
        issue(pos_ref, 0, ybuf0, 0)

    issue(pos_ref, n, ybuf1, 1)
    finish(ybuf0, 0, slice(0, tm))
    issue(nxt_ref, 0, ybuf0, 0)
    finish(ybuf1, 1, slice(tm, 2 * tm))

    @pl.when(i == pl.num_programs(0) - 1)
    def _():
        _wait_rows(y_hbm, ybuf0, sem.at[0])


def _combine_norm(pos, h, route, g, yg):
    T, D = h.shape
    nstep, _, n2 = pos.shape
    n = n2 // 2
    tm2 = n2 // TOP_K
    row = lambda w: pl.BlockSpec((tm2, w), lambda i: (i, 0))
    smem = lambda f: pl.BlockSpec((1, 1, n2), f, memory_space=pltpu.SMEM)
    return pl.pallas_call(
        _combine_kernel,
        grid=(nstep,),
        in_specs=[smem(lambda i: (i, 0, 0)), smem(lambda i: (jnp.minimum(i + 1, nstep - 1), 0, 0)),
                  row(D), row(LANES), pl.BlockSpec(g.shape, lambda i: (0, 0)), pl.BlockSpec(memory_space=pl.ANY)],
        out_specs=row(D),
        out_shape=jax.ShapeDtypeStruct((T, D), F32),
        scratch_shapes=[pltpu.VMEM((n, D), F32), pltpu.VMEM((n, D), F32), pltpu.SemaphoreType.DMA((2,))],
        compiler_params=_cparams(("arbitrary",)),
        name="combine_norm",
    )(pos, pos, h, route, g, yg)


def _rope_tables(seq, half):
    inv_freq = ROPE_BASE ** (-jnp.arange(half, dtype=F32) / half)
    ang = jnp.arange(seq).astype(F32)[:, None] * inv_freq[None, :]
    cos, sin = jnp.cos(ang), jnp.sin(ang)
    reps = LANES // (2 * half)
    return jnp.tile(jnp.concatenate([cos, cos], 1), (1, reps)), jnp.tile(jnp.concatenate([-sin, sin], 1), (1, reps))


def _retention_tables():
    log_gamma = jnp.log1p(-(2.0 ** (-5.0 - jnp.arange(RET_HEADS, dtype=F32))))
    idx = jnp.arange(CHUNK, dtype=F32)
    rel = idx[:, None] - idx[None, :]
    din = jnp.where(rel >= 0, jnp.exp(log_gamma[:, None, None] * jnp.maximum(rel, 0.0)), 0.0)
    qd = jnp.exp(log_gamma[:, None] * (idx + 1.0))
    kd = jnp.exp(log_gamma[:, None] * (CHUNK - 1.0 - idx))
    cd = jnp.exp(log_gamma * CHUNK)
    bc = lambda a: jnp.broadcast_to(a[:, :, None], (RET_HEADS, CHUNK, LANES))
    return din, bc(qd), bc(kd), jnp.broadcast_to(cd[:, None, None], (RET_HEADS, 1, LANES))


def kernel(x, ab_norm, ab_w_in, mla_q_norm, mla_w_uq, mla_kv_norm, mla_w_ukv, ret_gn, ab_w_out, ffn_norm,
           ffn_w_gate, ffn_w_up, ffn_w_down, cd_norm, cd_w_in, swa_sinks, cd_w_out, moe_norm, moe_router,
           moe_w_gate, moe_w_up, moe_w_down, final_norm):
    B, S, D = x.shape
    T = B * S
    h = x.reshape(T, D)
    row = lambda g: g.reshape(1, -1)

    w = ab_w_in[0]
    lat = MLA_Q_RANK + MLA_KV_RANK
    win0 = jnp.concatenate([w[:, :lat], jnp.tile(w[:, lat:lat + MLA_ROPE], (1, LANES // MLA_ROPE)),
                            w[:, lat + MLA_ROPE:]], axis=1).astype(BF16)
    wuq = mla_w_uq[0].reshape(MLA_Q_RANK, MLA_HEADS, MLA_NOPE + MLA_ROPE)
    wuq = jnp.concatenate([wuq[:, :, :MLA_NOPE].reshape(MLA_Q_RANK, -1),
                           wuq[:, :, MLA_NOPE:].reshape(MLA_Q_RANK, -1)], axis=1).astype(BF16)
    wukv = mla_w_ukv[0].reshape(MLA_KV_RANK, MLA_HEADS, MLA_NOPE + MLA_V)
    wukv = jnp.concatenate([wukv[:, :, :MLA_NOPE].reshape(MLA_KV_RANK, -1),
                            wukv[:, :, MLA_NOPE:].reshape(MLA_KV_RANK, -1)], axis=1).astype(BF16)
    cm, sm = _rope_tables(S, MLA_ROPE // 2)
    cr, sr = _rope_tables(S, RET_DK // 2)
    qnope, qrope, knope, krope, v, rq, rk, rv, rg = _proj0(
        h, row(ab_norm[0]), win0, row(mla_q_norm[0]), wuq, row(mla_kv_norm[0]), wukv, cm, sm, cr, sr, S)
    mla = _mla_attention(qnope, qrope, knope, krope, v, B, S)
    ret = _retention(rq, rk, rv, rg, row(ret_gn[0]), *_retention_tables(), B, S)
    wo = ab_w_out[0].astype(BF16)
    nm = MLA_HEADS * MLA_V
    h = _mixer_out_ffn(h, mla, ret, wo[:nm], wo[nm:], row(ffn_norm[0]), _to_bf16(ffn_w_gate)[0],
                       _to_bf16(ffn_w_up)[0], _to_bf16(ffn_w_down)[0])

    w = cd_w_in[0]
    nq = SWA_HEADS * SWA_DIM
    pair_order = np.stack([np.arange(SWA_HEADS // 2), np.arange(SWA_HEADS // 2) + SWA_HEADS // 2], 1).reshape(-1)
    cols = (pair_order[:, None] * SWA_DIM + np.arange(SWA_DIM)[None, :]).reshape(-1)
    win1 = jnp.concatenate([w[:, :nq][:, cols], w[:, nq:]], axis=1).astype(BF16)
    sq, sk, sv, bq, bk, bv = _proj1(h, row(cd_norm[0]), win1)
    slopes = 2.0 ** (-8.0 * jnp.arange(1, SWA_HEADS + 1, dtype=F32) / SWA_HEADS)
    swa = _swa_attention(jnp.concatenate([swa_sinks[0].astype(F32), slopes]), sq, sk, sv, B, S)
    sb = _sb_attention(bq, bk, bv, B, S)
    wo = cd_w_out[0]
    wo_swa = wo[:nq][cols].astype(BF16)
    router = jnp.pad(moe_router[0], ((0, 0), (0, LANES - N_EXPERTS)))
    rhi = router.astype(BF16)
    rlo = (router - rhi.astype(F32)).astype(BF16)
    h, route = _mixer_out_router(h, swa, sb, wo_swa, wo[nq:].astype(BF16), row(moe_norm[0]),
                                 jnp.concatenate([rhi, rlo], axis=1))

    R = MOE_ROWS
    TK = T * TOP_K
    ranks, counts = _routing_ranks(route)
    counts = counts[0, :N_EXPERTS].astype(jnp.int32)
    padded = ((counts + R - 1) // R) * R
    eidx = jnp.arange(N_EXPERTS, dtype=jnp.int32)
    pad_end = jnp.sum(jnp.where(eidx[None, :] <= eidx[:, None], padded[None, :], 0), axis=1)
    pad_start = pad_end - padded
    expert = route[:, :TOP_K].astype(jnp.int32)
    start_of = jnp.sum(jnp.where(expert[:, :, None] == eidx, pad_start, 0), axis=-1)
    dest = (start_of + ranks[:, :TOP_K].astype(jnp.int32)).reshape(-1)
    cap = -(-TK // R) * R + N_EXPERTS * R
    nblk = cap // R
    blk_start = jnp.arange(nblk, dtype=jnp.int32) * R
    blk_exp = jnp.minimum(jnp.sum((blk_start[:, None] >= pad_end[None, :]).astype(jnp.int32), axis=1), N_EXPERTS - 1)
    sched = jnp.concatenate([blk_exp, (pad_end[-1:] // R).astype(jnp.int32)])

    def tiles(tm):
        return dest.reshape(T // tm, 1, TOP_K * tm)

    meta = jnp.concatenate([pad_end, padded]).astype(jnp.int32)
    wg, wu, wd, xg = _dispatch_cast(meta, tiles(T // DISPATCH_STEPS), moe_w_gate[0], moe_w_up[0], moe_w_down[0],
                                    h, cap)
    yg = _moe_experts(sched, xg, row(moe_norm[0]), wg, wu, wd)
    pos = tiles(2 * COMBINE_ROWS)
    out = _combine_norm(pos, h, route, row(final_norm), yg)
    return out.reshape(B, S, D)
```

```python
import functools

import jax
import jax.numpy as jnp
import numpy as np
from jax import lax
from jax.experimental import pallas as pl
from jax.experimental.pallas import tpu as pltpu

F32 = jnp.float32
BF16 = jnp.bfloat16

LANES = 128
EPS = 1e-6
LOG2E = 1.4426950408889634
ROPE_BASE = 10000.0
CHUNK = 128
MLA_HEADS, MLA_NOPE, MLA_ROPE, MLA_V = 8, 64, 32, 64
MLA_Q_RANK, MLA_KV_RANK = 384, 256
RET_HEADS, RET_DK, RET_DV = 8, 64, 128
SWA_HEADS, SWA_KV_HEADS, SWA_DIM, WINDOW = 16, 2, 64, 128
SB_HEADS, SB_DIM = 8, 64
N_EXPERTS, TOP_K = 8, 2
MLA_NOPE_ALL, MLA_ROPE_ALL, MLA_V_ALL = MLA_HEADS * MLA_NOPE, MLA_HEADS * MLA_ROPE, MLA_HEADS * MLA_V
AB_COLS = tuple(int(c) for c in np.cumsum((0, MLA_Q_RANK, MLA_KV_RANK, LANES) + (RET_HEADS * RET_DK,) * 2
                                          + (RET_HEADS * RET_DV,) * 2))
CD_COLS = tuple(int(c) for c in np.cumsum((0, SWA_HEADS * SWA_DIM) + (SWA_KV_HEADS * SWA_DIM,) * 2
                                          + (SB_HEADS * SB_DIM,) * 3))

VMEM_LIMIT = 56 * 1024 * 1024
ROW_TILE = 512
FF_STEPS = 2
MOE_FF_STEPS = 2
CAST_ROWS = 512
COMBINE_ROWS = 256
COMBINE_GROUP = 32
RANK_ROWS = 512
DISPATCH_STEPS = 32
DISPATCH_BUFFERS = 3
DISPATCH_UNROLL = 16
MOE_ROWS = 512
MLA_TQ = 512
SB_QTILES = 2
SB_T = 256


def _cparams(sem):
    return pltpu.CompilerParams(dimension_semantics=sem, vmem_limit_bytes=VMEM_LIMIT)


def _rms(x, g):
    return x * lax.rsqrt(jnp.mean(x * x, axis=-1, keepdims=True) + EPS) * g


def _rope_slab(x, cos, sin_signed, half):
    lane = lax.broadcasted_iota(jnp.int32, x.shape, 1)
    first = (lane % (2 * half)) < half
    partner = jnp.where(first, pltpu.roll(x, LANES - half, 1), pltpu.roll(x, half, 1))
    return x * cos + partner * sin_signed


def _nt_dot(a, b):
    return lax.dot_general(a, b, (((1,), (1,)), ((), ())), preferred_element_type=F32)


def _dot(a, b):
    return jnp.dot(a, b, preferred_element_type=F32)


def _proj0_kernel(h_ref, g_ref, win_ref, qn_ref, wuq_ref, kvn_ref, wukv_ref, cm_ref, sm_ref, cr_ref, sr_ref,
                  qnope_ref, qrope_ref, knope_ref, krope_ref, v_ref, rq_ref, rk_ref, rv_ref, rg_ref):
    xn = _rms(h_ref[...], g_ref[...]).astype(BF16)

    def mm(seg):
        return _dot(xn, win_ref[:, AB_COLS[seg]:AB_COLS[seg + 1]])

    cm, sm, cr, sr = cm_ref[...], sm_ref[...], cr_ref[...], sr_ref[...]
    q_scale = (MLA_NOPE + MLA_ROPE) ** -0.5
    qh = _dot(_rms(mm(0), qn_ref[...]).astype(BF16), wuq_ref[...])
    qnope_ref[...] = (qh[:, :MLA_NOPE_ALL] * q_scale).astype(BF16)
    for s in range(MLA_ROPE_ALL // LANES):
        slab = qh[:, MLA_NOPE_ALL + LANES * s:MLA_NOPE_ALL + LANES * (s + 1)]
        qrope_ref[:, LANES * s:LANES * (s + 1)] = (_rope_slab(slab, cm, sm, MLA_ROPE // 2) * q_scale).astype(BF16)
    kvh = _dot(_rms(mm(1), kvn_ref[...]).astype(BF16), wukv_ref[...])
    knope_ref[...] = kvh[:, :MLA_NOPE_ALL].astype(BF16)
    v_ref[...] = kvh[:, MLA_NOPE_ALL:].astype(BF16)
    krope_ref[...] = _rope_slab(mm(2), cm, sm, MLA_ROPE // 2).astype(BF16)
    rq = mm(3)
    rk = mm(4)
    for s in range(RET_HEADS * RET_DK // LANES):
        sl = slice(LANES * s, LANES * (s + 1))
        rq_ref[:, sl] = _rope_slab(rq[:, sl], cr, sr, RET_DK // 2).astype(BF16)
        rk_ref[:, sl] = (_rope_slab(rk[:, sl], cr, sr, RET_DK // 2) * (RET_DK ** -0.5)).astype(BF16)
    rv_ref[...] = mm(5).astype(BF16)
    rg = mm(6)
    rg_ref[...] = (rg * (1.0 / (1.0 + jnp.exp(-rg)))).astype(BF16)


def _proj0(h, g, win, qn, wuq, kvn, wukv, cm, sm, cr, sr, seq):
    T, D = h.shape
    tm = ROW_TILE
    nseq = seq // tm
    row = lambda w: pl.BlockSpec((tm, w), lambda i: (i, 0))
    full = lambda a: pl.BlockSpec(a.shape, lambda i: (0, 0))
    pos = pl.BlockSpec((tm, LANES), lambda i: (i % nseq, 0))
    widths = (MLA_NOPE_ALL, MLA_ROPE_ALL, MLA_NOPE_ALL, LANES, MLA_V_ALL) + tuple(np.diff(AB_COLS[3:]))
    return pl.pallas_call(
        _proj0_kernel,
        grid=(T // tm,),
        in_specs=[row(D), full(g), full(win), full(qn), full(wuq), full(kvn), full(wukv), pos, pos, pos, pos],
        out_specs=[row(w) for w in widths],
        out_shape=[jax.ShapeDtypeStruct((T, w), BF16) for w in widths],
        compiler_params=_cparams(("parallel",)),
        name="proj0",
    )(h, g, win, qn, wuq, kvn, wukv, cm, sm, cr, sr)


def _mla_kernel(qn_ref, qr_ref, kn_ref, kr_ref, v_ref, o_ref, kk_ref, *, tq):
    S = qn_ref.shape[0]
    p = pl.program_id(1)
    kk_ref[:, :LANES] = kn_ref[...]
    kk_ref[:, LANES:] = kr_ref[...]
    lane = lax.broadcasted_iota(jnp.int32, (1, LANES), 1)
    quarter = (p % 2) * 2
    row = lax.broadcasted_iota(jnp.int32, (tq, tq), 0)
    col = lax.broadcasted_iota(jnp.int32, (tq, tq), 1)
    causal = col <= row
    for i in range(S // tq):
        lo, hi = i * tq, (i + 1) * tq
        qn = qn_ref[lo:hi, :]
        qr = qr_ref[lo:hi, :]
        outs = []
        for hh in range(2):
            mn = (lane // 64 == hh).astype(BF16)
            mr = (lane // 32 == quarter + hh).astype(BF16)
            qa = jnp.concatenate([qn * mn, qr * mr], axis=1)
            sd = jnp.where(causal, _nt_dot(qa, kk_ref[lo:hi, :]), -jnp.inf)
            m = jnp.max(sd, axis=-1, keepdims=True)
            if i > 0:
                so = _nt_dot(qa, kk_ref[:lo, :])
                m = jnp.maximum(m, jnp.max(so, axis=-1, keepdims=True))
            pd = jnp.exp(sd - m)
            l = jnp.sum(pd, axis=-1, keepdims=True)
            acc = _dot(pd.astype(BF16), v_ref[lo:hi, :])
            if i > 0:
                po = jnp.exp(so - m)
                l = l + jnp.sum(po, axis=-1, keepdims=True)
                acc = acc + _dot(po.astype(BF16), v_ref[:lo, :])
            outs.append(acc / l)
        o_ref[lo:hi, :] = jnp.where(lane < 64, outs[0], outs[1]).astype(BF16)


def _mla_attention(qnope, qrope, knope, krope, v, batch, seq):
    T = qnope.shape[0]
    blk = lambda f: pl.BlockSpec((seq, LANES), f)
    return pl.pallas_call(
        functools.partial(_mla_kernel, tq=MLA_TQ),
        grid=(batch, MLA_HEADS // 2),
        in_specs=[blk(lambda b, p: (b, p)), blk(lambda b, p: (b, p // 2)), blk(lambda b, p: (b, p)),
                  blk(lambda b, p: (b, 0)), blk(lambda b, p: (b, p))],
        out_specs=blk(lambda b, p: (b, p)),
        out_shape=jax.ShapeDtypeStruct((T, MLA_HEADS * MLA_V), BF16),
        scratch_shapes=[pltpu.VMEM((seq, 2 * LANES), BF16)],
        compiler_params=_cparams(("parallel", "parallel")),
        name="mla_attention",
    )(qnope, qrope, knope, krope, v)


def _ret_kernel(q_ref, k_ref, v_ref, g_ref, gn_ref, din_ref, qd_ref, kd_ref, cd_ref, o_ref):
    S = q_ref.shape[0]
    h = pl.program_id(1)
    lane = lax.broadcasted_iota(jnp.int32, (1, LANES), 1)
    qmask = (lane // RET_DK == h % 2).astype(F32)
    din = din_ref[0]
    qd = qd_ref[0] * qmask
    kd = kd_ref[0]
    cd = cd_ref[0]
    gain = gn_ref[...]
    rows = [slice(c * CHUNK, (c + 1) * CHUNK) for c in range(S // CHUNK)]
    kvs = [_dot((k_ref[r, :].astype(F32) * kd).T.astype(BF16), v_ref[r, :]) for r in rows[:-1]]
    states = [None]
    for kv in kvs:
        states.append(kv if states[-1] is None else states[-1] * cd + kv)
    for r, state in zip(rows, states):
        q = q_ref[r, :].astype(F32)
        inner = _nt_dot((q * qmask).astype(BF16), k_ref[r, :]) * din
        y = _dot(inner.astype(BF16), v_ref[r, :])
        if state is not None:
            y = y + _dot((q * qd).astype(BF16), state.astype(BF16))
        mu = jnp.mean(y, axis=-1, keepdims=True)
        var = jnp.mean(jnp.square(y - mu), axis=-1, keepdims=True)
        yn = (y - mu) * lax.rsqrt(var + EPS) * gain
        o_ref[r, :] = (g_ref[r, :].astype(F32) * yn).astype(BF16)


def _retention(rq, rk, rv, rg, gn, din, qd, kd, cd, batch, seq):
    T = rq.shape[0]
    blk = lambda f: pl.BlockSpec((seq, LANES), f)
    tab = pl.BlockSpec((1, CHUNK, LANES), lambda b, h: (h, 0, 0))
    return pl.pallas_call(
        _ret_kernel,
        grid=(batch, RET_HEADS),
        in_specs=[blk(lambda b, h: (b, h // 2)), blk(lambda b, h: (b, h // 2)), blk(lambda b, h: (b, h)),
                  blk(lambda b, h: (b, h)), pl.BlockSpec((1, LANES), lambda b, h: (0, h)), tab, tab, tab,
                  pl.BlockSpec((1, 1, LANES), lambda b, h: (h, 0, 0))],
        out_specs=blk(lambda b, h: (b, h)),
        out_shape=jax.ShapeDtypeStruct((T, RET_HEADS * RET_DV), BF16),
        compiler_params=_cparams(("parallel", "parallel")),
        name="retention",
    )(rq, rk, rv, rg, gn, din, qd, kd, cd)


def _ffn_kernel(h_ref, a_ref, b_ref, wa_ref, wb_ref, g_ref, wg_ref, wu_ref, wd_ref, o_ref, xn_ref, acc_ref):
    j = pl.program_id(1)

    @pl.when(j == 0)
    def _():
        tm = h_ref.shape[0]
        for rows in (slice(0, tm // 2), slice(tm // 2, tm)):
            hn = h_ref[rows, :] + _dot(a_ref[rows, :], wa_ref[...]) + _dot(b_ref[rows, :], wb_ref[...])
            acc_ref[rows, :] = hn
            xn_ref[rows, :] = _rms(hn, g_ref[...]).astype(BF16)

    xn = xn_ref[...]
    gate = _dot(xn, wg_ref[...])
    act = gate * (1.0 / (1.0 + jnp.exp(-gate))) * _dot(xn, wu_ref[...])
    acc_ref[...] += _dot(act.astype(BF16), wd_ref[...])

    @pl.when(j == pl.num_programs(1) - 1)
    def _():
        o_ref[...] = acc_ref[...]


def _mixer_out_ffn(h, a, b, wa, wb, g, wg, wu, wd):
    T, D = h.shape
    F = wg.shape[1]
    tm, nf = ROW_TILE, FF_STEPS
    tf = F // nf
    row = lambda w: pl.BlockSpec((tm, w), lambda i, j: (i, 0))
    full = lambda x: pl.BlockSpec(x.shape, lambda i, j: (0, 0))
    return pl.pallas_call(
        _ffn_kernel,
        grid=(T // tm, nf),
        in_specs=[row(D), row(a.shape[1]), row(b.shape[1]), full(wa), full(wb), full(g),
                  pl.BlockSpec((D, tf), lambda i, j: (0, j)), pl.BlockSpec((D, tf), lambda i, j: (0, j)),
                  pl.BlockSpec((tf, D), lambda i, j: (j, 0))],
        out_specs=row(D),
        out_shape=jax.ShapeDtypeStruct((T, D), F32),
        scratch_shapes=[pltpu.VMEM((tm, D), BF16), pltpu.VMEM((tm, D), F32)],
        compiler_params=_cparams(("parallel", "arbitrary")),
        name="mixer_out_ffn",
    )(h, a, b, wa, wb, g, wg, wu, wd)


def _proj1_kernel(h_ref, g_ref, win_ref, sq_ref, sk_ref, sv_ref, bq_ref, bk_ref, bv_ref):
    xn = _rms(h_ref[...], g_ref[...]).astype(BF16)

    def mm(seg):
        return _dot(xn, win_ref[:, CD_COLS[seg]:CD_COLS[seg + 1]])

    sq_ref[...] = (mm(0) * (SWA_DIM ** -0.5)).astype(BF16)
    sk_ref[...] = mm(1).astype(BF16)
    sv_ref[...] = mm(2).astype(BF16)
    bq_ref[...] = (mm(3) * (SB_DIM ** -0.5)).astype(BF16)
    bk_ref[...] = mm(4).astype(BF16)
    bv_ref[...] = mm(5).astype(BF16)


def _proj1(h, g, win):
    T, D = h.shape
    tm = ROW_TILE
    row = lambda w: pl.BlockSpec((tm, w), lambda i: (i, 0))
    full = lambda a: pl.BlockSpec(a.shape, lambda i: (0, 0))
    widths = tuple(np.diff(CD_COLS))
    return pl.pallas_call(
        _proj1_kernel,
        grid=(T // tm,),
        in_specs=[row(D), full(g), full(win)],
        out_specs=[row(w) for w in widths],
        out_shape=[jax.ShapeDtypeStruct((T, w), BF16) for w in widths],
        compiler_params=_cparams(("parallel",)),
        name="proj1",
    )(h, g, win)


def _swa_kernel(tab_ref, q_ref, k_ref, v_ref, o_ref):
    S = q_ref.shape[0]
    W = WINDOW
    j = pl.program_id(1)
    lane = lax.broadcasted_iota(jnp.int32, (1, LANES), 1)
    r2 = lax.broadcasted_iota(jnp.int32, (2 * W, 2 * W), 0)
    c2 = lax.broadcasted_iota(jnp.int32, (2 * W, 2 * W), 1)
    upper = r2 < W
    dist2 = jnp.where(upper, r2, r2 - W) + W - c2
    dist1 = dist2[:, W:]
    heads = [j, j + SWA_HEADS // 2]
    sink = jnp.where(upper[:, :1], tab_ref[heads[0]], tab_ref[heads[1]])
    slope = jnp.where(upper[:, :1], tab_ref[SWA_HEADS + heads[0]], tab_ref[SWA_HEADS + heads[1]])
    bias = [jnp.where((d >= 0) & (d < W), slope * d.astype(F32), jnp.inf) for d in (dist1, dist2)]
    masks = [(lane // 64 == hh).astype(BF16) for hh in range(2)]
    for i in range(S // W):
        q = q_ref[i * W:(i + 1) * W, :]
        lo = max(i - 1, 0) * W
        k, v = k_ref[lo:(i + 1) * W, :], v_ref[lo:(i + 1) * W, :]
        s = _nt_dot(jnp.concatenate([q * masks[0], q * masks[1]], axis=0), k) - bias[min(i, 1)]
        m = jnp.maximum(jnp.max(s, axis=-1, keepdims=True), sink)
        p = jnp.exp(s - m)
        den = jnp.sum(p, axis=-1, keepdims=True) + jnp.exp(sink - m)
        o = _dot(p.astype(BF16), v) / den
        o_ref[i * W:(i + 1) * W, :] = jnp.where(lane < 64, o[:W], o[W:]).astype(BF16)


def _swa_attention(tab, sq, sk, sv, batch, seq):
    T = sq.shape[0]
    blk = lambda f: pl.BlockSpec((seq, LANES), f)
    return pl.pallas_call(
        _swa_kernel,
        grid_spec=pltpu.PrefetchScalarGridSpec(
            num_scalar_prefetch=1,
            grid=(batch, SWA_HEADS // 2),
            in_specs=[blk(lambda b, j, t: (b, j)), blk(lambda b, j, t: (b, 0)), blk(lambda b, j, t: (b, 0))],
            out_specs=blk(lambda b, j, t: (b, j)),
        ),
        out_shape=jax.ShapeDtypeStruct((T, SWA_HEADS * SWA_DIM), BF16),
        compiler_params=_cparams(("parallel", "parallel")),
        name="swa_attention",
    )(tab, sq, sk, sv)


def _sb_kernel(q_ref, k_ref, v_ref, o_ref, *, t):
    S = q_ref.shape[0]
    lane = lax.broadcasted_iota(jnp.int32, (1, LANES), 1)
    row = lax.broadcasted_iota(jnp.int32, (t, t), 0)
    col = lax.broadcasted_iota(jnp.int32, (t, t), 1)
    before = col < row
    neg_later = jnp.where(row > col, -1.0, 0.0).astype(BF16)

    def softplus(z):
        return jnp.maximum(z, 0.0) + jnp.log(1.0 + jnp.exp2(jnp.abs(z) * (-LOG2E)))

    def diag_tile(qm, rows):
        z = _nt_dot(qm, k_ref[rows, :])
        sp = jnp.where(before, softplus(z), 0.0)
        a = jnp.where(before, jnp.exp((z - sp) + _dot(sp.astype(BF16), neg_later)), 0.0)
        return _dot(a.astype(BF16), v_ref[rows, :]), -jnp.sum(sp, axis=-1, keepdims=True)

    def full_tile(qm, krows, acc, run):
        z = _nt_dot(qm, k_ref[krows, :])
        sp = softplus(z)
        a = jnp.exp((z - sp) + (_dot(sp.astype(BF16), neg_later) + run))
        return acc + _dot(a.astype(BF16), v_ref[krows, :]), run - jnp.sum(sp, axis=-1, keepdims=True)

    m = SB_QTILES
    for i in range(S // (m * t)):
        lo = i * m * t
        q = q_ref[lo:lo + m * t, :]
        outs = []
        for hh in range(2):
            qm = q * (lane // 64 == hh).astype(BF16)
            accs, runs = [], []
            for r in range(m):
                qs = qm[r * t:(r + 1) * t]
                acc, run = diag_tile(qs, slice(lo + r * t, lo + (r + 1) * t))
                for jj in range(r - 1, -1, -1):
                    acc, run = full_tile(qs, slice(lo + jj * t, lo + (jj + 1) * t), acc, run)
                accs.append(acc)
                runs.append(run)
            acc = jnp.concatenate(accs, axis=0)
            run = jnp.concatenate(runs, axis=0)
            for j in range(i * m - 1, -1, -1):
                acc, run = full_tile(qm, slice(j * t, (j + 1) * t), acc, run)
            outs.append(acc)
        o_ref[lo:lo + m * t, :] = jnp.where(lane < 64, outs[0], outs[1]).astype(BF16)


def _sb_attention(bq, bk, bv, batch, seq):
    T = bq.shape[0]
    blk = pl.BlockSpec((seq, LANES), lambda b, p: (b, p))
    return pl.pallas_call(
        functools.partial(_sb_kernel, t=SB_T),
        grid=(batch, SB_HEADS // 2),
        in_specs=[blk, blk, blk],
        out_specs=blk,
        out_shape=jax.ShapeDtypeStruct((T, SB_HEADS * SB_DIM), BF16),
        compiler_params=_cparams(("parallel", "parallel")),
        name="sb_attention",
    )(bq, bk, bv)


def _split_bf16(x):
    hi = x.astype(BF16)
    return hi, (x - hi.astype(F32)).astype(BF16)


def _out_router_kernel(h_ref, a_ref, b_ref, wa_ref, wb_ref, g_ref, rcat_ref, o_ref, route_ref):
    tm = h_ref.shape[0]
    for rows in (slice(0, tm // 2), slice(tm // 2, tm)):
        hn = h_ref[rows, :] + _dot(a_ref[rows, :], wa_ref[...]) + _dot(b_ref[rows, :], wb_ref[...])
        o_ref[rows, :] = hn
        xhi, xlo = _split_bf16(_rms(hn, g_ref[...]))
        both = _dot(xhi, rcat_ref[...])
        logits = both[:, :LANES] + (both[:, LANES:] + _dot(xlo, rcat_ref[:, :LANES]))
        lane = lax.broadcasted_iota(jnp.int32, logits.shape, 1)
        logits = jnp.where(lane < N_EXPERTS, logits, -jnp.inf)
        m1 = jnp.max(logits, axis=-1, keepdims=True)
        i1 = jnp.min(jnp.where(logits == m1, lane, LANES), axis=-1, keepdims=True)
        rest = jnp.where(lane == i1, -jnp.inf, logits)
        m2 = jnp.max(rest, axis=-1, keepdims=True)
        i2 = jnp.min(jnp.where(rest == m2, lane, LANES), axis=-1, keepdims=True)
        e2 = jnp.exp(m2 - m1)
        den = 1.0 + e2
        route = jnp.where(lane == 0, i1.astype(F32), 0.0)
        route = jnp.where(lane == 1, i2.astype(F32), route)
        route = jnp.where(lane == 2, 1.0 / den, route)
        route = jnp.where(lane == 3, e2 / den, route)
        route_ref[rows, :] = route


def _mixer_out_router(h, a, b, wa, wb, g, rcat):
    T, D = h.shape
    tm = ROW_TILE
    row = lambda w: pl.BlockSpec((tm, w), lambda i: (i, 0))
    full = lambda x: pl.BlockSpec(x.shape, lambda i: (0, 0))
    return pl.pallas_call(
        _out_router_kernel,
        grid=(T // tm,),
        in_specs=[row(D), row(a.shape[1]), row(b.shape[1]), full(wa), full(wb), full(g), full(rcat)],
        out_specs=[row(D), row(LANES)],
        out_shape=[jax.ShapeDtypeStruct((T, D), F32), jax.ShapeDtypeStruct((T, LANES), F32)],
        compiler_params=_cparams(("parallel",)),
        name="mixer_out_router",
    )(h, a, b, wa, wb, g, rcat)


def _rank_kernel(route_ref, rank_ref, counts_ref):
    tm = route_ref.shape[0]

    @pl.when(pl.program_id(0) == 0)
    def _():
        counts_ref[...] = jnp.zeros_like(counts_ref)

    route = route_ref[...]
    lane = lax.broadcasted_iota(jnp.int32, route.shape, 1).astype(F32)
    oh0 = jnp.where(lane == route[:, 0:1], 1.0, 0.0)
    oh1 = jnp.where(lane == route[:, 1:2], 1.0, 0.0)
    both = oh0 + oh1
    earlier = (lax.broadcasted_iota(jnp.int32, (tm, tm), 1) < lax.broadcasted_iota(jnp.int32, (tm, tm), 0))
    prefix = _dot(earlier.astype(BF16), both.astype(BF16)) + counts_ref[0:1, :]
    r0 = jnp.sum(oh0 * prefix, axis=-1, keepdims=True)
    r1 = jnp.sum(oh1 * prefix, axis=-1, keepdims=True)
    rank_ref[...] = jnp.where(lane == 0.0, r0, jnp.where(lane == 1.0, r1, 0.0))
    counts_ref[...] += jnp.sum(both, axis=0, keepdims=True)


def _routing_ranks(route):
    T = route.shape[0]
    tm = RANK_ROWS
    return pl.pallas_call(
        _rank_kernel,
        grid=(T // tm,),
        in_specs=[pl.BlockSpec((tm, LANES), lambda i: (i, 0))],
        out_specs=[pl.BlockSpec((tm, LANES), lambda i: (i, 0)), pl.BlockSpec((8, LANES), lambda i: (0, 0))],
        out_shape=[jax.ShapeDtypeStruct((T, LANES), F32), jax.ShapeDtypeStruct((8, LANES), F32)],
        compiler_params=_cparams(("arbitrary",)),
        name="routing_ranks",
    )(route)


def _wait_rows(src_hbm, dst_ref, sem):
    pltpu.make_async_copy(src_hbm.at[pl.ds(0, dst_ref.shape[0])], dst_ref, sem).wait()


def _cast_kernel(w_ref, o_ref):
    o_ref[...] = w_ref[...].astype(BF16)


def _to_bf16(w):
    E, K, N = w.shape
    blk = pl.BlockSpec((1, CAST_ROWS, N), lambda e, k: (e, k, 0))
    return pl.pallas_call(
        _cast_kernel,
        grid=(E, K // CAST_ROWS),
        in_specs=[blk],
        out_specs=blk,
        out_shape=jax.ShapeDtypeStruct(w.shape, BF16),
        compiler_params=_cparams(("parallel", "parallel")),
        name="cast_bf16",
    )(w)


def _dispatch_kernel(meta_ref, dest_ref, wg_ref, wu_ref, wd_ref, h_hbm, og_ref, ou_ref, od_ref, xg_hbm,
                     zero_ref, hbuf, hsem, sem, zsem):
    step = pl.program_id(0) * pl.num_programs(1) + pl.program_id(1)
    nsteps = pl.num_programs(0) * pl.num_programs(1)
    tpb = hbuf.shape[1]
    R = zero_ref.shape[0]
    slot = step % DISPATCH_BUFFERS

    def zero_block(start):
        return pltpu.make_async_copy(zero_ref, xg_hbm.at[pl.ds(pl.multiple_of(start, R), R)], zsem)

    def h_block(s):
        return pltpu.make_async_copy(h_hbm.at[pl.ds(pl.multiple_of(s * tpb, tpb), tpb)],
                                     hbuf.at[s % DISPATCH_BUFFERS], hsem.at[s % DISPATCH_BUFFERS])

    def wait_scatter(s):
        for k in range(TOP_K):
            pltpu.make_async_copy(hbuf.at[s % DISPATCH_BUFFERS], xg_hbm.at[pl.ds(0, tpb)],
                                  sem.at[s % DISPATCH_BUFFERS]).wait()

    cap = xg_hbm.shape[0]
    fills = [(meta_ref[N_EXPERTS + e] > 0, meta_ref[e] - R) for e in range(N_EXPERTS)]
    fills += [(meta_ref[N_EXPERTS - 1] + b * R < cap, meta_ref[N_EXPERTS - 1] + b * R) for b in range(N_EXPERTS)]

    @pl.when(step == 0)
    def _():
        h_block(step).start()
        zero_ref[...] = jnp.zeros_like(zero_ref)
        for do_start in (True, False):
            for cond, start in fills:
                @pl.when(cond)
                def _():
                    zero_block(start).start() if do_start else zero_block(start).wait()

    @pl.when(step >= DISPATCH_BUFFERS - 1)
    def _():
        wait_scatter(step - (DISPATCH_BUFFERS - 1))

    @pl.when(step + 1 < nsteps)
    def _():
        h_block(step + 1).start()

    h_block(step).wait()
    src = hbuf.at[slot]

    def scatter_group(gi, carry):
        base = gi * DISPATCH_UNROLL
        rows = [[dest_ref[0, 0, TOP_K * (base + u) + k] for k in range(TOP_K)]
                for u in range(DISPATCH_UNROLL)]
        for u in range(DISPATCH_UNROLL):
            for k in range(TOP_K):
                pltpu.make_async_copy(src.at[pl.ds(base + u, 1)], xg_hbm.at[pl.ds(rows[u][k], 1)],
                                      sem.at[slot]).start(priority=1)
        return carry

    lax.fori_loop(0, tpb // DISPATCH_UNROLL, scatter_group, 0)
    og_ref[...] = wg_ref[...].astype(BF16)
    ou_ref[...] = wu_ref[...].astype(BF16)
    od_ref[...] = wd_ref[...].astype(BF16)

    @pl.when(step == nsteps - 1)
    def _():
        for back in range(DISPATCH_BUFFERS - 2, -1, -1):
            @pl.when(step - back >= 0)
            def _():
                wait_scatter(step - back)


def _dispatch_cast(meta, dest, wg, wu, wd, h, cap):
    E, D, F = wg.shape
    nsteps, _, n = dest.shape
    nk = nsteps // E
    spec = lambda w: pl.BlockSpec((1, w.shape[1] // nk, w.shape[2]), lambda e, k, m: (e, k, 0))
    hbm = pl.BlockSpec(memory_space=pl.ANY)
    nbuf = DISPATCH_BUFFERS
    return pl.pallas_call(
        _dispatch_kernel,
        grid_spec=pltpu.PrefetchScalarGridSpec(
            num_scalar_prefetch=1,
            grid=(E, nk),
            in_specs=[pl.BlockSpec((1, 1, n), lambda e, k, m: (e * nk + k, 0, 0), memory_space=pltpu.SMEM),
                      spec(wg), spec(wu), spec(wd), hbm],
            out_specs=[spec(wg), spec(wu), spec(wd), hbm],
            scratch_shapes=[pltpu.VMEM((MOE_ROWS, D), F32), pltpu.VMEM((nbuf, n // TOP_K, D), F32),
                            pltpu.SemaphoreType.DMA((nbuf,)), pltpu.SemaphoreType.DMA((nbuf,)),
                            pltpu.SemaphoreType.DMA],
        ),
        out_shape=[jax.ShapeDtypeStruct(wg.shape, BF16), jax.ShapeDtypeStruct(wu.shape, BF16),
                   jax.ShapeDtypeStruct(wd.shape, BF16), jax.ShapeDtypeStruct((cap, D), F32)],
        compiler_params=_cparams(("arbitrary", "arbitrary")),
        name="dispatch_cast",
    )(meta, dest, wg, wu, wd, h)


def _moe_kernel(sched_ref, x_ref, xnext_ref, g_ref, wg_ref, wu_ref, wd_ref, o_ref, xn_ref, acc_ref):
    i, j = pl.program_id(0), pl.program_id(1)
    nblk, nf = pl.num_programs(0), pl.num_programs(1)
    used = i < sched_ref[nblk]
    slot = i % 2
    share = x_ref.shape[0] // nf

    @pl.when((i == 0) & (j == 0))
    def _():
        xn_ref[0] = _rms(x_ref[...], g_ref[...]).astype(BF16)

    @pl.when(j == 0)
    def _():
        acc_ref[...] = jnp.zeros_like(acc_ref)

    @pl.when(used)
    def _():
        xn = xn_ref[slot]
        gate = _dot(xn, wg_ref[0])
        act = gate * (1.0 / (1.0 + jnp.exp(-gate))) * _dot(xn, wu_ref[0])
        acc_ref[...] += _dot(act.astype(BF16), wd_ref[0])
        rows = pl.ds(pl.multiple_of(j * share, share), share)
        xn_ref[1 - slot, rows, :] = _rms(xnext_ref[rows, :], g_ref[...]).astype(BF16)

    @pl.when(j == nf - 1)
    def _():
        o_ref[...] = acc_ref[...]


def _moe_experts(sched, xg, g, wg, wu, wd):
    cap, D = xg.shape
    F = wg.shape[2]
    R, nf = MOE_ROWS, MOE_FF_STEPS
    nblk, tf = cap // R, F // nf

    def blk_i(i, s):
        return jnp.minimum(i, s[nblk] - 1)

    def col_j(i, j, s):
        return jnp.where(i < s[nblk], j, nf - 1)

    return pl.pallas_call(
        _moe_kernel,
        grid_spec=pltpu.PrefetchScalarGridSpec(
            num_scalar_prefetch=1,
            grid=(nblk, nf),
            in_specs=[pl.BlockSpec((R, D), lambda i, j, s: (blk_i(i, s), 0)),
                      pl.BlockSpec((R, D), lambda i, j, s: (blk_i(i + 1, s), 0)),
                      pl.BlockSpec(g.shape, lambda i, j, s: (0, 0)),
                      pl.BlockSpec((1, D, tf), lambda i, j, s: (s[blk_i(i, s)], 0, col_j(i, j, s))),
                      pl.BlockSpec((1, D, tf), lambda i, j, s: (s[blk_i(i, s)], 0, col_j(i, j, s))),
                      pl.BlockSpec((1, tf, D), lambda i, j, s: (s[blk_i(i, s)], col_j(i, j, s), 0))],
            out_specs=pl.BlockSpec((R, D), lambda i, j, s: (i, 0)),
            scratch_shapes=[pltpu.VMEM((2, R, D), BF16), pltpu.VMEM((R, D), F32)],
        ),
        out_shape=jax.ShapeDtypeStruct((cap, D), F32),
        compiler_params=_cparams(("arbitrary", "arbitrary")),
        name="moe_experts",
    )(sched, xg, xg, g, wg, wu, wd)


def _combine_kernel(pos_ref, nxt_ref, h_ref, route_ref, g_ref, y_hbm, o_ref, ybuf0, ybuf1, sem):
    i = pl.program_id(0)
    n = ybuf0.shape[0]
    tm = n // TOP_K

    def issue(idx_ref, base, buf, s):
        for g0 in range(0, n, COMBINE_GROUP):
            rows = [idx_ref[0, 0, base + TOP_K * ((g0 + r) % tm) + (g0 + r) // tm] for r in range(COMBINE_GROUP)]
            for r in range(COMBINE_GROUP):
                pltpu.make_async_copy(y_hbm.at[pl.ds(rows[r], 1)], buf.at[pl.ds(g0 + r, 1)],
                                      sem.at[s]).start(priority=r % 2)

    def finish(buf, s, rows):
        _wait_rows(y_hbm, buf, sem.at[s])
        route = route_ref[rows, :]
        out = h_ref[rows, :] + (buf[:tm, :] * route[:, 2:3] + buf[tm:, :] * route[:, 3:4])
        o_ref[rows, :] = _rms(out, g_ref[...])

    @pl.when(i == 0)
    def _():
        issue(pos_ref, 0, ybuf0, 0)

    issue(pos_ref, n, ybuf1, 1)
    finish(ybuf0, 0, slice(0, tm))
    issue(nxt_ref, 0, ybuf0, 0)
    finish(ybuf1, 1, slice(tm, 2 * tm))

    @pl.when(i == pl.num_programs(0) - 1)
    def _():
        _wait_rows(y_hbm, ybuf0, sem.at[0])


def _combine_norm(pos, h, route, g, yg):
    T, D = h.shape
    nstep, _, n2 = pos.shape
    n = n2 // 2
    tm2 = n2 // TOP_K
    row = lambda w: pl.BlockSpec((tm2, w), lambda i: (i, 0))
    smem = lambda f: pl.BlockSpec((1, 1, n2), f, memory_space=pltpu.SMEM)
    return pl.pallas_call(
        _combine_kernel,
        grid=(nstep,),
        in_specs=[smem(lambda i: (i, 0, 0)), smem(lambda i: (jnp.minimum(i + 1, nstep - 1), 0, 0)),
                  row(D), row(LANES), pl.BlockSpec(g.shape, lambda i: (0, 0)), pl.BlockSpec(memory_space=pl.ANY)],
        out_specs=row(D),
        out_shape=jax.ShapeDtypeStruct((T, D), F32),
        scratch_shapes=[pltpu.VMEM((n, D), F32), pltpu.VMEM((n, D), F32), pltpu.SemaphoreType.DMA((2,))],
        compiler_params=_cparams(("arbitrary",)),
        name="combine_norm",
    )(pos, pos, h, route, g, yg)


def _rope_tables(seq, half):
    inv_freq = ROPE_BASE ** (-jnp.arange(half, dtype=F32) / half)
    ang = jnp.arange(seq).astype(F32)[:, None] * inv_freq[None, :]
    cos, sin = jnp.cos(ang), jnp.sin(ang)
    reps = LANES // (2 * half)
    return jnp.tile(jnp.concatenate([cos, cos], 1), (1, reps)), jnp.tile(jnp.concatenate([-sin, sin], 1), (1, reps))


def _retention_tables():
    log_gamma = jnp.log1p(-(2.0 ** (-5.0 - jnp.arange(RET_HEADS, dtype=F32))))
    idx = jnp.arange(CHUNK, dtype=F32)
    rel = idx[:, None] - idx[None, :]
    din = jnp.where(rel >= 0, jnp.exp(log_gamma[:, None, None] * jnp.maximum(rel, 0.0)), 0.0)
    qd = jnp.exp(log_gamma[:, None] * (idx + 1.0))
    kd = jnp.exp(log_gamma[:, None] * (CHUNK - 1.0 - idx))
    cd = jnp.exp(log_gamma * CHUNK)
    bc = lambda a: jnp.broadcast_to(a[:, :, None], (RET_HEADS, CHUNK, LANES))
    return din, bc(qd), bc(kd), jnp.broadcast_to(cd[:, None, None], (RET_HEADS, 1, LANES))


def kernel(x, ab_norm, ab_w_in, mla_q_norm, mla_w_uq, mla_kv_norm, mla_w_ukv, ret_gn, ab_w_out, ffn_norm,
           ffn_w_gate, ffn_w_up, ffn_w_down, cd_norm, cd_w_in, swa_sinks, cd_w_out, moe_norm, moe_router,
           moe_w_gate, moe_w_up, moe_w_down, final_norm):
    B, S, D = x.shape
    T = B * S
    h = x.reshape(T, D)
    row = lambda g: g.reshape(1, -1)

    w = ab_w_in[0]
    lat = MLA_Q_RANK + MLA_KV_RANK
    win0 = jnp.concatenate([w[:, :lat], jnp.tile(w[:, lat:lat + MLA_ROPE], (1, LANES // MLA_ROPE)),
                            w[:, lat + MLA_ROPE:]], axis=1).astype(BF16)
    wuq = mla_w_uq[0].reshape(MLA_Q_RANK, MLA_HEADS, MLA_NOPE + MLA_ROPE)
    wuq = jnp.concatenate([wuq[:, :, :MLA_NOPE].reshape(MLA_Q_RANK, -1),
                           wuq[:, :, MLA_NOPE:].reshape(MLA_Q_RANK, -1)], axis=1).astype(BF16)
    wukv = mla_w_ukv[0].reshape(MLA_KV_RANK, MLA_HEADS, MLA_NOPE + MLA_V)
    wukv = jnp.concatenate([wukv[:, :, :MLA_NOPE].reshape(MLA_KV_RANK, -1),
                            wukv[:, :, MLA_NOPE:].reshape(MLA_KV_RANK, -1)], axis=1).astype(BF16)
    cm, sm = _rope_tables(S, MLA_ROPE // 2)
    cr, sr = _rope_tables(S, RET_DK // 2)
    qnope, qrope, knope, krope, v, rq, rk, rv, rg = _proj0(
        h, row(ab_norm[0]), win0, row(mla_q_norm[0]), wuq, row(mla_kv_norm[0]), wukv, cm, sm, cr, sr, S)
    mla = _mla_attention(qnope, qrope, knope, krope, v, B, S)
    ret = _retention(rq, rk, rv, rg, row(ret_gn[0]), *_retention_tables(), B, S)
    wo = ab_w_out[0].astype(BF16)
    nm = MLA_HEADS * MLA_V
    h = _mixer_out_ffn(h, mla, ret, wo[:nm], wo[nm:], row(ffn_norm[0]), _to_bf16(ffn_w_gate)[0],
                       _to_bf16(ffn_w_up)[0], _to_bf16(ffn_w_down)[0])

    w = cd_w_in[0]
    nq = SWA_HEADS * SWA_DIM
    pair_order = np.stack([np.arange(SWA_HEADS // 2), np.arange(SWA_HEADS // 2) + SWA_HEADS // 2], 1).reshape(-1)
    cols = (pair_order[:, None] * SWA_DIM + np.arange(SWA_DIM)[None, :]).reshape(-1)
    win1 = jnp.concatenate([w[:, :nq][:, cols], w[:, nq:]], axis=1).astype(BF16)
    sq, sk, sv, bq, bk, bv = _proj1(h, row(cd_norm[0]), win1)
    slopes = 2.0 ** (-8.0 * jnp.arange(1, SWA_HEADS + 1, dtype=F32) / SWA_HEADS)
    swa = _swa_attention(jnp.concatenate([swa_sinks[0].astype(F32), slopes]), sq, sk, sv, B, S)
    sb = _sb_attention(bq, bk, bv, B, S)
    wo = cd_w_out[0]
    wo_swa = wo[:nq][cols].astype(BF16)
    router = jnp.pad(moe_router[0], ((0, 0), (0, LANES - N_EXPERTS)))
    rhi = router.astype(BF16)
    rlo = (router - rhi.astype(F32)).astype(BF16)
    h, route = _mixer_out_router(h, swa, sb, wo_swa, wo[nq:].astype(BF16), row(moe_norm[0]),
                                 jnp.concatenate([rhi, rlo], axis=1))

    R = MOE_ROWS
    TK = T * TOP_K
    ranks, counts = _routing_ranks(route)
    counts = counts[0, :N_EXPERTS].astype(jnp.int32)
    padded = ((counts + R - 1) // R) * R
    eidx = jnp.arange(N_EXPERTS, dtype=jnp.int32)
    pad_end = jnp.sum(jnp.where(eidx[None, :] <= eidx[:, None], padded[None, :], 0), axis=1)
    pad_start = pad_end - padded
    expert = route[:, :TOP_K].astype(jnp.int32)
    start_of = jnp.sum(jnp.where(expert[:, :, None] == eidx, pad_start, 0), axis=-1)
    dest = (start_of + ranks[:, :TOP_K].astype(jnp.int32)).reshape(-1)
    cap = -(-TK // R) * R + N_EXPERTS * R
    nblk = cap // R
    blk_start = jnp.arange(nblk, dtype=jnp.int32) * R
    blk_exp = jnp.minimum(jnp.sum((blk_start[:, None] >= pad_end[None, :]).astype(jnp.int32), axis=1), N_EXPERTS - 1)
    sched = jnp.concatenate([blk_exp, (pad_end[-1:] // R).astype(jnp.int32)])

    def tiles(tm):
        return dest.reshape(T // tm, 1, TOP_K * tm)

    meta = jnp.concatenate([pad_end, padded]).astype(jnp.int32)
    wg, wu, wd, xg = _dispatch_cast(meta, tiles(T // DISPATCH_STEPS), moe_w_gate[0], moe_w_up[0], moe_w_down[0],
                                    h, cap)
    yg = _moe_experts(sched, xg, row(moe_norm[0]), wg, wu, wd)
    pos = tiles(2 * COMBINE_ROWS)
    out = _combine_norm(pos, h, route, row(final_norm), yg)
    return out.reshape(B, S, D)
```
